```python
import jax, jax.numpy as jnp
from jax import lax
import numpy as np

D_MODEL = 2048
BATCH = 2
SEQ = 16384
DEPTH = 2

GRID_W = 64
CTX_LEN = 256
N_MIXERS = 2
N_REC_LAYERS = (DEPTH + 1) // 2
N_CONV_LAYERS = DEPTH // 2
D_RNN = 2560
RG_HEADS = 10
RG_HEAD_DIM = D_RNN // RG_HEADS
RG_CONV_W = 4
RG_PAD = (1, 2)
RG_C = 8.0
CF_WIDTH = 31
CF_PAD = (15, 15)
N_EXPERTS = 64
N_GROUPS = 8
TOPK_GROUPS = 4
TOP_K = 8
EXPERT_HIDDEN = 512
SHARED_HIDDEN = 512
ROUTED_SCALE = 2.5
MOE_BLOCK = 128
ALPHA = (2 * DEPTH) ** 0.25
BETA = (8 * DEPTH) ** -0.25
LN_EPS = 1e-5

kernel_name = 'hybrid_rglru_conformer_moe_dit'


def layer_norm(h, g, b):
    hf = h.astype(jnp.float32)
    mu = jnp.mean(hf, axis=-1, keepdims=True)
    var = jnp.mean(jnp.square(hf - mu), axis=-1, keepdims=True)
    return ((hf - mu) * lax.rsqrt(var + LN_EPS) * g.astype(jnp.float32) + b.astype(jnp.float32)).astype(h.dtype)


def modulate(h, shift, scale):
    return h * (1.0 + scale) + shift


def depthwise_conv(h, w, b, pad):
    out = lax.conv_general_dilated(h, w[:, None, :].astype(h.dtype), window_strides=(1,), padding=[pad],
                                   dimension_numbers=('NWC', 'WIO', 'NWC'), feature_group_count=h.shape[-1])
    return out + b


def _scan_combine(left, right):
    a_l, h_l = left
    a_r, h_r = right
    return a_l * a_r, a_r * h_l + h_r


def linear_scan(a, u, h0, reverse):
    a_cum, h = lax.associative_scan(_scan_combine, (a, u), axis=1, reverse=reverse)
    return h + a_cum * h0[:, None, :]


def rglru_coeffs(xb, w_gates, b_gates, lam):
    nb, ns, _ = xb.shape
    xh = xb.reshape(nb, ns, RG_HEADS, RG_HEAD_DIM)
    pre = jnp.einsum('bshi,dghij->dgbshj', xh, w_gates) + b_gates[:, :, None, None]
    pre = pre.astype(jnp.float32).reshape(2, 2, nb, ns, D_RNN)
    r = jax.nn.sigmoid(pre[:, 0])
    gi = jax.nn.sigmoid(pre[:, 1])
    log_a = -RG_C * r * jax.nn.softplus(-lam.astype(jnp.float32))[:, None, None, :]
    a = jnp.exp(log_a)
    u = jnp.sqrt(-jnp.expm1(2.0 * log_a)) * gi * xb.astype(jnp.float32)[None]
    return a, u


def bidir_rglru(hx, hc, w_in, b_in, conv_w, conv_b, w_gates, b_gates, lam, w_out, b_out, ctx_out):
    def branches(h):
        gate_br, rnn_br = jnp.split(h @ w_in + b_in, 2, axis=-1)
        a, u = rglru_coeffs(depthwise_conv(rnn_br, conv_w, conv_b, RG_PAD), w_gates, b_gates, lam)
        return gate_br, a, u

    def readout(h_sum, gate_br):
        return (h_sum.astype(gate_br.dtype) * jax.nn.gelu(gate_br)) @ w_out + b_out

    gc, ac, uc = branches(hc)
    h0 = jnp.zeros((hc.shape[0], D_RNN), jnp.float32)
    hcf = linear_scan(ac[0], uc[0], h0, False)
    hcb = linear_scan(ac[1], uc[1], h0, True)
    gx, ax, ux = branches(hx)
    hxf = linear_scan(ax[0], ux[0], hcf[:, -1], False)
    hxb = linear_scan(ax[1], ux[1], hcb[:, 0], True)
    yx = readout(hxf + hxb, gx)
    if not ctx_out:
        return yx, None
    return yx, readout(hcf + hcb, gc)


def conformer_conv(h, w_pw1, b_pw1, w_dw, b_dw, n_g, n_b, w_pw2, b_pw2):
    val, gate = jnp.split(h @ w_pw1 + b_pw1, 2, axis=-1)
    z = val * jax.nn.sigmoid(gate)
    z = depthwise_conv(z, w_dw, b_dw, CF_PAD)
    z = jax.nn.silu(layer_norm(z, n_g, n_b))
    return z @ w_pw2 + b_pw2


def swiglu(h, w_gate, w_up, w_down):
    return (jax.nn.silu(h @ w_gate) * (h @ w_up)) @ w_down


def route(tokens, w_router, b_corr):
    n_tok = tokens.shape[0]
    per_group = N_EXPERTS // N_GROUPS
    scores = jax.nn.sigmoid((tokens @ w_router).astype(jnp.float32))
    biased = scores + b_corr.astype(jnp.float32)
    group_score = lax.top_k(biased.reshape(n_tok, N_GROUPS, per_group), 2)[0].sum(-1)
    _, top_groups = lax.top_k(group_score, TOPK_GROUPS)
    group_mask = jax.nn.one_hot(top_groups, N_GROUPS, dtype=jnp.float32).sum(1) > 0
    expert_mask = jnp.repeat(group_mask, per_group, axis=1)
    _, idx = lax.top_k(jnp.where(expert_mask, biased, -jnp.inf), TOP_K)
    w = jnp.take_along_axis(scores, idx, axis=1)
    w = w / jnp.sum(w, axis=-1, keepdims=True) * ROUTED_SCALE
    return idx, w


def routed_experts(tokens, idx, w, w_gate, w_up, w_down):
    n_tok, d = tokens.shape
    n_assign = n_tok * TOP_K
    flat_e = idx.reshape(-1)
    flat_tok = jnp.arange(n_assign, dtype=jnp.int32) // TOP_K
    flat_w = w.reshape(-1)
    order = jnp.argsort(flat_e)
    e_sorted = flat_e[order]
    counts = jnp.bincount(flat_e, length=N_EXPERTS)
    starts = jnp.cumsum(counts) - counts
    padded = (counts + MOE_BLOCK - 1) // MOE_BLOCK * MOE_BLOCK
    pad_ends = jnp.cumsum(padded)
    pad_starts = pad_ends - padded
    dest = pad_starts[e_sorted] + jnp.arange(n_assign, dtype=jnp.int32) - starts[e_sorted]
    n_blocks = -(-n_assign // MOE_BLOCK) + N_EXPERTS
    n_slots = n_blocks * MOE_BLOCK
    slot_tok = jnp.full((n_slots,), n_tok, jnp.int32).at[dest].set(flat_tok[order])
    slot_w = jnp.zeros((n_slots,), tokens.dtype).at[dest].set(flat_w[order].astype(tokens.dtype))
    block_start = jnp.arange(n_blocks, dtype=jnp.int32) * MOE_BLOCK
    block_e = jnp.minimum(jnp.searchsorted(pad_ends, block_start, side='right'), N_EXPERTS - 1).astype(jnp.int32)
    x_pad = jnp.concatenate([tokens, jnp.zeros((1, d), tokens.dtype)], axis=0)

    def body(acc, blk):
        tok, wt, e = blk
        xb = x_pad[tok]
        yb = swiglu(xb, w_gate[e], w_up[e], w_down[e]) * wt[:, None]
        return acc.at[tok].add(yb), None

    acc, _ = lax.scan(body, jnp.zeros((n_tok + 1, d), tokens.dtype),
                      (slot_tok.reshape(n_blocks, MOE_BLOCK), slot_w.reshape(n_blocks, MOE_BLOCK), block_e))
    return acc[:n_tok]


def moe(tokens, w_router, b_router, w_gate, w_up, w_down, ws_gate, ws_up, ws_down):
    idx, w = route(tokens, w_router, b_router)
    return routed_experts(tokens, idx, w, w_gate, w_up, w_down) + swiglu(tokens, ws_gate, ws_up, ws_down)


def setup_inputs(seed: int = 0) -> dict:
    key = jax.random.key(seed)
    ks = jax.random.split(key, 40)
    D = D_MODEL
    NR, NC = N_REC_LAYERS, N_CONV_LAYERS

    def nrm(k, shape, scale):
        return jax.random.normal(k, shape, jnp.float32) * scale

    lam_u = jax.random.uniform(ks[16], (NR, 2, D_RNN), jnp.float32, 0.9, 0.999)
    lam_s = lam_u ** (1.0 / RG_C)
    return {
        'x': nrm(ks[0], (BATCH, SEQ, D), 1.0),
        'c': nrm(ks[1], (BATCH, D), 1.0),
        'ctx': nrm(ks[2], (BATCH, CTX_LEN, D), 1.0),
        'c_ctx': nrm(ks[3], (D,), 1.0),
        'w_mod': nrm(ks[4], (DEPTH, D, 6 * D), 0.5 * D ** -0.5),
        'b_mod': nrm(ks[5], (DEPTH, 6 * D), 0.02),
        'ln1_g': 1.0 + nrm(ks[6], (DEPTH, D), 0.02),
        'ln1_b': nrm(ks[7], (DEPTH, D), 0.02),
        'ln2_g': 1.0 + nrm(ks[8], (DEPTH, D), 0.02),
        'ln2_b': nrm(ks[9], (DEPTH, D), 0.02),
        'rg_w_in': nrm(ks[10], (NR, D, 2 * D_RNN), D ** -0.5),
        'rg_b_in': nrm(ks[11], (NR, 2 * D_RNN), 0.02),
        'rg_conv_w': nrm(ks[12], (NR, RG_CONV_W, D_RNN), RG_CONV_W ** -0.5),
        'rg_conv_b': nrm(ks[13], (NR, D_RNN), 0.02),
        'rg_w_gates': nrm(ks[14], (NR, 2, 2, RG_HEADS, RG_HEAD_DIM, RG_HEAD_DIM), RG_HEAD_DIM ** -0.5),
        'rg_b_gates': nrm(ks[15], (NR, 2, 2, RG_HEADS, RG_HEAD_DIM), 0.02),
        'rg_lambda': jnp.log(lam_s) - jnp.log1p(-lam_s),
        'rg_w_out': nrm(ks[17], (NR, D_RNN, D), BETA * D_RNN ** -0.5),
        'rg_b_out': nrm(ks[18], (NR, D), 0.02),
        'cf_w_pw1': nrm(ks[19], (NC, D, 2 * D), D ** -0.5),
        'cf_b_pw1': nrm(ks[20], (NC, 2 * D), 0.02),
        'cf_w_dw': nrm(ks[21], (NC, CF_WIDTH, D), CF_WIDTH ** -0.5),
        'cf_b_dw': nrm(ks[22], (NC, D), 0.02),
        'cf_norm_g': 1.0 + nrm(ks[23], (NC, D), 0.02),
        'cf_norm_b': nrm(ks[24], (NC, D), 0.02),
        'cf_w_pw2': nrm(ks[25], (NC, D, D), BETA * D ** -0.5),
        'cf_b_pw2': nrm(ks[26], (NC, D), 0.02),
        'moe_w_router': nrm(ks[27], (DEPTH, D, N_EXPERTS), D ** -0.5),
        'moe_b_router': nrm(ks[28], (DEPTH, N_EXPERTS), 0.01),
        'moe_w_gate': nrm(ks[29], (DEPTH, N_EXPERTS, D, EXPERT_HIDDEN), D ** -0.5),
        'moe_w_up': nrm(ks[30], (DEPTH, N_EXPERTS, D, EXPERT_HIDDEN), D ** -0.5),
        'moe_w_down': nrm(ks[31], (DEPTH, N_EXPERTS, EXPERT_HIDDEN, D), BETA * EXPERT_HIDDEN ** -0.5),
        'sh_w_gate': nrm(ks[32], (DEPTH, D, SHARED_HIDDEN), D ** -0.5),
        'sh_w_up': nrm(ks[33], (DEPTH, D, SHARED_HIDDEN), D ** -0.5),
        'sh_w_down': nrm(ks[34], (DEPTH, SHARED_HIDDEN, D), BETA * SHARED_HIDDEN ** -0.5),
    }


def reference(x, c, ctx, c_ctx, w_mod, b_mod, ln1_g, ln1_b, ln2_g, ln2_b,
              rg_w_in, rg_b_in, rg_conv_w, rg_conv_b, rg_w_gates, rg_b_gates, rg_lambda, rg_w_out, rg_b_out,
              cf_w_pw1, cf_b_pw1, cf_w_dw, cf_b_dw, cf_norm_g, cf_norm_b, cf_w_pw2, cf_b_pw2,
              moe_w_router, moe_b_router, moe_w_gate, moe_w_up, moe_w_down, sh_w_gate, sh_w_up, sh_w_down):
    silu_c = jax.nn.silu(c)
    silu_cc = jax.nn.silu(c_ctx)
    for i in range(DEPTH):
        mixer, j = i % N_MIXERS, i // N_MIXERS
        ctx_later = any(k % N_MIXERS == 0 for k in range(i + 1, DEPTH))
        sh1, sc1, g1, sh2, sc2, g2 = jnp.split((silu_c @ w_mod[i] + b_mod[i])[:, None, :], 6, axis=-1)
        if mixer == 0 or ctx_later:
            csh1, csc1, cg1, csh2, csc2, cg2 = jnp.split(silu_cc @ w_mod[i] + b_mod[i], 6)
            hc = modulate(ctx, csh1, csc1)
        hx = modulate(x, sh1, sc1)
        if mixer == 0:
            yx, yc = bidir_rglru(hx, hc, rg_w_in[j], rg_b_in[j], rg_conv_w[j], rg_conv_b[j], rg_w_gates[j],
                                 rg_b_gates[j], rg_lambda[j], rg_w_out[j], rg_b_out[j], ctx_later)
        else:
            cf = (cf_w_pw1[j], cf_b_pw1[j], cf_w_dw[j], cf_b_dw[j], cf_norm_g[j], cf_norm_b[j], cf_w_pw2[j], cf_b_pw2[j])
            yx = conformer_conv(hx, *cf)
            if ctx_later:
                yc = conformer_conv(hc, *cf)
        x = layer_norm(ALPHA * x + g1 * yx, ln1_g[i], ln1_b[i])
        n_x = x.shape[0] * x.shape[1]
        tokens = modulate(x, sh2, sc2).reshape(n_x, D_MODEL)
        if ctx_later:
            ctx = layer_norm(ALPHA * ctx + cg1 * yc, ln1_g[i], ln1_b[i])
            tokens = jnp.concatenate([tokens, modulate(ctx, csh2, csc2).reshape(-1, D_MODEL)], axis=0)
        f = moe(tokens, moe_w_router[i], moe_b_router[i], moe_w_gate[i], moe_w_up[i], moe_w_down[i],
                sh_w_gate[i], sh_w_up[i], sh_w_down[i])
        x = layer_norm(ALPHA * x + g2 * f[:n_x].reshape(x.shape), ln2_g[i], ln2_b[i])
        if ctx_later:
            ctx = layer_norm(ALPHA * ctx + cg2 * f[n_x:].reshape(ctx.shape), ln2_g[i], ln2_b[i])
    return x
```

```python
import functools

import jax
import jax.numpy as jnp
from jax import lax
from jax.experimental import pallas as pl
from jax.experimental.pallas import tpu as pltpu

F32 = jnp.float32
BF16 = jnp.bfloat16

RG_C = 8.0
N_GROUPS = 8
TOPK_GROUPS = 4
TOP_K = 8
ROUTED_SCALE = 2.5
LN_EPS = 1e-5

VMEM_LIMIT_BYTES = 56 * 1024 * 1024
SUBLANES = 8
LANES = 128
BF16_ROWS = 16

TM_PROJ = 512
TM_SCAN = 512
TM_CONV = 256
TM_ROUTE = 512
TM_DISPATCH = 256
TM_COMBINE = 128
MOE_BLOCK = 512


def _tile(default, n):
    t = min(default, n)
    assert n % t == 0, (default, n)
    return t


def _cparams(n_axes):
    return pltpu.CompilerParams(
        dimension_semantics=("arbitrary",) * n_axes,
        vmem_limit_bytes=VMEM_LIMIT_BYTES,
    )


def _sigmoid(v):
    return 1.0 / (1.0 + jnp.exp(-v))


def _silu(v):
    return v * _sigmoid(v)


def _gelu_tanh(v):
    c = 0.7978845608028654
    return 0.5 * v * (1.0 + jnp.tanh(c * (v + 0.044715 * (v * v * v))))


def _softplus(v):
    return jnp.maximum(v, 0.0) + jnp.log(1.0 + jnp.exp(-jnp.abs(v)))


def _layer_norm(v, g, b):
    mu = jnp.mean(v, axis=-1, keepdims=True)
    d = v - mu
    var = jnp.mean(d * d, axis=-1, keepdims=True)
    return d * lax.rsqrt(var + LN_EPS) * g + b


def _bdot(a, b):
    return jnp.dot(a, b, preferred_element_type=F32)


def _mod_kernel(c_ref, w_ref, b_ref, o_ref):
    s = _silu(c_ref[...])
    o_ref[0] = _bdot(s.astype(BF16), w_ref[0].astype(BF16)) + b_ref[0]


def _adaln_vectors(c_rows, w_mod, b_mod):
    depth, d, d6 = w_mod.shape
    tn = _tile(1024, d6)
    return pl.pallas_call(
        _mod_kernel,
        grid=(depth, d6 // tn),
        in_specs=[
            pl.BlockSpec((SUBLANES, d), lambda l, n: (0, 0)),
            pl.BlockSpec((1, d, tn), lambda l, n: (l, 0, n)),
            pl.BlockSpec((1, 1, tn), lambda l, n: (l, 0, n)),
        ],
        out_specs=pl.BlockSpec((1, SUBLANES, tn), lambda l, n: (l, 0, n)),
        out_shape=jax.ShapeDtypeStruct((depth, SUBLANES, d6), F32),
        compiler_params=_cparams(2),
        name="adaln_vectors",
    )(c_rows, w_mod, b_mod.reshape(depth, 1, d6))


def _mod_spec(chunk, batch_of):
    def index(*ids):
        return (chunk, batch_of(*ids), 0, 0)
    return index


def _proj_kernel(x_ref, sh_ref, sc_ref, w_ref, b_ref, o_ref):
    h = x_ref[0] * (1.0 + sc_ref[0, 0]) + sh_ref[0, 0]
    o_ref[0] = (_bdot(h.astype(BF16), w_ref[...]) + b_ref[...]).astype(o_ref.dtype)


def _modulated_projection(x, mod, w_bf16, bias, out_dtype):
    nb, s, d = x.shape
    n = w_bf16.shape[1]
    tm = _tile(TM_PROJ, s)
    tn = _tile(1280, n)
    dm = mod.shape[-1]
    return pl.pallas_call(
        _proj_kernel,
        grid=(n // tn, nb, s // tm),
        in_specs=[
            pl.BlockSpec((1, tm, d), lambda j, b, i: (b, i, 0)),
            pl.BlockSpec((1, 1, 1, dm), lambda j, b, i: (0, b, 0, 0)),
            pl.BlockSpec((1, 1, 1, dm), lambda j, b, i: (1, b, 0, 0)),
            pl.BlockSpec((d, tn), lambda j, b, i: (0, j)),
            pl.BlockSpec((1, tn), lambda j, b, i: (0, j)),
        ],
        out_specs=pl.BlockSpec((1, tm, tn), lambda j, b, i: (b, i, j)),
        out_shape=jax.ShapeDtypeStruct((nb, s, n), out_dtype),
        compiler_params=_cparams(3),
        name="modulated_projection",
    )(x, mod, mod, w_bf16, bias.reshape(1, n))


def _rglru_kernel(*refs, reverse, readout, n_t, tm, alpha):
    if readout:
        (cur_ref, prev_ref, next_ref, cw_ref, cb_ref, wg_ref, bg_ref, lam_ref, h0_ref,
         hf_ref, gate_ref, wout_ref, bout_ref, x_ref, g1_ref, lng_ref, lnb_ref,
         xo_ref, hlast_ref,
         win_ref, a_ref, hl_ref, carry_ref, p_ref, acc_ref) = refs
    else:
        (cur_ref, prev_ref, next_ref, cw_ref, cb_ref, wg_ref, bg_ref, lam_ref, h0_ref,
         ho_ref, hlast_ref,
         win_ref, a_ref, hl_ref, carry_ref) = refs
    i = pl.program_id(1)
    h = pl.program_id(2)
    n_h = pl.num_programs(2)
    ti = (n_t - 1 - i) if reverse else i
    hd = cur_ref.shape[-1]
    chunk = tm // SUBLANES

    prev = prev_ref[0].astype(F32)[BF16_ROWS - SUBLANES:, :]
    nxt = next_ref[0].astype(F32)[:SUBLANES, :]
    win_ref[0:SUBLANES, :] = jnp.where(ti == 0, 0.0, prev)
    win_ref[SUBLANES:SUBLANES + tm, :] = cur_ref[0].astype(F32)
    win_ref[SUBLANES + tm:, :] = jnp.where(ti == n_t - 1, 0.0, nxt)
    n_taps = cw_ref.shape[0]
    xb = cb_ref[...] + cw_ref[0:1, :] * win_ref[pl.ds(SUBLANES - 1, tm), :]
    for k in range(1, n_taps):
        xb = xb + cw_ref[k:k + 1, :] * win_ref[pl.ds(SUBLANES - 1 + k, tm), :]

    pre = _bdot(xb.astype(BF16), wg_ref[0]) + bg_ref[0]
    r = _sigmoid(pre[:, :hd])
    gi = _sigmoid(pre[:, hd:])
    log_a = (-RG_C) * r * _softplus(-lam_ref[...])
    a = jnp.exp(log_a)
    u = jnp.sqrt(1.0 - jnp.exp(2.0 * log_a)) * gi * xb
    n_l = hd // LANES
    for c in range(n_l):
        a_ref[c] = a[:, c * LANES:(c + 1) * LANES]
        hl_ref[c] = u[:, c * LANES:(c + 1) * LANES]

    def scan_step(jj, carry):
        j = (chunk - 1 - jj) if reverse else jj
        rows = pl.ds(j, SUBLANES, stride=chunk)
        out = []
        for c in range(n_l):
            a_cum, h_loc = carry[c]
            aj = a_ref[c, rows, :]
            h_loc = aj * h_loc + hl_ref[c, rows, :]
            a_cum = a_cum * aj
            hl_ref[c, rows, :] = h_loc
            a_ref[c, rows, :] = a_cum
            out.append((a_cum, h_loc))
        return tuple(out)

    init = tuple((jnp.ones((SUBLANES, LANES), F32), jnp.zeros((SUBLANES, LANES), F32)) for _ in range(n_l))
    ends = lax.fori_loop(0, chunk, scan_step, init)
    a_end = jnp.concatenate([e[0] for e in ends], axis=1)
    h_end = jnp.concatenate([e[1] for e in ends], axis=1)

    @pl.when(i == 0)
    def _():
        carry_ref[h] = h0_ref[0, h]

    state = carry_ref[h]
    entry = [None] * SUBLANES
    for s in (range(SUBLANES - 1, -1, -1) if reverse else range(SUBLANES)):
        entry[s] = state
        state = a_end[s:s + 1, :] * state + h_end[s:s + 1, :]
    carry_ref[h] = state
    hlast_ref[0, h] = state

    for s in range(SUBLANES):
        rows = pl.ds(s * chunk, chunk)
        h_full = jnp.concatenate([hl_ref[c, rows, :] + a_ref[c, rows, :] * entry[s][:, c * LANES:(c + 1) * LANES]
                                  for c in range(n_l)], axis=1)
        if readout:
            h_sum = h_full + hf_ref[0, rows, :].astype(F32)
            p_ref[rows, :] = (h_sum * _gelu_tanh(gate_ref[0, rows, :].astype(F32))).astype(BF16)
        else:
            ho_ref[0, rows, :] = h_full.astype(ho_ref.dtype)

    if readout:
        contrib = _bdot(p_ref[...], wout_ref[...])

        @pl.when(h == 0)
        def _():
            acc_ref[...] = contrib

        @pl.when(h != 0)
        def _():
            acc_ref[...] += contrib

        @pl.when(h == n_h - 1)
        def _():
            y = acc_ref[...] + bout_ref[...]
            v = alpha * x_ref[0] + g1_ref[0, 0] * y
            xo_ref[0] = _layer_norm(v, lng_ref[...], lnb_ref[...])


def _rglru_scan(proj, conv_w, conv_b, wg, bg, lam, h0, *, reverse, h_dtype=BF16, readout=None, alpha=None):
    nb, s, r2 = proj.shape
    r = r2 // 2
    n_h, hd = wg.shape[0], wg.shape[1]
    tm = _tile(TM_SCAN, s)
    n_t = s // tm
    hb = tm // BF16_ROWS
    n_hb = s // BF16_ROWS

    def tix(i):
        return (n_t - 1 - i) if reverse else i

    in_specs = [
        pl.BlockSpec((1, tm, hd), lambda b, i, h: (b, tix(i), n_h + h)),
        pl.BlockSpec((1, BF16_ROWS, hd), lambda b, i, h: (b, jnp.maximum(tix(i) * hb - 1, 0), n_h + h)),
        pl.BlockSpec((1, BF16_ROWS, hd), lambda b, i, h: (b, jnp.minimum((tix(i) + 1) * hb, n_hb - 1), n_h + h)),
        pl.BlockSpec((conv_w.shape[0], hd), lambda b, i, h: (0, h)),
        pl.BlockSpec((1, hd), lambda b, i, h: (0, h)),
        pl.BlockSpec((1, hd, 2 * hd), lambda b, i, h: (h, 0, 0)),
        pl.BlockSpec((1, 1, 2 * hd), lambda b, i, h: (h, 0, 0)),
        pl.BlockSpec((1, hd), lambda b, i, h: (0, h)),
        pl.BlockSpec((1, n_h, 1, hd), lambda b, i, h: (b, 0, 0, 0)),
    ]
    args = [proj, proj, proj, conv_w, conv_b.reshape(1, r), wg, bg, lam.reshape(1, r), h0]
    scratch = [
        pltpu.VMEM((tm + 2 * SUBLANES, hd), F32),
        pltpu.VMEM((hd // LANES, tm, LANES), F32),
        pltpu.VMEM((hd // LANES, tm, LANES), F32),
        pltpu.VMEM((n_h, 1, hd), F32),
    ]
    hlast_spec = pl.BlockSpec((1, n_h, 1, hd), lambda b, i, h: (b, 0, 0, 0))
    hlast_shape = jax.ShapeDtypeStruct((nb, n_h, 1, hd), F32)
    if readout is None:
        out_specs = [pl.BlockSpec((1, tm, hd), lambda b, i, h: (b, tix(i), h)), hlast_spec]
        out_shape = [jax.ShapeDtypeStruct((nb, s, r), h_dtype), hlast_shape]
    else:
        h_other, w_out, b_out, x, mod, ln_g, ln_b = readout
        d = x.shape[-1]
        in_specs += [
            pl.BlockSpec((1, tm, hd), lambda b, i, h: (b, tix(i), h)),
            pl.BlockSpec((1, tm, hd), lambda b, i, h: (b, tix(i), h)),
            pl.BlockSpec((hd, d), lambda b, i, h: (h, 0)),
            pl.BlockSpec((1, d), lambda b, i, h: (0, 0)),
            pl.BlockSpec((1, tm, d), lambda b, i, h: (b, tix(i), 0)),
            pl.BlockSpec((1, 1, 1, d), lambda b, i, h: (2, b, 0, 0)),
            pl.BlockSpec((1, d), lambda b, i, h: (0, 0)),
            pl.BlockSpec((1, d), lambda b, i, h: (0, 0)),
        ]
        args += [h_other, proj, w_out, b_out.reshape(1, d), x, mod, ln_g.reshape(1, d), ln_b.reshape(1, d)]
        scratch += [pltpu.VMEM((tm, hd), BF16), pltpu.VMEM((tm, d), F32)]
        out_specs = [pl.BlockSpec((1, tm, d), lambda b, i, h: (b, tix(i), 0)), hlast_spec]
        out_shape = [jax.ShapeDtypeStruct((nb, s, d), F32), hlast_shape]
    kern = functools.partial(_rglru_kernel, reverse=reverse, readout=readout is not None,
                             n_t=n_t, tm=tm, alpha=alpha)
    return pl.pallas_call(
        kern,
        grid=(nb, n_t, n_h),
        in_specs=in_specs,
        out_specs=out_specs,
        out_shape=out_shape,
        scratch_shapes=scratch,
        compiler_params=_cparams(3),
        name="rglru_bwd_readout" if readout is not None else ("rglru_bwd" if reverse else "rglru_fwd"),
    )(*args)


def _glu_kernel(x_ref, sh_ref, sc_ref, wv_ref, wg_ref, bv_ref, bgate_ref, o_ref):
    h = (x_ref[0] * (1.0 + sc_ref[0, 0]) + sh_ref[0, 0]).astype(BF16)
    val = _bdot(h, wv_ref[...]) + bv_ref[...]
    gate = _bdot(h, wg_ref[...]) + bgate_ref[...]
    o_ref[0] = (val * _sigmoid(gate)).astype(o_ref.dtype)


def _modulated_glu(x, mod, w_bf16, bias):
    nb, s, d = x.shape
    n = w_bf16.shape[1] // 2
    tm = _tile(TM_PROJ, s)
    tn = _tile(1024, n)
    nj = n // tn
    b2 = bias.reshape(1, 2 * n)
    return pl.pallas_call(
        _glu_kernel,
        grid=(nj, nb, s // tm),
        in_specs=[
            pl.BlockSpec((1, tm, d), lambda j, b, i: (b, i, 0)),
            pl.BlockSpec((1, 1, 1, d), lambda j, b, i: (0, b, 0, 0)),
            pl.BlockSpec((1, 1, 1, d), lambda j, b, i: (1, b, 0, 0)),
            pl.BlockSpec((d, tn), lambda j, b, i: (0, j)),
            pl.BlockSpec((d, tn), lambda j, b, i: (0, nj + j)),
            pl.BlockSpec((1, tn), lambda j, b, i: (0, j)),
            pl.BlockSpec((1, tn), lambda j, b, i: (0, nj + j)),
        ],
        out_specs=pl.BlockSpec((1, tm, tn), lambda j, b, i: (b, i, j)),
        out_shape=jax.ShapeDtypeStruct((nb, s, n), BF16),
        compiler_params=_cparams(3),
        name="modulated_glu",
    )(x, mod, mod, w_bf16, w_bf16, b2, b2)


def _conformer_tail_kernel(cur_ref, prev_ref, next_ref, wdw_ref, bdw_ref, ng_ref, nb_ref, w2_ref, b2_ref,
                           x_ref, g1_ref, lng_ref, lnb_ref, xo_ref, win_ref, z_ref, shift_ref, *, n_t, tm, alpha,
                           lane_chunk, row_chunk):
    i = pl.program_id(1)
    d = cur_ref.shape[-1]
    halo = BF16_ROWS
    win_ref[0:halo, :] = jnp.where(i == 0, 0.0, prev_ref[0].astype(F32))
    win_ref[halo:halo + tm, :] = cur_ref[0].astype(F32)
    win_ref[halo + tm:, :] = jnp.where(i == n_t - 1, 0.0, next_ref[0].astype(F32))
    n_taps = wdw_ref.shape[0]
    base = halo - (n_taps - 1) // 2

    n_shift = shift_ref.shape[1]
    for c in range(d // lane_chunk):
        lanes = slice(c * lane_chunk, (c + 1) * lane_chunk)
        for s in range(SUBLANES):
            shift_ref[s] = win_ref[pl.ds(s, n_shift), lanes]

        def rows_step(rc, _, lanes=lanes):
            r0 = pl.multiple_of(rc * row_chunk, row_chunk)
            acc = jnp.zeros((row_chunk, lane_chunk), F32) + bdw_ref[:, lanes]
            for k in range(n_taps):
                q, s = divmod(base + k, SUBLANES)
                rows = pl.ds(pl.multiple_of(r0 + q * SUBLANES, SUBLANES), row_chunk)
                acc = acc + wdw_ref[k:k + 1, lanes] * shift_ref[s, rows, :]
            z_ref[pl.ds(r0, row_chunk), lanes] = acc
            return 0

        lax.fori_loop(0, tm // row_chunk, rows_step, 0)

    z = _layer_norm(z_ref[...], ng_ref[...], nb_ref[...])
    y = _bdot(_silu(z).astype(BF16), w2_ref[...]) + b2_ref[...]
    v = alpha * x_ref[0] + g1_ref[0, 0] * y
    xo_ref[0] = _layer_norm(v, lng_ref[...], lnb_ref[...])


def _conformer_tail(z, w_dw, b_dw, n_g, n_b, w2_bf16, b2, x, mod, ln_g, ln_b, alpha):
    nb, s, d = z.shape
    tm = _tile(TM_CONV, s)
    n_t = s // tm
    hb = tm // BF16_ROWS
    n_hb = s // BF16_ROWS
    lane_chunk = min(256, d)
    n_taps = w_dw.shape[0]
    last_tap_row = BF16_ROWS + (n_taps - 1) // 2
    n_shift = tm + last_tap_row // SUBLANES * SUBLANES
    assert n_shift + SUBLANES - 1 <= tm + 2 * BF16_ROWS and (n_taps - 1) // 2 <= BF16_ROWS
    kern = functools.partial(_conformer_tail_kernel, n_t=n_t, tm=tm, alpha=alpha,
                             lane_chunk=lane_chunk, row_chunk=min(32, tm))
    vec = lambda a: a.reshape(1, d)
    vspec = pl.BlockSpec((1, d), lambda b, i: (0, 0))
    return pl.pallas_call(
        kern,
        grid=(nb, n_t),
        in_specs=[
            pl.BlockSpec((1, tm, d), lambda b, i: (b, i, 0)),
            pl.BlockSpec((1, BF16_ROWS, d), lambda b, i: (b, jnp.maximum(i * hb - 1, 0), 0)),
            pl.BlockSpec((1, BF16_ROWS, d), lambda b, i: (b, jnp.minimum((i + 1) * hb, n_hb - 1), 0)),
            pl.BlockSpec((w_dw.shape[0], d), lambda b, i: (0, 0)),
            vspec, vspec, vspec,
            pl.BlockSpec((d, d), lambda b, i: (0, 0)),
            vspec,
            pl.BlockSpec((1, tm, d), lambda b, i: (b, i, 0)),
            pl.BlockSpec((1, 1, 1, d), lambda b, i: (2, b, 0, 0)),
            vspec, vspec,
        ],
        out_specs=pl.BlockSpec((1, tm, d), lambda b, i: (b, i, 0)),
        out_shape=jax.ShapeDtypeStruct((nb, s, d), F32),
        scratch_shapes=[pltpu.VMEM((tm + 2 * BF16_ROWS, d), F32), pltpu.VMEM((tm, d), F32),
                        pltpu.VMEM((SUBLANES, n_shift, lane_chunk), F32)],
        compiler_params=_cparams(2),
        name="conformer_tail",
    )(z, z, z, w_dw, vec(b_dw), vec(n_g), vec(n_b), w2_bf16, vec(b2), x, mod, vec(ln_g), vec(ln_b))


def _router_kernel(x_ref, sh_ref, sc_ref, whi_ref, wlo_ref, bc_ref,
                   idx_ref, wt_ref, rank_ref, cnt_ref, carry_ref, *, n_exp, tm):
    i = pl.program_id(0)

    @pl.when(i == 0)
    def _():
        carry_ref[...] = jnp.zeros_like(carry_ref)

    tok = x_ref[...] * (1.0 + sc_ref[0, 0]) + sh_ref[0, 0]
    t_hi = tok.astype(BF16)
    t_lo = (tok - t_hi.astype(F32)).astype(BF16)
    nt = (((1,), (1,)), ((), ()))
    logits = (lax.dot_general(whi_ref[...], t_hi, nt, preferred_element_type=F32)
              + lax.dot_general(whi_ref[...], t_lo, nt, preferred_element_type=F32)
              + lax.dot_general(wlo_ref[...], t_hi, nt, preferred_element_type=F32))
    scores = _sigmoid(logits)
    biased = scores + bc_ref[...]

    per_group = n_exp // N_GROUPS
    sub = lax.broadcasted_iota(jnp.int32, (per_group, tm), 0)
    neg = -jnp.inf
    grp = [biased[g * per_group:(g + 1) * per_group, :] for g in range(N_GROUPS)]
    sc_g = [scores[g * per_group:(g + 1) * per_group, :] for g in range(N_GROUPS)]

    gscore = []
    for g in range(N_GROUPS):
        m1 = jnp.max(grp[g], axis=0, keepdims=True)
        first = jnp.min(jnp.where(grp[g] == m1, sub, per_group), axis=0, keepdims=True)
        m2 = jnp.max(jnp.where(sub == first, neg, grp[g]), axis=0, keepdims=True)
        gscore.append(m1 + m2)
    masked = []
    for g in range(N_GROUPS):
        beaten = jnp.zeros((1, tm), jnp.int32)
        for o in range(N_GROUPS):
            if o == g:
                continue
            wins = (gscore[o] > gscore[g]) | ((gscore[o] == gscore[g]) & (o < g))
            beaten = beaten + wins.astype(jnp.int32)
        masked.append(jnp.where(beaten < TOPK_GROUPS, grp[g], neg))

    eid = [sub + g * per_group for g in range(N_GROUPS)]
    member = [jnp.zeros((per_group, tm), F32) for _ in range(N_GROUPS)]
    idx_rows, score_rows = [], []
    for _ in range(TOP_K):
        m = masked[0]
        for g in range(1, N_GROUPS):
            m = jnp.maximum(m, masked[g])
        m = jnp.max(m, axis=0, keepdims=True)
        cand = jnp.where(masked[0] == m, eid[0], n_exp)
        for g in range(1, N_GROUPS):
            cand = jnp.minimum(cand, jnp.where(masked[g] == m, eid[g], n_exp))
        first = jnp.min(cand, axis=0, keepdims=True)
        picked = jnp.zeros((per_group, tm), F32)
        for g in range(N_GROUPS):
            sel = eid[g] == first
            picked = picked + jnp.where(sel, sc_g[g], 0.0)
            member[g] = jnp.where(sel, 1.0, member[g])
            masked[g] = jnp.where(sel, neg, masked[g])
        idx_rows.append(first)
        score_rows.append(jnp.sum(picked, axis=0, keepdims=True))

    total = score_rows[0]
    for k in range(1, TOP_K):
        total = total + score_rows[k]

    memb = jnp.concatenate(member, axis=0)
    earlier = jnp.where(lax.broadcasted_iota(jnp.int32, (tm, tm), 0)
                        < lax.broadcasted_iota(jnp.int32, (tm, tm), 1), 1.0, 0.0).astype(BF16)
    pos = _bdot(memb.astype(BF16), earlier) + carry_ref[...]
    pos_g = [pos[g * per_group:(g + 1) * per_group, :] for g in range(N_GROUPS)]
    for k in range(TOP_K):
        acc = jnp.zeros((per_group, tm), F32)
        for g in range(N_GROUPS):
            acc = acc + jnp.where(eid[g] == idx_rows[k], pos_g[g], 0.0)
        rank_ref[k:k + 1, :] = jnp.sum(acc, axis=0, keepdims=True).astype(jnp.int32)
        idx_ref[k:k + 1, :] = idx_rows[k]
        wt_ref[k:k + 1, :] = score_rows[k] / total * ROUTED_SCALE
    carry_ref[...] += jnp.sum(memb, axis=1, keepdims=True)
    cnt_ref[...] = carry_ref[...].astype(jnp.int32)


def _route(x, mod, w_router, b_corr):
    nb, s, d = x.shape
    t = nb * s
    n_exp = w_router.shape[1]
    tm = _tile(TM_ROUTE, s)
    per_b = s // tm
    w_t = w_router.T
    w_hi = w_t.astype(BF16)
    w_lo = (w_t - w_hi.astype(F32)).astype(BF16)
    kern = functools.partial(_router_kernel, n_exp=n_exp, tm=tm)
    out_spec = pl.BlockSpec((TOP_K, tm), lambda i: (0, i))
    return pl.pallas_call(
        kern,
        grid=(t // tm,),
        in_specs=[
            pl.BlockSpec((tm, d), lambda i: (i, 0)),
            pl.BlockSpec((1, 1, 1, d), lambda i: (3, i // per_b, 0, 0)),
            pl.BlockSpec((1, 1, 1, d), lambda i: (4, i // per_b, 0, 0)),
            pl.BlockSpec((n_exp, d), lambda i: (0, 0)),
            pl.BlockSpec((n_exp, d), lambda i: (0, 0)),
            pl.BlockSpec((n_exp, 1), lambda i: (0, 0)),
        ],
        out_specs=[out_spec, out_spec, out_spec, pl.BlockSpec((n_exp, 1), lambda i: (0, 0))],
        out_shape=[
            jax.ShapeDtypeStruct((TOP_K, t), jnp.int32),
            jax.ShapeDtypeStruct((TOP_K, t), F32),
            jax.ShapeDtypeStruct((TOP_K, t), jnp.int32),
            jax.ShapeDtypeStruct((n_exp, 1), jnp.int32),
        ],
        scratch_shapes=[pltpu.VMEM((n_exp, 1), F32)],
        compiler_params=_cparams(1),
        name="moe_router",
    )(x.reshape(t, d), mod, mod, w_hi, w_lo, b_corr.reshape(n_exp, 1).astype(F32))


def _slab_rows(row, n_seg):
    return pl.ds(pl.multiple_of(row * n_seg, n_seg), n_seg)


def _to_slabs(ref, value, n_seg):
    m = value.shape[0]
    for s in range(n_seg):
        ref[pl.ds(s, m, stride=n_seg), :] = value[:, s * LANES:(s + 1) * LANES]


def _from_slabs(ref, m, n_seg):
    return jnp.concatenate([ref[pl.ds(s, m, stride=n_seg), :] for s in range(n_seg)], axis=1)


def _dispatch_kernel(pend_ref, padded_ref, x_ref, sh_ref, sc_ref, dest_ref, xs_ref,
                     tok_ref, zero_ref, sem, zsem, *, n_exp, tm, bm, n_seg):
    i = pl.program_id(0)

    def zero_copy(e):
        start = pl.multiple_of((pend_ref[e] - bm) * n_seg, n_seg)
        return pltpu.make_async_copy(zero_ref, xs_ref.at[pl.ds(start, bm * n_seg)], zsem)

    @pl.when(i == 0)
    def _():
        zero_ref[...] = jnp.zeros_like(zero_ref)

        def start(e, _):
            @pl.when(padded_ref[e] > 0)
            def _():
                zero_copy(e).start()
            return 0

        def wait(e, _):
            @pl.when(padded_ref[e] > 0)
            def _():
                zero_copy(e).wait()
            return 0

        lax.fori_loop(0, n_exp, start, 0)
        lax.fori_loop(0, n_exp, wait, 0)

    _to_slabs(tok_ref, x_ref[...] * (1.0 + sc_ref[0, 0]) + sh_ref[0, 0], n_seg)

    def row_copy(r, k):
        return pltpu.make_async_copy(tok_ref.at[_slab_rows(r, n_seg)],
                                     xs_ref.at[_slab_rows(dest_ref[k, r], n_seg)], sem)

    def issue(r, _):
        for k in range(TOP_K):
            row_copy(r, k).start()
        return 0

    def drain(r, _):
        for k in range(TOP_K):
            row_copy(r, k).wait()
        return 0

    lax.fori_loop(0, tm, issue, 0)
    lax.fori_loop(0, tm, drain, 0)


def _dispatch(x, mod, dest, pad_ends, padded, n_slots, bm):
    nb, s, d = x.shape
    t = nb * s
    tm = _tile(TM_DISPATCH, s)
    per_b = s // tm
    n_exp = pad_ends.shape[0]
    n_seg = d // LANES
    assert n_seg % SUBLANES == 0, "token slabs must be whole (8, 128) tiles"
    kern = functools.partial(_dispatch_kernel, n_exp=n_exp, tm=tm, bm=bm, n_seg=n_seg)
    grid_spec = pltpu.PrefetchScalarGridSpec(
        num_scalar_prefetch=2,
        grid=(t // tm,),
        in_specs=[
            pl.BlockSpec((tm, d), lambda i, pe, pd: (i, 0)),
            pl.BlockSpec((1, 1, 1, d), lambda i, pe, pd: (3, i // per_b, 0, 0)),
            pl.BlockSpec((1, 1, 1, d), lambda i, pe, pd: (4, i // per_b, 0, 0)),
            pl.BlockSpec((TOP_K, tm), lambda i, pe, pd: (0, i), memory_space=pltpu.SMEM),
        ],
        out_specs=pl.BlockSpec(memory_space=pl.ANY),
        scratch_shapes=[
            pltpu.VMEM((tm * n_seg, LANES), F32),
            pltpu.VMEM((bm * n_seg, LANES), F32),
            pltpu.SemaphoreType.DMA(()),
            pltpu.SemaphoreType.DMA(()),
        ],
    )
    return pl.pallas_call(
        kern,
        grid_spec=grid_spec,
        out_shape=jax.ShapeDtypeStruct((n_slots * n_seg, LANES), F32),
        compiler_params=_cparams(1),
        name="moe_dispatch",
    )(pad_ends, padded, x.reshape(t, d), mod, mod, dest)


def _experts_kernel(be_ref, nu_ref, xs_ref, wg_ref, wu_ref, wd_ref, ys_ref, *, bm, n_seg):
    b = pl.program_id(0)

    @pl.when(b < nu_ref[0])
    def _():
        x = _from_slabs(xs_ref, bm, n_seg).astype(BF16)
        g = _bdot(x, wg_ref[0])
        u = _bdot(x, wu_ref[0])
        _to_slabs(ys_ref, _bdot((_silu(g) * u).astype(BF16), wd_ref[0]), n_seg)

    @pl.when(b >= nu_ref[0])
    def _():
        ys_ref[...] = jnp.zeros_like(ys_ref)


def _grouped_experts(xs, blk_e, n_used, w_gate, w_up, w_down, bm):
    d, f = w_gate.shape[1:]
    n_seg = d // LANES
    n_blk = xs.shape[0] // (bm * n_seg)

    def blk(b, be, nu):
        return jnp.minimum(b, nu[0] - 1)

    grid_spec = pltpu.PrefetchScalarGridSpec(
        num_scalar_prefetch=2,
        grid=(n_blk,),
        in_specs=[
            pl.BlockSpec((bm * n_seg, LANES), lambda b, be, nu: (blk(b, be, nu), 0)),
            pl.BlockSpec((1, d, f), lambda b, be, nu: (be[blk(b, be, nu)], 0, 0)),
            pl.BlockSpec((1, d, f), lambda b, be, nu: (be[blk(b, be, nu)], 0, 0)),
            pl.BlockSpec((1, f, d), lambda b, be, nu: (be[blk(b, be, nu)], 0, 0)),
        ],
        out_specs=pl.BlockSpec((bm * n_seg, LANES), lambda b, be, nu: (jnp.where(b < nu[0], b, n_blk - 1), 0)),
    )
    return pl.pallas_call(
        functools.partial(_experts_kernel, bm=bm, n_seg=n_seg),
        grid_spec=grid_spec,
        out_shape=jax.ShapeDtypeStruct(xs.shape, F32),
        compiler_params=_cparams(1),
        name="moe_experts",
    )(blk_e, n_used, xs, w_gate, w_up, w_down)


def _combine_kernel(x_ref, sh_ref, sc_ref, gt_ref, dest_ref, wt_ref, ys_ref,
                    wsg_ref, wsu_ref, wsd_ref, lng_ref, lnb_ref, xo_ref, rows_ref, sem, *, tm, alpha, n_seg):
    def row_copy(r, k):
        return pltpu.make_async_copy(ys_ref.at[_slab_rows(dest_ref[k, r], n_seg)],
                                     rows_ref.at[k, _slab_rows(r, n_seg)], sem)

    def issue(r, _):
        for k in range(TOP_K):
            row_copy(r, k).start()
        return 0

    def drain(r, _):
        for k in range(TOP_K):
            row_copy(r, k).wait()
        return 0

    lax.fori_loop(0, tm, issue, 0)

    x = x_ref[...]
    tok = (x * (1.0 + sc_ref[0, 0]) + sh_ref[0, 0]).astype(BF16)
    hid = (_silu(_bdot(tok, wsg_ref[...])) * _bdot(tok, wsu_ref[...])).astype(BF16)
    f = _bdot(hid, wsd_ref[...])

    lax.fori_loop(0, tm, drain, 0)
    wt = wt_ref[...]
    for k in range(TOP_K):
        f = f + wt[:, k:k + 1] * _from_slabs(rows_ref.at[k], tm, n_seg)
    v = alpha * x + gt_ref[0, 0] * f
    xo_ref[...] = _layer_norm(v, lng_ref[...], lnb_ref[...])


def _combine(x, mod, dest, wt_rows, ys, ws_gate, ws_up, ws_down, ln_g, ln_b, alpha):
    nb, s, d = x.shape
    t = nb * s
    tm = _tile(TM_COMBINE, s)
    per_b = s // tm
    f = ws_gate.shape[-1]
    n_seg = d // LANES
    kern = functools.partial(_combine_kernel, tm=tm, alpha=alpha, n_seg=n_seg)
    vspec = pl.BlockSpec((1, d), lambda i: (0, 0))
    out = pl.pallas_call(
        kern,
        grid=(t // tm,),
        in_specs=[
            pl.BlockSpec((tm, d), lambda i: (i, 0)),
            pl.BlockSpec((1, 1, 1, d), lambda i: (3, i // per_b, 0, 0)),
            pl.BlockSpec((1, 1, 1, d), lambda i: (4, i // per_b, 0, 0)),
            pl.BlockSpec((1, 1, 1, d), lambda i: (5, i // per_b, 0, 0)),
            pl.BlockSpec((TOP_K, tm), lambda i: (0, i), memory_space=pltpu.SMEM),
            pl.BlockSpec((tm, TOP_K), lambda i: (i, 0)),
            pl.BlockSpec(memory_space=pl.ANY),
            pl.BlockSpec((d, f), lambda i: (0, 0)),
            pl.BlockSpec((d, f), lambda i: (0, 0)),
            pl.BlockSpec((f, d), lambda i: (0, 0)),
            vspec, vspec,
        ],
        out_specs=pl.BlockSpec((tm, d), lambda i: (i, 0)),
        out_shape=jax.ShapeDtypeStruct((t, d), F32),
        scratch_shapes=[pltpu.VMEM((TOP_K, tm * n_seg, LANES), F32), pltpu.SemaphoreType.DMA(())],
        compiler_params=_cparams(1),
        name="moe_combine",
    )(x.reshape(t, d), mod, mod, mod, dest, wt_rows, ys, ws_gate, ws_up, ws_down,
      ln_g.reshape(1, d), ln_b.reshape(1, d))
    return out.reshape(nb, s, d)


def _moe_block(x, mod, w_router, b_router, w_gate, w_up, w_down, ws_gate, ws_up, ws_down, ln_g, ln_b, alpha):
    nb, s, d = x.shape
    t = nb * s
    n_exp = w_router.shape[1]
    bm = _tile(MOE_BLOCK, t * TOP_K)
    idx, wt, rank, counts = _route(x, mod, w_router, b_router)

    counts = counts.reshape(n_exp)
    padded = (counts + bm - 1) // bm * bm
    pad_ends = jnp.cumsum(padded).astype(jnp.int32)
    pad_starts = pad_ends - padded
    dest = (jnp.take(pad_starts, idx, axis=0) + rank).astype(jnp.int32)
    n_blk = t * TOP_K // bm + n_exp
    n_used = (pad_ends[-1:] // bm).astype(jnp.int32)
    blk_start = jnp.arange(n_blk, dtype=jnp.int32) * bm
    blk_e = jnp.minimum(jnp.searchsorted(pad_ends, blk_start, side="right"), n_exp - 1).astype(jnp.int32)

    xs = _dispatch(x, mod, dest, pad_ends, padded.astype(jnp.int32), n_blk * bm, bm)
    ys = _grouped_experts(xs, blk_e, n_used, w_gate.astype(BF16), w_up.astype(BF16), w_down.astype(BF16), bm)
    return _combine(x, mod, dest, wt.T, ys, ws_gate.astype(BF16), ws_up.astype(BF16), ws_down.astype(BF16),
                    ln_g, ln_b, alpha)


def kernel(x, c, ctx, c_ctx, w_mod, b_mod, ln1_g, ln1_b, ln2_g, ln2_b, rg_w_in, rg_b_in, rg_conv_w, rg_conv_b, rg_w_gates, rg_b_gates, rg_lambda, rg_w_out, rg_b_out, cf_w_pw1, cf_b_pw1, cf_w_dw, cf_b_dw, cf_norm_g, cf_norm_b, cf_w_pw2, cf_b_pw2, moe_w_router, moe_b_router, moe_w_gate, moe_w_up, moe_w_down, sh_w_gate, sh_w_up, sh_w_down):
    nb, s, d = x.shape
    depth = w_mod.shape[0]
    assert depth == 2 and nb + 1 <= SUBLANES
    alpha = (2 * depth) ** 0.25

    c_rows = jnp.zeros((SUBLANES, d), F32).at[:nb].set(c).at[nb].set(c_ctx)
    mod_all = _adaln_vectors(c_rows, w_mod, b_mod)
    mod_all = mod_all.reshape(depth, SUBLANES, 6, 1, d).transpose(0, 2, 1, 3, 4)
    mod_lat = [mod_all[l, :, :nb] for l in range(depth)]
    mod_ctx = jnp.broadcast_to(mod_all[0, :, nb:nb + 1], (6, nb, 1, d))

    n_h, hd = rg_w_gates.shape[3], rg_w_gates.shape[4]
    r = n_h * hd
    w_in = rg_w_in[0].astype(BF16)
    wg = [jnp.concatenate([rg_w_gates[0, dr, 0], rg_w_gates[0, dr, 1]], axis=-1).astype(BF16) for dr in range(2)]
    bg = [jnp.concatenate([rg_b_gates[0, dr, 0], rg_b_gates[0, dr, 1]], axis=-1).reshape(n_h, 1, 2 * hd)
          for dr in range(2)]
    conv_w, conv_b = rg_conv_w[0], rg_conv_b[0]
    zero_state = jnp.zeros((nb, n_h, 1, hd), F32)

    proj_c = _modulated_projection(ctx, mod_ctx, w_in, rg_b_in[0], BF16)
    _, hcf = _rglru_scan(proj_c, conv_w, conv_b, wg[0], bg[0], rg_lambda[0, 0], zero_state, reverse=False)
    _, hcb = _rglru_scan(proj_c, conv_w, conv_b, wg[1], bg[1], rg_lambda[0, 1], zero_state, reverse=True)

    proj_x = _modulated_projection(x, mod_lat[0], w_in, rg_b_in[0], BF16)
    hxf, _ = _rglru_scan(proj_x, conv_w, conv_b, wg[0], bg[0], rg_lambda[0, 0], hcf, reverse=False)
    readout = (hxf, rg_w_out[0].astype(BF16), rg_b_out[0], x, mod_lat[0], ln1_g[0], ln1_b[0])
    x, _ = _rglru_scan(proj_x, conv_w, conv_b, wg[1], bg[1], rg_lambda[0, 1], hcb, reverse=True,
                       readout=readout, alpha=alpha)
    x = _moe_block(x, mod_lat[0], moe_w_router[0], moe_b_router[0], moe_w_gate[0], moe_w_up[0], moe_w_down[0],
                   sh_w_gate[0], sh_w_up[0], sh_w_down[0], ln2_g[0], ln2_b[0], alpha)

    z = _modulated_glu(x, mod_lat[1], cf_w_pw1[0].astype(BF16), cf_b_pw1[0])
    x = _conformer_tail(z, cf_w_dw[0], cf_b_dw[0], cf_norm_g[0], cf_norm_b[0], cf_w_pw2[0].astype(BF16),
                        cf_b_pw2[0], x, mod_lat[1], ln1_g[1], ln1_b[1], alpha)
    x = _moe_block(x, mod_lat[1], moe_w_router[1], moe_b_router[1], moe_w_gate[1], moe_w_up[1], moe_w_down[1],
                   sh_w_gate[1], sh_w_up[1], sh_w_down[1], ln2_g[1], ln2_b[1], alpha)
    return x
```

```python
import functools

import jax
import jax.numpy as jnp
from jax import lax
from jax.experimental import pallas as pl
from jax.experimental.pallas import tpu as pltpu

F32 = jnp.float32
BF16 = jnp.bfloat16

RG_C = 8.0
N_GROUPS = 8
TOPK_GROUPS = 4
TOP_K = 8
ROUTED_SCALE = 2.5
LN_EPS = 1e-5

VMEM_LIMIT_BYTES = 56 * 1024 * 1024
SUBLANES = 8
LANES = 128
BF16_ROWS = 16

TM_PROJ = 512
TM_SCAN = 512
TM_CONV = 256
TM_ROUTE = 512
TM_DISPATCH = 256
TM_COMBINE = 128
MOE_BLOCK = 512


def _tile(default, n):
    t = min(default, n)
    assert n % t == 0, (default, n)
    return t


def _cparams(n_axes):
    return pltpu.CompilerParams(
        dimension_semantics=("arbitrary",) * n_axes,
        vmem_limit_bytes=VMEM_LIMIT_BYTES,
    )


def _sigmoid(v):
    return 0.5 * jnp.tanh(0.5 * v) + 0.5


def _sqrt_nonneg(v):
    return jnp.where(v > 0.0, v * lax.rsqrt(v), 0.0)


def _silu(v):
    return v * _sigmoid(v)


def _gelu_tanh(v):
    c = 0.7978845608028654
    return 0.5 * v * (1.0 + jnp.tanh(c * (v + 0.044715 * (v * v * v))))


def _softplus(v):
    return jnp.maximum(v, 0.0) + jnp.log(1.0 + jnp.exp(-jnp.abs(v)))


def _layer_norm(v, g, b):
    mu = jnp.mean(v, axis=-1, keepdims=True)
    d = v - mu
    var = jnp.mean(d * d, axis=-1, keepdims=True)
    return d * lax.rsqrt(var + LN_EPS) * g + b


def _bdot(a, b):
    return jnp.dot(a, b, preferred_element_type=F32)


def _mod_kernel(c_ref, w_ref, b_ref, o_ref):
    s = _silu(c_ref[...])
    o_ref[0] = _bdot(s.astype(BF16), w_ref[0].astype(BF16)) + b_ref[0]


def _adaln_vectors(c_rows, w_mod, b_mod):
    depth, d, d6 = w_mod.shape
    tn = _tile(1024, d6)
    return pl.pallas_call(
        _mod_kernel,
        grid=(depth, d6 // tn),
        in_specs=[
            pl.BlockSpec((SUBLANES, d), lambda l, n: (0, 0)),
            pl.BlockSpec((1, d, tn), lambda l, n: (l, 0, n)),
            pl.BlockSpec((1, 1, tn), lambda l, n: (l, 0, n)),
        ],
        out_specs=pl.BlockSpec((1, SUBLANES, tn), lambda l, n: (l, 0, n)),
        out_shape=jax.ShapeDtypeStruct((depth, SUBLANES, d6), F32),
        compiler_params=_cparams(2),
        name="adaln_vectors",
    )(c_rows, w_mod, b_mod.reshape(depth, 1, d6))


def _mod_spec(chunk, batch_of):
    def index(*ids):
        return (chunk, batch_of(*ids), 0, 0)
    return index


def _proj_kernel(x_ref, sh_ref, sc_ref, w_ref, b_ref, o_ref):
    h = x_ref[0] * (1.0 + sc_ref[0, 0]) + sh_ref[0, 0]
    o_ref[0] = (_bdot(h.astype(BF16), w_ref[...]) + b_ref[...]).astype(o_ref.dtype)


def _modulated_projection(x, mod, w_bf16, bias, out_dtype):
    nb, s, d = x.shape
    n = w_bf16.shape[1]
    tm = _tile(TM_PROJ, s)
    tn = _tile(1280, n)
    dm = mod.shape[-1]
    return pl.pallas_call(
        _proj_kernel,
        grid=(n // tn, nb, s // tm),
        in_specs=[
            pl.BlockSpec((1, tm, d), lambda j, b, i: (b, i, 0)),
            pl.BlockSpec((1, 1, 1, dm), lambda j, b, i: (0, b, 0, 0)),
            pl.BlockSpec((1, 1, 1, dm), lambda j, b, i: (1, b, 0, 0)),
            pl.BlockSpec((d, tn), lambda j, b, i: (0, j)),
            pl.BlockSpec((1, tn), lambda j, b, i: (0, j)),
        ],
        out_specs=pl.BlockSpec((1, tm, tn), lambda j, b, i: (b, i, j)),
        out_shape=jax.ShapeDtypeStruct((nb, s, n), out_dtype),
        compiler_params=_cparams(3),
        name="modulated_projection",
    )(x, mod, mod, w_bf16, bias.reshape(1, n))


def _rglru_kernel(*refs, reverse, readout, n_t, tm, alpha):
    if readout:
        (cur_ref, prev_ref, next_ref, cw_ref, cb_ref, wg_ref, bg_ref, lam_ref, h0_ref,
         hf_ref, gate_ref, wout_ref, bout_ref, x_ref, g1_ref, lng_ref, lnb_ref,
         xo_ref, hlast_ref,
         win_ref, a_ref, hl_ref, carry_ref, p_ref, acc_ref) = refs
    else:
        (cur_ref, prev_ref, next_ref, cw_ref, cb_ref, wg_ref, bg_ref, lam_ref, h0_ref,
         ho_ref, hlast_ref,
         win_ref, a_ref, hl_ref, carry_ref) = refs
    i = pl.program_id(1)
    h = pl.program_id(2)
    n_h = pl.num_programs(2)
    ti = (n_t - 1 - i) if reverse else i
    hd = cur_ref.shape[-1]
    chunk = tm // SUBLANES

    prev = prev_ref[0].astype(F32)[BF16_ROWS - SUBLANES:, :]
    nxt = next_ref[0].astype(F32)[:SUBLANES, :]
    win_ref[0:SUBLANES, :] = jnp.where(ti == 0, 0.0, prev)
    win_ref[SUBLANES:SUBLANES + tm, :] = cur_ref[0].astype(F32)
    win_ref[SUBLANES + tm:, :] = jnp.where(ti == n_t - 1, 0.0, nxt)
    n_taps = cw_ref.shape[0]
    xb = cb_ref[...] + cw_ref[0:1, :] * win_ref[pl.ds(SUBLANES - 1, tm), :]
    for k in range(1, n_taps):
        xb = xb + cw_ref[k:k + 1, :] * win_ref[pl.ds(SUBLANES - 1 + k, tm), :]

    pre = _bdot(xb.astype(BF16), wg_ref[0]) + bg_ref[0]
    r = _sigmoid(pre[:, :hd])
    gi = _sigmoid(pre[:, hd:])
    log_a = (-RG_C) * r * _softplus(-lam_ref[...])
    a = jnp.exp(log_a)
    u = _sqrt_nonneg(1.0 - a * a) * gi * xb
    n_l = hd // LANES
    for c in range(n_l):
        a_ref[c] = a[:, c * LANES:(c + 1) * LANES]
        hl_ref[c] = u[:, c * LANES:(c + 1) * LANES]

    def scan_step(jj, carry):
        j = (chunk - 1 - jj) if reverse else jj
        rows = pl.ds(j, SUBLANES, stride=chunk)
        out = []
        for c in range(n_l):
            a_cum, h_loc = carry[c]
            aj = a_ref[c, rows, :]
            h_loc = aj * h_loc + hl_ref[c, rows, :]
            a_cum = a_cum * aj
            hl_ref[c, rows, :] = h_loc
            a_ref[c, rows, :] = a_cum
            out.append((a_cum, h_loc))
        return tuple(out)

    init = tuple((jnp.ones((SUBLANES, LANES), F32), jnp.zeros((SUBLANES, LANES), F32)) for _ in range(n_l))
    ends = lax.fori_loop(0, chunk, scan_step, init, unroll=SUBLANES)
    a_end = jnp.concatenate([e[0] for e in ends], axis=1)
    h_end = jnp.concatenate([e[1] for e in ends], axis=1)

    @pl.when(i == 0)
    def _():
        carry_ref[h] = h0_ref[0, h]

    state = carry_ref[h]
    entry = [None] * SUBLANES
    for s in (range(SUBLANES - 1, -1, -1) if reverse else range(SUBLANES)):
        entry[s] = state
        state = a_end[s:s + 1, :] * state + h_end[s:s + 1, :]
    carry_ref[h] = state
    hlast_ref[0, h] = state

    for s in range(SUBLANES):
        rows = pl.ds(s * chunk, chunk)
        h_full = jnp.concatenate([hl_ref[c, rows, :] + a_ref[c, rows, :] * entry[s][:, c * LANES:(c + 1) * LANES]
                                  for c in range(n_l)], axis=1)
        if readout:
            h_sum = h_full + hf_ref[0, rows, :].astype(F32)
            p_ref[rows, :] = (h_sum * _gelu_tanh(gate_ref[0, rows, :].astype(F32))).astype(BF16)
        else:
            ho_ref[0, rows, :] = h_full.astype(ho_ref.dtype)

    if readout:
        contrib = _bdot(p_ref[...], wout_ref[...])

        @pl.when(h == 0)
        def _():
            acc_ref[...] = contrib

        @pl.when(h != 0)
        def _():
            acc_ref[...] += contrib

        @pl.when(h == n_h - 1)
        def _():
            y = acc_ref[...] + bout_ref[...]
            v = alpha * x_ref[0] + g1_ref[0, 0] * y
            xo_ref[0] = _layer_norm(v, lng_ref[...], lnb_ref[...])


def _rglru_scan(proj, conv_w, conv_b, wg, bg, lam, h0, *, reverse, h_dtype=BF16, readout=None, alpha=None):
    nb, s, r2 = proj.shape
    r = r2 // 2
    n_h, hd = wg.shape[0], wg.shape[1]
    tm = _tile(TM_SCAN, s)
    n_t = s // tm
    hb = tm // BF16_ROWS
    n_hb = s // BF16_ROWS

    def tix(i):
        return (n_t - 1 - i) if reverse else i

    in_specs = [
        pl.BlockSpec((1, tm, hd), lambda b, i, h: (b, tix(i), n_h + h)),
        pl.BlockSpec((1, BF16_ROWS, hd), lambda b, i, h: (b, jnp.maximum(tix(i) * hb - 1, 0), n_h + h)),
        pl.BlockSpec((1, BF16_ROWS, hd), lambda b, i, h: (b, jnp.minimum((tix(i) + 1) * hb, n_hb - 1), n_h + h)),
        pl.BlockSpec((conv_w.shape[0], hd), lambda b, i, h: (0, h)),
        pl.BlockSpec((1, hd), lambda b, i, h: (0, h)),
        pl.BlockSpec((1, hd, 2 * hd), lambda b, i, h: (h, 0, 0)),
        pl.BlockSpec((1, 1, 2 * hd), lambda b, i, h: (h, 0, 0)),
        pl.BlockSpec((1, hd), lambda b, i, h: (0, h)),
        pl.BlockSpec((1, n_h, 1, hd), lambda b, i, h: (b, 0, 0, 0)),
    ]
    args = [proj, proj, proj, conv_w, conv_b.reshape(1, r), wg, bg, lam.reshape(1, r), h0]
    scratch = [
        pltpu.VMEM((tm + 2 * SUBLANES, hd), F32),
        pltpu.VMEM((hd // LANES, tm, LANES), F32),
        pltpu.VMEM((hd // LANES, tm, LANES), F32),
        pltpu.VMEM((n_h, 1, hd), F32),
    ]
    hlast_spec = pl.BlockSpec((1, n_h, 1, hd), lambda b, i, h: (b, 0, 0, 0))
    hlast_shape = jax.ShapeDtypeStruct((nb, n_h, 1, hd), F32)
    if readout is None:
        out_specs = [pl.BlockSpec((1, tm, hd), lambda b, i, h: (b, tix(i), h)), hlast_spec]
        out_shape = [jax.ShapeDtypeStruct((nb, s, r), h_dtype), hlast_shape]
    else:
        h_other, w_out, b_out, x, mod, ln_g, ln_b = readout
        d = x.shape[-1]
        in_specs += [
            pl.BlockSpec((1, tm, hd), lambda b, i, h: (b, tix(i), h)),
            pl.BlockSpec((1, tm, hd), lambda b, i, h: (b, tix(i), h)),
            pl.BlockSpec((hd, d), lambda b, i, h: (h, 0)),
            pl.BlockSpec((1, d), lambda b, i, h: (0, 0)),
            pl.BlockSpec((1, tm, d), lambda b, i, h: (b, tix(i), 0)),
            pl.BlockSpec((1, 1, 1, d), lambda b, i, h: (2, b, 0, 0)),
            pl.BlockSpec((1, d), lambda b, i, h: (0, 0)),
            pl.BlockSpec((1, d), lambda b, i, h: (0, 0)),
        ]
        args += [h_other, proj, w_out, b_out.reshape(1, d), x, mod, ln_g.reshape(1, d), ln_b.reshape(1, d)]
        scratch += [pltpu.VMEM((tm, hd), BF16), pltpu.VMEM((tm, d), F32)]
        out_specs = [pl.BlockSpec((1, tm, d), lambda b, i, h: (b, tix(i), 0)), hlast_spec]
        out_shape = [jax.ShapeDtypeStruct((nb, s, d), F32), hlast_shape]
    kern = functools.partial(_rglru_kernel, reverse=reverse, readout=readout is not None,
                             n_t=n_t, tm=tm, alpha=alpha)
    return pl.pallas_call(
        kern,
        grid=(nb, n_t, n_h),
        in_specs=in_specs,
        out_specs=out_specs,
        out_shape=out_shape,
        scratch_shapes=scratch,
        compiler_params=_cparams(3),
        name="rglru_bwd_readout" if readout is not None else ("rglru_bwd" if reverse else "rglru_fwd"),
    )(*args)


def _glu_kernel(x_ref, sh_ref, sc_ref, wv_ref, wg_ref, bv_ref, bgate_ref, o_ref):
    h = (x_ref[0] * (1.0 + sc_ref[0, 0]) + sh_ref[0, 0]).astype(BF16)
    val = _bdot(h, wv_ref[...]) + bv_ref[...]
    gate = _bdot(h, wg_ref[...]) + bgate_ref[...]
    o_ref[0] = (val * _sigmoid(gate)).astype(o_ref.dtype)


def _modulated_glu(x, mod, w_bf16, bias):
    nb, s, d = x.shape
    n = w_bf16.shape[1] // 2
    tm = _tile(TM_PROJ, s)
    tn = _tile(1024, n)
    nj = n // tn
    b2 = bias.reshape(1, 2 * n)
    return pl.pallas_call(
        _glu_kernel,
        grid=(nj, nb, s // tm),
        in_specs=[
            pl.BlockSpec((1, tm, d), lambda j, b, i: (b, i, 0)),
            pl.BlockSpec((1, 1, 1, d), lambda j, b, i: (0, b, 0, 0)),
            pl.BlockSpec((1, 1, 1, d), lambda j, b, i: (1, b, 0, 0)),
            pl.BlockSpec((d, tn), lambda j, b, i: (0, j)),
            pl.BlockSpec((d, tn), lambda j, b, i: (0, nj + j)),
            pl.BlockSpec((1, tn), lambda j, b, i: (0, j)),
            pl.BlockSpec((1, tn), lambda j, b, i: (0, nj + j)),
        ],
        out_specs=pl.BlockSpec((1, tm, tn), lambda j, b, i: (b, i, j)),
        out_shape=jax.ShapeDtypeStruct((nb, s, n), BF16),
        compiler_params=_cparams(3),
        name="modulated_glu",
    )(x, mod, mod, w_bf16, w_bf16, b2, b2)


def _conformer_tail_kernel(cur_ref, prev_ref, next_ref, wdw_ref, bdw_ref, ng_ref, nb_ref, w2_ref, b2_ref,
                           x_ref, g1_ref, lng_ref, lnb_ref, xo_ref, win_ref, z_ref, shift_ref, *, n_t, tm, alpha,
                           lane_chunk, row_chunk):
    i = pl.program_id(1)
    d = cur_ref.shape[-1]
    halo = BF16_ROWS
    win_ref[0:halo, :] = jnp.where(i == 0, 0.0, prev_ref[0].astype(F32))
    win_ref[halo:halo + tm, :] = cur_ref[0].astype(F32)
    win_ref[halo + tm:, :] = jnp.where(i == n_t - 1, 0.0, next_ref[0].astype(F32))
    n_taps = wdw_ref.shape[0]
    base = halo - (n_taps - 1) // 2

    n_shift = shift_ref.shape[1]
    for c in range(d // lane_chunk):
        lanes = slice(c * lane_chunk, (c + 1) * lane_chunk)
        for s in range(SUBLANES):
            shift_ref[s] = win_ref[pl.ds(s, n_shift), lanes]

        def rows_step(rc, _, lanes=lanes):
            r0 = pl.multiple_of(rc * row_chunk, row_chunk)
            acc = jnp.zeros((row_chunk, lane_chunk), F32) + bdw_ref[:, lanes]
            for k in range(n_taps):
                q, s = divmod(base + k, SUBLANES)
                rows = pl.ds(pl.multiple_of(r0 + q * SUBLANES, SUBLANES), row_chunk)
                acc = acc + wdw_ref[k:k + 1, lanes] * shift_ref[s, rows, :]
            z_ref[pl.ds(r0, row_chunk), lanes] = acc
            return 0

        lax.fori_loop(0, tm // row_chunk, rows_step, 0)

    z = _layer_norm(z_ref[...], ng_ref[...], nb_ref[...])
    y = _bdot(_silu(z).astype(BF16), w2_ref[...]) + b2_ref[...]
    v = alpha * x_ref[0] + g1_ref[0, 0] * y
    xo_ref[0] = _layer_norm(v, lng_ref[...], lnb_ref[...])


def _conformer_tail(z, w_dw, b_dw, n_g, n_b, w2_bf16, b2, x, mod, ln_g, ln_b, alpha):
    nb, s, d = z.shape
    tm = _tile(TM_CONV, s)
    n_t = s // tm
    hb = tm // BF16_ROWS
    n_hb = s // BF16_ROWS
    lane_chunk = min(256, d)
    n_taps = w_dw.shape[0]
    last_tap_row = BF16_ROWS + (n_taps - 1) // 2
    n_shift = tm + last_tap_row // SUBLANES * SUBLANES
    assert n_shift + SUBLANES - 1 <= tm + 2 * BF16_ROWS and (n_taps - 1) // 2 <= BF16_ROWS
    kern = functools.partial(_conformer_tail_kernel, n_t=n_t, tm=tm, alpha=alpha,
                             lane_chunk=lane_chunk, row_chunk=min(32, tm))
    vec = lambda a: a.reshape(1, d)
    vspec = pl.BlockSpec((1, d), lambda b, i: (0, 0))
    return pl.pallas_call(
        kern,
        grid=(nb, n_t),
        in_specs=[
            pl.BlockSpec((1, tm, d), lambda b, i: (b, i, 0)),
            pl.BlockSpec((1, BF16_ROWS, d), lambda b, i: (b, jnp.maximum(i * hb - 1, 0), 0)),
            pl.BlockSpec((1, BF16_ROWS, d), lambda b, i: (b, jnp.minimum((i + 1) * hb, n_hb - 1), 0)),
            pl.BlockSpec((w_dw.shape[0], d), lambda b, i: (0, 0)),
            vspec, vspec, vspec,
            pl.BlockSpec((d, d), lambda b, i: (0, 0)),
            vspec,
            pl.BlockSpec((1, tm, d), lambda b, i: (b, i, 0)),
            pl.BlockSpec((1, 1, 1, d), lambda b, i: (2, b, 0, 0)),
            vspec, vspec,
        ],
        out_specs=pl.BlockSpec((1, tm, d), lambda b, i: (b, i, 0)),
        out_shape=jax.ShapeDtypeStruct((nb, s, d), F32),
        scratch_shapes=[pltpu.VMEM((tm + 2 * BF16_ROWS, d), F32), pltpu.VMEM((tm, d), F32),
                        pltpu.VMEM((SUBLANES, n_shift, lane_chunk), F32)],
        compiler_params=_cparams(2),
        name="conformer_tail",
    )(z, z, z, w_dw, vec(b_dw), vec(n_g), vec(n_b), w2_bf16, vec(b2), x, mod, vec(ln_g), vec(ln_b))


def _router_kernel(x_ref, sh_ref, sc_ref, whi_ref, wlo_ref, bc_ref,
                   idx_ref, wt_ref, rank_ref, cnt_ref, carry_ref, *, n_exp, tm):
    i = pl.program_id(0)

    @pl.when(i == 0)
    def _():
        carry_ref[...] = jnp.zeros_like(carry_ref)

    tok = x_ref[...] * (1.0 + sc_ref[0, 0]) + sh_ref[0, 0]
    t_hi = tok.astype(BF16)
    t_lo = (tok - t_hi.astype(F32)).astype(BF16)
    nt = (((1,), (1,)), ((), ()))
    logits = (lax.dot_general(whi_ref[...], t_hi, nt, preferred_element_type=F32)
              + lax.dot_general(whi_ref[...], t_lo, nt, preferred_element_type=F32)
              + lax.dot_general(wlo_ref[...], t_hi, nt, preferred_element_type=F32))
    scores = _sigmoid(logits)
    biased = scores + bc_ref[...]

    per_group = n_exp // N_GROUPS
    sub = lax.broadcasted_iota(jnp.int32, (per_group, tm), 0)
    neg = -jnp.inf
    grp = [biased[g * per_group:(g + 1) * per_group, :] for g in range(N_GROUPS)]
    sc_g = [scores[g * per_group:(g + 1) * per_group, :] for g in range(N_GROUPS)]

    gscore = []
    for g in range(N_GROUPS):
        m1 = jnp.max(grp[g], axis=0, keepdims=True)
        first = jnp.min(jnp.where(grp[g] == m1, sub, per_group), axis=0, keepdims=True)
        m2 = jnp.max(jnp.where(sub == first, neg, grp[g]), axis=0, keepdims=True)
        gscore.append(m1 + m2)
    masked = []
    for g in range(N_GROUPS):
        beaten = jnp.zeros((1, tm), jnp.int32)
        for o in range(N_GROUPS):
            if o == g:
                continue
            wins = (gscore[o] > gscore[g]) | ((gscore[o] == gscore[g]) & (o < g))
            beaten = beaten + wins.astype(jnp.int32)
        masked.append(jnp.where(beaten < TOPK_GROUPS, grp[g], neg))

    eid = [sub + g * per_group for g in range(N_GROUPS)]
    member = [jnp.zeros((per_group, tm), F32) for _ in range(N_GROUPS)]
    idx_rows, score_rows = [], []
    for _ in range(TOP_K):
        m = masked[0]
        for g in range(1, N_GROUPS):
            m = jnp.maximum(m, masked[g])
        m = jnp.max(m, axis=0, keepdims=True)
        cand = jnp.where(masked[0] == m, eid[0], n_exp)
        for g in range(1, N_GROUPS):
            cand = jnp.minimum(cand, jnp.where(masked[g] == m, eid[g], n_exp))
        first = jnp.min(cand, axis=0, keepdims=True)
        picked = jnp.zeros((per_group, tm), F32)
        for g in range(N_GROUPS):
            sel = eid[g] == first
            picked = picked + jnp.where(sel, sc_g[g], 0.0)
            member[g] = jnp.where(sel, 1.0, member[g])
            masked[g] = jnp.where(sel, neg, masked[g])
        idx_rows.append(first)
        score_rows.append(jnp.sum(picked, axis=0, keepdims=True))

    total = score_rows[0]
    for k in range(1, TOP_K):
        total = total + score_rows[k]

    memb = jnp.concatenate(member, axis=0)
    earlier = jnp.where(lax.broadcasted_iota(jnp.int32, (tm, tm), 0)
                        < lax.broadcasted_iota(jnp.int32, (tm, tm), 1), 1.0, 0.0).astype(BF16)
    pos = _bdot(memb.astype(BF16), earlier) + carry_ref[...]
    pos_g = [pos[g * per_group:(g + 1) * per_group, :] for g in range(N_GROUPS)]
    for k in range(TOP_K):
        acc = jnp.zeros((per_group, tm), F32)
        for g in range(N_GROUPS):
            acc = acc + jnp.where(eid[g] == idx_rows[k], pos_g[g], 0.0)
        rank_ref[k:k + 1, :] = jnp.sum(acc, axis=0, keepdims=True).astype(jnp.int32)
        idx_ref[k:k + 1, :] = idx_rows[k]
        wt_ref[k:k + 1, :] = score_rows[k] / total * ROUTED_SCALE
    carry_ref[...] += jnp.sum(memb, axis=1, keepdims=True)
    cnt_ref[...] = carry_ref[...].astype(jnp.int32)


def _route(x, mod, w_router, b_corr):
    nb, s, d = x.shape
    t = nb * s
    n_exp = w_router.shape[1]
    tm = _tile(TM_ROUTE, s)
    per_b = s // tm
    w_t = w_router.T
    w_hi = w_t.astype(BF16)
    w_lo = (w_t - w_hi.astype(F32)).astype(BF16)
    kern = functools.partial(_router_kernel, n_exp=n_exp, tm=tm)
    out_spec = pl.BlockSpec((TOP_K, tm), lambda i: (0, i))
    return pl.pallas_call(
        kern,
        grid=(t // tm,),
        in_specs=[
            pl.BlockSpec((tm, d), lambda i: (i, 0)),
            pl.BlockSpec((1, 1, 1, d), lambda i: (3, i // per_b, 0, 0)),
            pl.BlockSpec((1, 1, 1, d), lambda i: (4, i // per_b, 0, 0)),
            pl.BlockSpec((n_exp, d), lambda i: (0, 0)),
            pl.BlockSpec((n_exp, d), lambda i: (0, 0)),
            pl.BlockSpec((n_exp, 1), lambda i: (0, 0)),
        ],
        out_specs=[out_spec, out_spec, out_spec, pl.BlockSpec((n_exp, 1), lambda i: (0, 0))],
        out_shape=[
            jax.ShapeDtypeStruct((TOP_K, t), jnp.int32),
            jax.ShapeDtypeStruct((TOP_K, t), F32),
            jax.ShapeDtypeStruct((TOP_K, t), jnp.int32),
            jax.ShapeDtypeStruct((n_exp, 1), jnp.int32),
        ],
        scratch_shapes=[pltpu.VMEM((n_exp, 1), F32)],
        compiler_params=_cparams(1),
        name="moe_router",
    )(x.reshape(t, d), mod, mod, w_hi, w_lo, b_corr.reshape(n_exp, 1).astype(F32))


def _slots_kernel(ps_ref, idx_ref, rank_ref, dest_ref, *, n_exp):
    idx = idx_ref[...]

    def add_start(e, acc):
        return acc + jnp.where(idx == e, ps_ref[e], 0)

    dest_ref[...] = lax.fori_loop(0, n_exp, add_start, rank_ref[...])


def _slots(idx, rank, pad_starts):
    k, t = idx.shape
    tm = _tile(2048, t)
    grid_spec = pltpu.PrefetchScalarGridSpec(
        num_scalar_prefetch=1,
        grid=(t // tm,),
        in_specs=[pl.BlockSpec((k, tm), lambda i, ps: (0, i)), pl.BlockSpec((k, tm), lambda i, ps: (0, i))],
        out_specs=pl.BlockSpec((k, tm), lambda i, ps: (0, i)),
    )
    return pl.pallas_call(
        functools.partial(_slots_kernel, n_exp=pad_starts.shape[0]),
        grid_spec=grid_spec,
        out_shape=jax.ShapeDtypeStruct((k, t), jnp.int32),
        compiler_params=_cparams(1),
        name="moe_slots",
    )(pad_starts, idx, rank)


def _slab_rows(row, n_seg):
    return pl.ds(pl.multiple_of(row * n_seg, n_seg), n_seg)


def _to_slabs(ref, value, n_seg):
    m = value.shape[0]
    for s in range(n_seg):
        ref[pl.ds(s, m, stride=n_seg), :] = value[:, s * LANES:(s + 1) * LANES]


def _from_slabs(ref, m, n_seg):
    return jnp.concatenate([ref[pl.ds(s, m, stride=n_seg), :] for s in range(n_seg)], axis=1)


def _dispatch_kernel(pend_ref, padded_ref, x_ref, sh_ref, sc_ref, dest_ref, xs_ref,
                     tok_ref, zero_ref, sem, zsem, *, n_exp, tm, bm, n_seg):
    i = pl.program_id(0)

    def zero_copy(e):
        start = pl.multiple_of((pend_ref[e] - bm) * n_seg, n_seg)
        return pltpu.make_async_copy(zero_ref, xs_ref.at[pl.ds(start, bm * n_seg)], zsem)

    @pl.when(i == 0)
    def _():
        zero_ref[...] = jnp.zeros_like(zero_ref)

        def start(e, _):
            @pl.when(padded_ref[e] > 0)
            def _():
                zero_copy(e).start()
            return 0

        def wait(e, _):
            @pl.when(padded_ref[e] > 0)
            def _():
                zero_copy(e).wait()
            return 0

        lax.fori_loop(0, n_exp, start, 0)
        lax.fori_loop(0, n_exp, wait, 0)

    _to_slabs(tok_ref, x_ref[...] * (1.0 + sc_ref[0, 0]) + sh_ref[0, 0], n_seg)

    def issue(r, _):
        for k in range(TOP_K):
            pltpu.make_async_copy(tok_ref.at[_slab_rows(r, n_seg)],
                                  xs_ref.at[_slab_rows(dest_ref[k, r], n_seg)], sem).start()
        return 0

    lax.fori_loop(0, tm, issue, 0)
    for k in range(TOP_K):
        pltpu.make_async_copy(tok_ref, xs_ref.at[pl.ds(0, tm * n_seg)], sem).wait()


def _dispatch(x, mod, dest, pad_ends, padded, n_slots, bm):
    nb, s, d = x.shape
    t = nb * s
    tm = _tile(TM_DISPATCH, s)
    per_b = s // tm
    n_exp = pad_ends.shape[0]
    n_seg = d // LANES
    assert n_seg % SUBLANES == 0, "token slabs must be whole (8, 128) tiles"
    kern = functools.partial(_dispatch_kernel, n_exp=n_exp, tm=tm, bm=bm, n_seg=n_seg)
    grid_spec = pltpu.PrefetchScalarGridSpec(
        num_scalar_prefetch=2,
        grid=(t // tm,),
        in_specs=[
            pl.BlockSpec((tm, d), lambda i, pe, pd: (i, 0)),
            pl.BlockSpec((1, 1, 1, d), lambda i, pe, pd: (3, i // per_b, 0, 0)),
            pl.BlockSpec((1, 1, 1, d), lambda i, pe, pd: (4, i // per_b, 0, 0)),
            pl.BlockSpec((TOP_K, tm), lambda i, pe, pd: (0, i), memory_space=pltpu.SMEM),
        ],
        out_specs=pl.BlockSpec(memory_space=pl.ANY),
        scratch_shapes=[
            pltpu.VMEM((tm * n_seg, LANES), F32),
            pltpu.VMEM((bm * n_seg, LANES), F32),
            pltpu.SemaphoreType.DMA(()),
            pltpu.SemaphoreType.DMA(()),
        ],
    )
    return pl.pallas_call(
        kern,
        grid_spec=grid_spec,
        out_shape=jax.ShapeDtypeStruct((n_slots * n_seg, LANES), F32),
        compiler_params=_cparams(1),
        name="moe_dispatch",
    )(pad_ends, padded, x.reshape(t, d), mod, mod, dest)


def _experts_kernel(be_ref, nu_ref, xs_ref, wg_ref, wu_ref, wd_ref, ys_ref, *, bm, n_seg):
    b = pl.program_id(0)

    @pl.when(b < nu_ref[0])
    def _():
        x = _from_slabs(xs_ref, bm, n_seg).astype(BF16)
        g = _bdot(x, wg_ref[0])
        u = _bdot(x, wu_ref[0])
        _to_slabs(ys_ref, _bdot((_silu(g) * u).astype(BF16), wd_ref[0]), n_seg)

    @pl.when(b >= nu_ref[0])
    def _():
        ys_ref[...] = jnp.zeros_like(ys_ref)


def _grouped_experts(xs, blk_e, n_used, w_gate, w_up, w_down, bm):
    d, f = w_gate.shape[1:]
    n_seg = d // LANES
    n_blk = xs.shape[0] // (bm * n_seg)

    def blk(b, be, nu):
        return jnp.minimum(b, nu[0] - 1)

    grid_spec = pltpu.PrefetchScalarGridSpec(
        num_scalar_prefetch=2,
        grid=(n_blk,),
        in_specs=[
            pl.BlockSpec((bm * n_seg, LANES), lambda b, be, nu: (blk(b, be, nu), 0)),
            pl.BlockSpec((1, d, f), lambda b, be, nu: (be[blk(b, be, nu)], 0, 0)),
            pl.BlockSpec((1, d, f), lambda b, be, nu: (be[blk(b, be, nu)], 0, 0)),
            pl.BlockSpec((1, f, d), lambda b, be, nu: (be[blk(b, be, nu)], 0, 0)),
        ],
        out_specs=pl.BlockSpec((bm * n_seg, LANES), lambda b, be, nu: (jnp.where(b < nu[0], b, n_blk - 1), 0)),
    )
    return pl.pallas_call(
        functools.partial(_experts_kernel, bm=bm, n_seg=n_seg),
        grid_spec=grid_spec,
        out_shape=jax.ShapeDtypeStruct(xs.shape, F32),
        compiler_params=_cparams(1),
        name="moe_experts",
    )(blk_e, n_used, xs, w_gate, w_up, w_down)


def _combine_kernel(x_ref, sh_ref, sc_ref, gt_ref, dest_ref, dest_next_ref, wt_ref, ys_ref,
                    wsg_ref, wsu_ref, wsd_ref, lng_ref, lnb_ref, xo_ref, rows_ref, sem, *, tm, alpha, n_seg):
    i = pl.program_id(0)
    n_steps = pl.num_programs(0)
    slot = i % 2

    def gather(d_ref, buf):
        def issue(r, _):
            for k in range(TOP_K):
                pltpu.make_async_copy(ys_ref.at[_slab_rows(d_ref[k, r], n_seg)],
                                      rows_ref.at[buf, k, _slab_rows(r, n_seg)], sem.at[buf]).start()
            return 0
        lax.fori_loop(0, tm, issue, 0)

    @pl.when(i == 0)
    def _():
        gather(dest_ref, 0)

    @pl.when(i + 1 < n_steps)
    def _():
        gather(dest_next_ref, 1 - slot)

    x = x_ref[...]
    tok = (x * (1.0 + sc_ref[0, 0]) + sh_ref[0, 0]).astype(BF16)
    hid = (_silu(_bdot(tok, wsg_ref[...])) * _bdot(tok, wsu_ref[...])).astype(BF16)
    f = _bdot(hid, wsd_ref[...])

    for k in range(TOP_K):
        pltpu.make_async_copy(ys_ref.at[pl.ds(0, tm * n_seg)], rows_ref.at[slot, k], sem.at[slot]).wait()
    wt = wt_ref[...]
    for k in range(TOP_K):
        f = f + wt[:, k:k + 1] * _from_slabs(rows_ref.at[slot, k], tm, n_seg)
    v = alpha * x + gt_ref[0, 0] * f
    xo_ref[...] = _layer_norm(v, lng_ref[...], lnb_ref[...])


def _combine(x, mod, dest, wt_rows, ys, ws_gate, ws_up, ws_down, ln_g, ln_b, alpha):
    nb, s, d = x.shape
    t = nb * s
    tm = _tile(TM_COMBINE, s)
    per_b = s // tm
    f = ws_gate.shape[-1]
    n_seg = d // LANES
    kern = functools.partial(_combine_kernel, tm=tm, alpha=alpha, n_seg=n_seg)
    vspec = pl.BlockSpec((1, d), lambda i: (0, 0))
    n_steps = t // tm
    out = pl.pallas_call(
        kern,
        grid=(n_steps,),
        in_specs=[
            pl.BlockSpec((tm, d), lambda i: (i, 0)),
            pl.BlockSpec((1, 1, 1, d), lambda i: (3, i // per_b, 0, 0)),
            pl.BlockSpec((1, 1, 1, d), lambda i: (4, i // per_b, 0, 0)),
            pl.BlockSpec((1, 1, 1, d), lambda i: (5, i // per_b, 0, 0)),
            pl.BlockSpec((TOP_K, tm), lambda i: (0, i), memory_space=pltpu.SMEM),
            pl.BlockSpec((TOP_K, tm), lambda i: (0, jnp.minimum(i + 1, n_steps - 1)), memory_space=pltpu.SMEM),
            pl.BlockSpec((tm, TOP_K), lambda i: (i, 0)),
            pl.BlockSpec(memory_space=pl.ANY),
            pl.BlockSpec((d, f), lambda i: (0, 0)),
            pl.BlockSpec((d, f), lambda i: (0, 0)),
            pl.BlockSpec((f, d), lambda i: (0, 0)),
            vspec, vspec,
        ],
        out_specs=pl.BlockSpec((tm, d), lambda i: (i, 0)),
        out_shape=jax.ShapeDtypeStruct((t, d), F32),
        scratch_shapes=[pltpu.VMEM((2, TOP_K, tm * n_seg, LANES), F32), pltpu.SemaphoreType.DMA((2,))],
        compiler_params=_cparams(1),
        name="moe_combine",
    )(x.reshape(t, d), mod, mod, mod, dest, dest, wt_rows, ys, ws_gate, ws_up, ws_down,
      ln_g.reshape(1, d), ln_b.reshape(1, d))
    return out.reshape(nb, s, d)


def _moe_block(x, mod, w_router, b_router, w_gate, w_up, w_down, ws_gate, ws_up, ws_down, ln_g, ln_b, alpha):
    nb, s, d = x.shape
    t = nb * s
    n_exp = w_router.shape[1]
    bm = _tile(MOE_BLOCK, t * TOP_K)
    idx, wt, rank, counts = _route(x, mod, w_router, b_router)

    counts = counts.reshape(n_exp)
    padded = (counts + bm - 1) // bm * bm
    pad_ends = jnp.cumsum(padded).astype(jnp.int32)
    pad_starts = pad_ends - padded
    dest = _slots(idx, rank, pad_starts.astype(jnp.int32))
    n_blk = t * TOP_K // bm + n_exp
    n_used = (pad_ends[-1:] // bm).astype(jnp.int32)
    blk_start = jnp.arange(n_blk, dtype=jnp.int32) * bm
    blk_e = jnp.sum((pad_ends[None, :] <= blk_start[:, None]).astype(jnp.int32), axis=1)
    blk_e = jnp.minimum(blk_e, n_exp - 1).astype(jnp.int32)

    xs = _dispatch(x, mod, dest, pad_ends, padded.astype(jnp.int32), n_blk * bm, bm)
    ys = _grouped_experts(xs, blk_e, n_used, w_gate.astype(BF16), w_up.astype(BF16), w_down.astype(BF16), bm)
    return _combine(x, mod, dest, wt.T, ys, ws_gate.astype(BF16), ws_up.astype(BF16), ws_down.astype(BF16),
                    ln_g, ln_b, alpha)


def kernel(x, c, ctx, c_ctx, w_mod, b_mod, ln1_g, ln1_b, ln2_g, ln2_b, rg_w_in, rg_b_in, rg_conv_w, rg_conv_b, rg_w_gates, rg_b_gates, rg_lambda, rg_w_out, rg_b_out, cf_w_pw1, cf_b_pw1, cf_w_dw, cf_b_dw, cf_norm_g, cf_norm_b, cf_w_pw2, cf_b_pw2, moe_w_router, moe_b_router, moe_w_gate, moe_w_up, moe_w_down, sh_w_gate, sh_w_up, sh_w_down):
    nb, s, d = x.shape
    depth = w_mod.shape[0]
    assert depth == 2 and nb + 1 <= SUBLANES
    alpha = (2 * depth) ** 0.25

    c_rows = jnp.zeros((SUBLANES, d), F32).at[:nb].set(c).at[nb].set(c_ctx)
    mod_all = _adaln_vectors(c_rows, w_mod, b_mod)
    mod_all = mod_all.reshape(depth, SUBLANES, 6, 1, d).transpose(0, 2, 1, 3, 4)
    mod_lat = [mod_all[l, :, :nb] for l in range(depth)]
    mod_ctx = jnp.broadcast_to(mod_all[0, :, nb:nb + 1], (6, nb, 1, d))

    n_h, hd = rg_w_gates.shape[3], rg_w_gates.shape[4]
    r = n_h * hd
    w_in = rg_w_in[0].astype(BF16)
    wg = [jnp.concatenate([rg_w_gates[0, dr, 0], rg_w_gates[0, dr, 1]], axis=-1).astype(BF16) for dr in range(2)]
    bg = [jnp.concatenate([rg_b_gates[0, dr, 0], rg_b_gates[0, dr, 1]], axis=-1).reshape(n_h, 1, 2 * hd)
          for dr in range(2)]
    conv_w, conv_b = rg_conv_w[0], rg_conv_b[0]
    zero_state = jnp.zeros((nb, n_h, 1, hd), F32)

    proj_c = _modulated_projection(ctx, mod_ctx, w_in, rg_b_in[0], BF16)
    _, hcf = _rglru_scan(proj_c, conv_w, conv_b, wg[0], bg[0], rg_lambda[0, 0], zero_state, reverse=False)
    _, hcb = _rglru_scan(proj_c, conv_w, conv_b, wg[1], bg[1], rg_lambda[0, 1], zero_state, reverse=True)

    proj_x = _modulated_projection(x, mod_lat[0], w_in, rg_b_in[0], BF16)
    hxf, _ = _rglru_scan(proj_x, conv_w, conv_b, wg[0], bg[0], rg_lambda[0, 0], hcf, reverse=False)
    readout = (hxf, rg_w_out[0].astype(BF16), rg_b_out[0], x, mod_lat[0], ln1_g[0], ln1_b[0])
    x, _ = _rglru_scan(proj_x, conv_w, conv_b, wg[1], bg[1], rg_lambda[0, 1], hcb, reverse=True,
                       readout=readout, alpha=alpha)
    x = _moe_block(x, mod_lat[0], moe_w_router[0], moe_b_router[0], moe_w_gate[0], moe_w_up[0], moe_w_down[0],
                   sh_w_gate[0], sh_w_up[0], sh_w_down[0], ln2_g[0], ln2_b[0], alpha)

    z = _modulated_glu(x, mod_lat[1], cf_w_pw1[0].astype(BF16), cf_b_pw1[0])
    x = _conformer_tail(z, cf_w_dw[0], cf_b_dw[0], cf_norm_g[0], cf_norm_b[0], cf_w_pw2[0].astype(BF16),
                        cf_b_pw2[0], x, mod_lat[1], ln1_g[1], ln1_b[1], alpha)
    x = _moe_block(x, mod_lat[1], moe_w_router[1], moe_b_router[1], moe_w_gate[1], moe_w_up[1], moe_w_down[1],
                   sh_w_gate[1], sh_w_up[1], sh_w_down[1], ln2_g[1], ln2_b[1], alpha)
    return x
```

```python
import functools

import jax
import jax.numpy as jnp
from jax import lax
from jax.experimental import pallas as pl
from jax.experimental.pallas import tpu as pltpu

F32 = jnp.float32
BF16 = jnp.bfloat16

RG_C = 8.0
N_GROUPS = 8
TOPK_GROUPS = 4
TOP_K = 8
ROUTED_SCALE = 2.5
LN_EPS = 1e-5

VMEM_LIMIT_BYTES = 56 * 1024 * 1024
SUBLANES = 8
LANES = 128
BF16_ROWS = 16

TM_PROJ = 512
TM_SCAN = 512
TM_CONV = 256
TM_ROUTE = 512
TM_DISPATCH = 256
TM_COMBINE = 128
MOE_BLOCK = 512


def _tile(default, n):
    t = min(default, n)
    assert n % t == 0, (default, n)
    return t


def _cparams(n_axes):
    return pltpu.CompilerParams(
        dimension_semantics=("arbitrary",) * n_axes,
        vmem_limit_bytes=VMEM_LIMIT_BYTES,
    )


def _sigmoid(v):
    return 0.5 * jnp.tanh(0.5 * v) + 0.5


def _sqrt_nonneg(v):
    return jnp.where(v > 0.0, v * lax.rsqrt(v), 0.0)


def _silu(v):
    return v * _sigmoid(v)


def _gelu_tanh(v):
    c = 0.7978845608028654
    return 0.5 * v * (1.0 + jnp.tanh(c * (v + 0.044715 * (v * v * v))))


def _softplus(v):
    return jnp.maximum(v, 0.0) + jnp.log(1.0 + jnp.exp(-jnp.abs(v)))


def _layer_norm(v, g, b):
    mu = jnp.mean(v, axis=-1, keepdims=True)
    d = v - mu
    var = jnp.mean(d * d, axis=-1, keepdims=True)
    return d * lax.rsqrt(var + LN_EPS) * g + b


def _bdot(a, b):
    return jnp.dot(a, b, preferred_element_type=F32)


def _mod_kernel(c_ref, w_ref, b_ref, o_ref):
    s = _silu(c_ref[...])
    o_ref[0] = _bdot(s.astype(BF16), w_ref[0].astype(BF16)) + b_ref[0]


def _adaln_vectors(c_rows, w_mod, b_mod):
    depth, d, d6 = w_mod.shape
    tn = _tile(1024, d6)
    return pl.pallas_call(
        _mod_kernel,
        grid=(depth, d6 // tn),
        in_specs=[
            pl.BlockSpec((SUBLANES, d), lambda l, n: (0, 0)),
            pl.BlockSpec((1, d, tn), lambda l, n: (l, 0, n)),
            pl.BlockSpec((1, 1, tn), lambda l, n: (l, 0, n)),
        ],
        out_specs=pl.BlockSpec((1, SUBLANES, tn), lambda l, n: (l, 0, n)),
        out_shape=jax.ShapeDtypeStruct((depth, SUBLANES, d6), F32),
        compiler_params=_cparams(2),
        name="adaln_vectors",
    )(c_rows, w_mod, b_mod.reshape(depth, 1, d6))


def _mod_spec(chunk, batch_of):
    def index(*ids):
        return (chunk, batch_of(*ids), 0, 0)
    return index


def _proj_kernel(x_ref, sh_ref, sc_ref, w_ref, b_ref, o_ref):
    h = x_ref[0] * (1.0 + sc_ref[0, 0]) + sh_ref[0, 0]
    o_ref[0] = (_bdot(h.astype(BF16), w_ref[...]) + b_ref[...]).astype(o_ref.dtype)


def _modulated_projection(x, mod, w_bf16, bias, out_dtype):
    nb, s, d = x.shape
    n = w_bf16.shape[1]
    tm = _tile(TM_PROJ, s)
    tn = _tile(1280, n)
    dm = mod.shape[-1]
    return pl.pallas_call(
        _proj_kernel,
        grid=(n // tn, nb, s // tm),
        in_specs=[
            pl.BlockSpec((1, tm, d), lambda j, b, i: (b, i, 0)),
            pl.BlockSpec((1, 1, 1, dm), lambda j, b, i: (0, b, 0, 0)),
            pl.BlockSpec((1, 1, 1, dm), lambda j, b, i: (1, b, 0, 0)),
            pl.BlockSpec((d, tn), lambda j, b, i: (0, j)),
            pl.BlockSpec((1, tn), lambda j, b, i: (0, j)),
        ],
        out_specs=pl.BlockSpec((1, tm, tn), lambda j, b, i: (b, i, j)),
        out_shape=jax.ShapeDtypeStruct((nb, s, n), out_dtype),
        compiler_params=_cparams(3),
        name="modulated_projection",
    )(x, mod, mod, w_bf16, bias.reshape(1, n))


def _rglru_kernel(*refs, reverse, readout, n_t, tm, alpha):
    if readout:
        (cur_ref, prev_ref, next_ref, cw_ref, cb_ref, wg_ref, bg_ref, lam_ref, h0_ref,
         hf_ref, gate_ref, wout_ref, bout_ref, x_ref, g1_ref, lng_ref, lnb_ref,
         xo_ref, hlast_ref,
         win_ref, a_ref, hl_ref, carry_ref, p_ref, acc_ref) = refs
    else:
        (cur_ref, prev_ref, next_ref, cw_ref, cb_ref, wg_ref, bg_ref, lam_ref, h0_ref,
         ho_ref, hlast_ref,
         win_ref, a_ref, hl_ref, carry_ref) = refs
    i = pl.program_id(1)
    h = pl.program_id(2)
    n_h = pl.num_programs(2)
    ti = (n_t - 1 - i) if reverse else i
    hd = cur_ref.shape[-1]
    chunk = tm // SUBLANES

    prev = prev_ref[0].astype(F32)[BF16_ROWS - SUBLANES:, :]
    nxt = next_ref[0].astype(F32)[:SUBLANES, :]
    win_ref[0:SUBLANES, :] = jnp.where(ti == 0, 0.0, prev)
    win_ref[SUBLANES:SUBLANES + tm, :] = cur_ref[0].astype(F32)
    win_ref[SUBLANES + tm:, :] = jnp.where(ti == n_t - 1, 0.0, nxt)
    n_taps = cw_ref.shape[0]
    xb = cb_ref[...] + cw_ref[0:1, :] * win_ref[pl.ds(SUBLANES - 1, tm), :]
    for k in range(1, n_taps):
        xb = xb + cw_ref[k:k + 1, :] * win_ref[pl.ds(SUBLANES - 1 + k, tm), :]

    pre = _bdot(xb.astype(BF16), wg_ref[0]) + bg_ref[0]
    r = _sigmoid(pre[:, :hd])
    gi = _sigmoid(pre[:, hd:])
    log_a = (-RG_C) * r * _softplus(-lam_ref[...])
    a = jnp.exp(log_a)
    u = _sqrt_nonneg(1.0 - a * a) * gi * xb
    n_l = hd // LANES
    for c in range(n_l):
        a_ref[c] = a[:, c * LANES:(c + 1) * LANES]
        hl_ref[c] = u[:, c * LANES:(c + 1) * LANES]

    def scan_step(jj, carry):
        j = (chunk - 1 - jj) if reverse else jj
        rows = pl.ds(j, SUBLANES, stride=chunk)
        out = []
        for c in range(n_l):
            a_cum, h_loc = carry[c]
            aj = a_ref[c, rows, :]
            h_loc = aj * h_loc + hl_ref[c, rows, :]
            a_cum = a_cum * aj
            hl_ref[c, rows, :] = h_loc
            a_ref[c, rows, :] = a_cum
            out.append((a_cum, h_loc))
        return tuple(out)

    init = tuple((jnp.ones((SUBLANES, LANES), F32), jnp.zeros((SUBLANES, LANES), F32)) for _ in range(n_l))
    ends = lax.fori_loop(0, chunk, scan_step, init, unroll=SUBLANES)
    a_end = jnp.concatenate([e[0] for e in ends], axis=1)
    h_end = jnp.concatenate([e[1] for e in ends], axis=1)

    @pl.when(i == 0)
    def _():
        carry_ref[h] = h0_ref[0, h]

    state = carry_ref[h]
    entry = [None] * SUBLANES
    for s in (range(SUBLANES - 1, -1, -1) if reverse else range(SUBLANES)):
        entry[s] = state
        state = a_end[s:s + 1, :] * state + h_end[s:s + 1, :]
    carry_ref[h] = state
    hlast_ref[0, h] = state

    for s in range(SUBLANES):
        rows = pl.ds(s * chunk, chunk)
        h_full = jnp.concatenate([hl_ref[c, rows, :] + a_ref[c, rows, :] * entry[s][:, c * LANES:(c + 1) * LANES]
                                  for c in range(n_l)], axis=1)
        if readout:
            h_sum = h_full + hf_ref[0, rows, :].astype(F32)
            p_ref[rows, :] = (h_sum * _gelu_tanh(gate_ref[0, rows, :].astype(F32))).astype(BF16)
        else:
            ho_ref[0, rows, :] = h_full.astype(ho_ref.dtype)

    if readout:
        contrib = _bdot(p_ref[...], wout_ref[...])

        @pl.when(h == 0)
        def _():
            acc_ref[...] = contrib

        @pl.when(h != 0)
        def _():
            acc_ref[...] += contrib

        @pl.when(h == n_h - 1)
        def _():
            y = acc_ref[...] + bout_ref[...]
            v = alpha * x_ref[0] + g1_ref[0, 0] * y
            xo_ref[0] = _layer_norm(v, lng_ref[...], lnb_ref[...])


def _rglru_scan(proj, conv_w, conv_b, wg, bg, lam, h0, *, reverse, h_dtype=BF16, readout=None, alpha=None):
    nb, s, r2 = proj.shape
    r = r2 // 2
    n_h, hd = wg.shape[0], wg.shape[1]
    tm = _tile(TM_SCAN, s)
    n_t = s // tm
    hb = tm // BF16_ROWS
    n_hb = s // BF16_ROWS

    def tix(i):
        return (n_t - 1 - i) if reverse else i

    in_specs = [
        pl.BlockSpec((1, tm, hd), lambda b, i, h: (b, tix(i), n_h + h)),
        pl.BlockSpec((1, BF16_ROWS, hd), lambda b, i, h: (b, jnp.maximum(tix(i) * hb - 1, 0), n_h + h)),
        pl.BlockSpec((1, BF16_ROWS, hd), lambda b, i, h: (b, jnp.minimum((tix(i) + 1) * hb, n_hb - 1), n_h + h)),
        pl.BlockSpec((conv_w.shape[0], hd), lambda b, i, h: (0, h)),
        pl.BlockSpec((1, hd), lambda b, i, h: (0, h)),
        pl.BlockSpec((1, hd, 2 * hd), lambda b, i, h: (h, 0, 0)),
        pl.BlockSpec((1, 1, 2 * hd), lambda b, i, h: (h, 0, 0)),
        pl.BlockSpec((1, hd), lambda b, i, h: (0, h)),
        pl.BlockSpec((1, n_h, 1, hd), lambda b, i, h: (b, 0, 0, 0)),
    ]
    args = [proj, proj, proj, conv_w, conv_b.reshape(1, r), wg, bg, lam.reshape(1, r), h0]
    scratch = [
        pltpu.VMEM((tm + 2 * SUBLANES, hd), F32),
        pltpu.VMEM((hd // LANES, tm, LANES), F32),
        pltpu.VMEM((hd // LANES, tm, LANES), F32),
        pltpu.VMEM((n_h, 1, hd), F32),
    ]
    hlast_spec = pl.BlockSpec((1, n_h, 1, hd), lambda b, i, h: (b, 0, 0, 0))
    hlast_shape = jax.ShapeDtypeStruct((nb, n_h, 1, hd), F32)
    if readout is None:
        out_specs = [pl.BlockSpec((1, tm, hd), lambda b, i, h: (b, tix(i), h)), hlast_spec]
        out_shape = [jax.ShapeDtypeStruct((nb, s, r), h_dtype), hlast_shape]
    else:
        h_other, w_out, b_out, x, mod, ln_g, ln_b = readout
        d = x.shape[-1]
        in_specs += [
            pl.BlockSpec((1, tm, hd), lambda b, i, h: (b, tix(i), h)),
            pl.BlockSpec((1, tm, hd), lambda b, i, h: (b, tix(i), h)),
            pl.BlockSpec((hd, d), lambda b, i, h: (h, 0)),
            pl.BlockSpec((1, d), lambda b, i, h: (0, 0)),
            pl.BlockSpec((1, tm, d), lambda b, i, h: (b, tix(i), 0)),
            pl.BlockSpec((1, 1, 1, d), lambda b, i, h: (2, b, 0, 0)),
            pl.BlockSpec((1, d), lambda b, i, h: (0, 0)),
            pl.BlockSpec((1, d), lambda b, i, h: (0, 0)),
        ]
        args += [h_other, proj, w_out, b_out.reshape(1, d), x, mod, ln_g.reshape(1, d), ln_b.reshape(1, d)]
        scratch += [pltpu.VMEM((tm, hd), BF16), pltpu.VMEM((tm, d), F32)]
        out_specs = [pl.BlockSpec((1, tm, d), lambda b, i, h: (b, tix(i), 0)), hlast_spec]
        out_shape = [jax.ShapeDtypeStruct((nb, s, d), F32), hlast_shape]
    kern = functools.partial(_rglru_kernel, reverse=reverse, readout=readout is not None,
                             n_t=n_t, tm=tm, alpha=alpha)
    return pl.pallas_call(
        kern,
        grid=(nb, n_t, n_h),
        in_specs=in_specs,
        out_specs=out_specs,
        out_shape=out_shape,
        scratch_shapes=scratch,
        compiler_params=_cparams(3),
        name="rglru_bwd_readout" if readout is not None else ("rglru_bwd" if reverse else "rglru_fwd"),
    )(*args)


def _glu_kernel(x_ref, sh_ref, sc_ref, wv_ref, wg_ref, bv_ref, bgate_ref, o_ref):
    h = (x_ref[0] * (1.0 + sc_ref[0, 0]) + sh_ref[0, 0]).astype(BF16)
    val = _bdot(h, wv_ref[...]) + bv_ref[...]
    gate = _bdot(h, wg_ref[...]) + bgate_ref[...]
    o_ref[0] = (val * _sigmoid(gate)).astype(o_ref.dtype)


def _modulated_glu(x, mod, w_bf16, bias):
    nb, s, d = x.shape
    n = w_bf16.shape[1] // 2
    tm = _tile(TM_PROJ, s)
    tn = _tile(1024, n)
    nj = n // tn
    b2 = bias.reshape(1, 2 * n)
    return pl.pallas_call(
        _glu_kernel,
        grid=(nj, nb, s // tm),
        in_specs=[
            pl.BlockSpec((1, tm, d), lambda j, b, i: (b, i, 0)),
            pl.BlockSpec((1, 1, 1, d), lambda j, b, i: (0, b, 0, 0)),
            pl.BlockSpec((1, 1, 1, d), lambda j, b, i: (1, b, 0, 0)),
            pl.BlockSpec((d, tn), lambda j, b, i: (0, j)),
            pl.BlockSpec((d, tn), lambda j, b, i: (0, nj + j)),
            pl.BlockSpec((1, tn), lambda j, b, i: (0, j)),
            pl.BlockSpec((1, tn), lambda j, b, i: (0, nj + j)),
        ],
        out_specs=pl.BlockSpec((1, tm, tn), lambda j, b, i: (b, i, j)),
        out_shape=jax.ShapeDtypeStruct((nb, s, n), BF16),
        compiler_params=_cparams(3),
        name="modulated_glu",
    )(x, mod, mod, w_bf16, w_bf16, b2, b2)


def _conformer_tail_kernel(cur_ref, prev_ref, next_ref, wdw_ref, bdw_ref, ng_ref, nb_ref, w2_ref, b2_ref,
                           x_ref, g1_ref, lng_ref, lnb_ref, xo_ref, win_ref, z_ref, shift_ref, *, n_t, tm, alpha,
                           lane_chunk, row_chunk):
    i = pl.program_id(1)
    d = cur_ref.shape[-1]
    halo = BF16_ROWS
    win_ref[0:halo, :] = jnp.where(i == 0, 0.0, prev_ref[0].astype(F32))
    win_ref[halo:halo + tm, :] = cur_ref[0].astype(F32)
    win_ref[halo + tm:, :] = jnp.where(i == n_t - 1, 0.0, next_ref[0].astype(F32))
    n_taps = wdw_ref.shape[0]
    base = halo - (n_taps - 1) // 2

    n_shift = shift_ref.shape[1]
    for c in range(d // lane_chunk):
        lanes = slice(c * lane_chunk, (c + 1) * lane_chunk)
        for s in range(SUBLANES):
            shift_ref[s] = win_ref[pl.ds(s, n_shift), lanes]

        def rows_step(rc, _, lanes=lanes):
            r0 = pl.multiple_of(rc * row_chunk, row_chunk)
            acc = jnp.zeros((row_chunk, lane_chunk), F32) + bdw_ref[:, lanes]
            for k in range(n_taps):
                q, s = divmod(base + k, SUBLANES)
                rows = pl.ds(pl.multiple_of(r0 + q * SUBLANES, SUBLANES), row_chunk)
                acc = acc + wdw_ref[k:k + 1, lanes] * shift_ref[s, rows, :]
            z_ref[pl.ds(r0, row_chunk), lanes] = acc
            return 0

        lax.fori_loop(0, tm // row_chunk, rows_step, 0)

    z = _layer_norm(z_ref[...], ng_ref[...], nb_ref[...])
    y = _bdot(_silu(z).astype(BF16), w2_ref[...]) + b2_ref[...]
    v = alpha * x_ref[0] + g1_ref[0, 0] * y
    xo_ref[0] = _layer_norm(v, lng_ref[...], lnb_ref[...])


def _conformer_tail(z, w_dw, b_dw, n_g, n_b, w2_bf16, b2, x, mod, ln_g, ln_b, alpha):
    nb, s, d = z.shape
    tm = _tile(TM_CONV, s)
    n_t = s // tm
    hb = tm // BF16_ROWS
    n_hb = s // BF16_ROWS
    lane_chunk = min(256, d)
    n_taps = w_dw.shape[0]
    last_tap_row = BF16_ROWS + (n_taps - 1) // 2
    n_shift = tm + last_tap_row // SUBLANES * SUBLANES
    assert n_shift + SUBLANES - 1 <= tm + 2 * BF16_ROWS and (n_taps - 1) // 2 <= BF16_ROWS
    kern = functools.partial(_conformer_tail_kernel, n_t=n_t, tm=tm, alpha=alpha,
                             lane_chunk=lane_chunk, row_chunk=min(32, tm))
    vec = lambda a: a.reshape(1, d)
    vspec = pl.BlockSpec((1, d), lambda b, i: (0, 0))
    return pl.pallas_call(
        kern,
        grid=(nb, n_t),
        in_specs=[
            pl.BlockSpec((1, tm, d), lambda b, i: (b, i, 0)),
            pl.BlockSpec((1, BF16_ROWS, d), lambda b, i: (b, jnp.maximum(i * hb - 1, 0), 0)),
            pl.BlockSpec((1, BF16_ROWS, d), lambda b, i: (b, jnp.minimum((i + 1) * hb, n_hb - 1), 0)),
            pl.BlockSpec((w_dw.shape[0], d), lambda b, i: (0, 0)),
            vspec, vspec, vspec,
            pl.BlockSpec((d, d), lambda b, i: (0, 0)),
            vspec,
            pl.BlockSpec((1, tm, d), lambda b, i: (b, i, 0)),
            pl.BlockSpec((1, 1, 1, d), lambda b, i: (2, b, 0, 0)),
            vspec, vspec,
        ],
        out_specs=pl.BlockSpec((1, tm, d), lambda b, i: (b, i, 0)),
        out_shape=jax.ShapeDtypeStruct((nb, s, d), F32),
        scratch_shapes=[pltpu.VMEM((tm + 2 * BF16_ROWS, d), F32), pltpu.VMEM((tm, d), F32),
                        pltpu.VMEM((SUBLANES, n_shift, lane_chunk), F32)],
        compiler_params=_cparams(2),
        name="conformer_tail",
    )(z, z, z, w_dw, vec(b_dw), vec(n_g), vec(n_b), w2_bf16, vec(b2), x, mod, vec(ln_g), vec(ln_b))


def _router_kernel(x_ref, sh_ref, sc_ref, whi_ref, wlo_ref, bc_ref,
                   idx_ref, wt_ref, rank_ref, cnt_ref, carry_ref, *, n_exp, tm):
    i = pl.program_id(0)

    @pl.when(i == 0)
    def _():
        carry_ref[...] = jnp.zeros_like(carry_ref)

    tok = x_ref[...] * (1.0 + sc_ref[0, 0]) + sh_ref[0, 0]
    t_hi = tok.astype(BF16)
    t_lo = (tok - t_hi.astype(F32)).astype(BF16)
    nt = (((1,), (1,)), ((), ()))
    logits = (lax.dot_general(whi_ref[...], t_hi, nt, preferred_element_type=F32)
              + lax.dot_general(whi_ref[...], t_lo, nt, preferred_element_type=F32)
              + lax.dot_general(wlo_ref[...], t_hi, nt, preferred_element_type=F32))
    scores = _sigmoid(logits)
    biased = scores + bc_ref[...]

    per_group = n_exp // N_GROUPS
    sub = lax.broadcasted_iota(jnp.int32, (per_group, tm), 0)
    neg = -jnp.inf
    grp = [biased[g * per_group:(g + 1) * per_group, :] for g in range(N_GROUPS)]
    sc_g = [scores[g * per_group:(g + 1) * per_group, :] for g in range(N_GROUPS)]

    gscore = []
    for g in range(N_GROUPS):
        m1 = jnp.max(grp[g], axis=0, keepdims=True)
        first = jnp.min(jnp.where(grp[g] == m1, sub, per_group), axis=0, keepdims=True)
        m2 = jnp.max(jnp.where(sub == first, neg, grp[g]), axis=0, keepdims=True)
        gscore.append(m1 + m2)
    masked = []
    for g in range(N_GROUPS):
        beaten = jnp.zeros((1, tm), jnp.int32)
        for o in range(N_GROUPS):
            if o == g:
                continue
            wins = (gscore[o] > gscore[g]) | ((gscore[o] == gscore[g]) & (o < g))
            beaten = beaten + wins.astype(jnp.int32)
        masked.append(jnp.where(beaten < TOPK_GROUPS, grp[g], neg))

    eid = [sub + g * per_group for g in range(N_GROUPS)]
    member = [jnp.zeros((per_group, tm), F32) for _ in range(N_GROUPS)]
    idx_rows, score_rows = [], []
    for _ in range(TOP_K):
        m = masked[0]
        for g in range(1, N_GROUPS):
            m = jnp.maximum(m, masked[g])
        m = jnp.max(m, axis=0, keepdims=True)
        cand = jnp.where(masked[0] == m, eid[0], n_exp)
        for g in range(1, N_GROUPS):
            cand = jnp.minimum(cand, jnp.where(masked[g] == m, eid[g], n_exp))
        first = jnp.min(cand, axis=0, keepdims=True)
        picked = jnp.zeros((per_group, tm), F32)
        for g in range(N_GROUPS):
            sel = eid[g] == first
            picked = picked + jnp.where(sel, sc_g[g], 0.0)
            member[g] = jnp.where(sel, 1.0, member[g])
            masked[g] = jnp.where(sel, neg, masked[g])
        idx_rows.append(first)
        score_rows.append(jnp.sum(picked, axis=0, keepdims=True))

    total = score_rows[0]
    for k in range(1, TOP_K):
        total = total + score_rows[k]

    memb = jnp.concatenate(member, axis=0)
    earlier = jnp.where(lax.broadcasted_iota(jnp.int32, (tm, tm), 0)
                        < lax.broadcasted_iota(jnp.int32, (tm, tm), 1), 1.0, 0.0).astype(BF16)
    pos = _bdot(memb.astype(BF16), earlier) + carry_ref[...]
    pos_g = [pos[g * per_group:(g + 1) * per_group, :] for g in range(N_GROUPS)]
    for k in range(TOP_K):
        acc = jnp.zeros((per_group, tm), F32)
        for g in range(N_GROUPS):
            acc = acc + jnp.where(eid[g] == idx_rows[k], pos_g[g], 0.0)
        rank_ref[k:k + 1, :] = jnp.sum(acc, axis=0, keepdims=True).astype(jnp.int32)
        idx_ref[k:k + 1, :] = idx_rows[k]
        wt_ref[k:k + 1, :] = score_rows[k] / total * ROUTED_SCALE
    carry_ref[...] += jnp.sum(memb, axis=1, keepdims=True)
    cnt_ref[...] = carry_ref[...].astype(jnp.int32)


def _route(x, mod, w_router, b_corr):
    nb, s, d = x.shape
    t = nb * s
    n_exp = w_router.shape[1]
    tm = _tile(TM_ROUTE, s)
    per_b = s // tm
    w_t = w_router.T
    w_hi = w_t.astype(BF16)
    w_lo = (w_t - w_hi.astype(F32)).astype(BF16)
    kern = functools.partial(_router_kernel, n_exp=n_exp, tm=tm)
    out_spec = pl.BlockSpec((TOP_K, tm), lambda i: (0, i))
    return pl.pallas_call(
        kern,
        grid=(t // tm,),
        in_specs=[
            pl.BlockSpec((tm, d), lambda i: (i, 0)),
            pl.BlockSpec((1, 1, 1, d), lambda i: (3, i // per_b, 0, 0)),
            pl.BlockSpec((1, 1, 1, d), lambda i: (4, i // per_b, 0, 0)),
            pl.BlockSpec((n_exp, d), lambda i: (0, 0)),
            pl.BlockSpec((n_exp, d), lambda i: (0, 0)),
            pl.BlockSpec((n_exp, 1), lambda i: (0, 0)),
        ],
        out_specs=[out_spec, out_spec, out_spec, pl.BlockSpec((n_exp, 1), lambda i: (0, 0))],
        out_shape=[
            jax.ShapeDtypeStruct((TOP_K, t), jnp.int32),
            jax.ShapeDtypeStruct((TOP_K, t), F32),
            jax.ShapeDtypeStruct((TOP_K, t), jnp.int32),
            jax.ShapeDtypeStruct((n_exp, 1), jnp.int32),
        ],
        scratch_shapes=[pltpu.VMEM((n_exp, 1), F32)],
        compiler_params=_cparams(1),
        name="moe_router",
    )(x.reshape(t, d), mod, mod, w_hi, w_lo, b_corr.reshape(n_exp, 1).astype(F32))


def _slots_kernel(ps_ref, idx_ref, rank_ref, dest_ref, *, n_exp):
    idx = idx_ref[...]

    def add_start(e, acc):
        return acc + jnp.where(idx == e, ps_ref[e], 0)

    dest_ref[...] = lax.fori_loop(0, n_exp, add_start, rank_ref[...])


def _slots(idx, rank, pad_starts):
    k, t = idx.shape
    tm = _tile(2048, t)
    grid_spec = pltpu.PrefetchScalarGridSpec(
        num_scalar_prefetch=1,
        grid=(t // tm,),
        in_specs=[pl.BlockSpec((k, tm), lambda i, ps: (0, i)), pl.BlockSpec((k, tm), lambda i, ps: (0, i))],
        out_specs=pl.BlockSpec((k, tm), lambda i, ps: (0, i)),
    )
    return pl.pallas_call(
        functools.partial(_slots_kernel, n_exp=pad_starts.shape[0]),
        grid_spec=grid_spec,
        out_shape=jax.ShapeDtypeStruct((k, t), jnp.int32),
        compiler_params=_cparams(1),
        name="moe_slots",
    )(pad_starts, idx, rank)


def _to_slabs(value):
    n_seg = value.shape[1] // LANES
    by_seg = jnp.stack([value[:, s * LANES:(s + 1) * LANES] for s in range(n_seg)], axis=0)
    return jnp.swapaxes(by_seg, 0, 1)


def _from_slabs(slabs):
    by_seg = jnp.swapaxes(slabs, 0, 1)
    return jnp.concatenate([by_seg[s] for s in range(slabs.shape[1])], axis=1)


def _dispatch_kernel(pend_ref, padded_ref, x_ref, sh_ref, sc_ref, dest_ref, xs_ref,
                     tok_ref, zero_ref, sem, zsem, *, n_exp, tm, bm):
    i = pl.program_id(0)

    def zero_copy(e):
        return pltpu.make_async_copy(zero_ref, xs_ref.at[pl.ds(pend_ref[e] - bm, bm)], zsem)

    @pl.when(i == 0)
    def _():
        zero_ref[...] = jnp.zeros_like(zero_ref)

        def start(e, _):
            @pl.when(padded_ref[e] > 0)
            def _():
                zero_copy(e).start()
            return 0

        def wait(e, _):
            @pl.when(padded_ref[e] > 0)
            def _():
                zero_copy(e).wait()
            return 0

        lax.fori_loop(0, n_exp, start, 0)
        lax.fori_loop(0, n_exp, wait, 0)

    tok_ref[...] = _to_slabs(x_ref[...] * (1.0 + sc_ref[0, 0]) + sh_ref[0, 0])

    def issue(r, _):
        for k in range(TOP_K):
            pltpu.make_async_copy(tok_ref.at[r], xs_ref.at[dest_ref[k, r]], sem).start(priority=k % 2)
        return 0

    lax.fori_loop(0, tm, issue, 0)
    for k in range(TOP_K):
        pltpu.make_async_copy(tok_ref, xs_ref.at[pl.ds(0, tm)], sem).wait()


def _dispatch(x, mod, dest, pad_ends, padded, n_slots, bm):
    nb, s, d = x.shape
    t = nb * s
    tm = _tile(TM_DISPATCH, s)
    per_b = s // tm
    n_exp = pad_ends.shape[0]
    n_seg = d // LANES
    assert n_seg % SUBLANES == 0, "token slabs must be whole (8, 128) tiles"
    kern = functools.partial(_dispatch_kernel, n_exp=n_exp, tm=tm, bm=bm)
    grid_spec = pltpu.PrefetchScalarGridSpec(
        num_scalar_prefetch=2,
        grid=(t // tm,),
        in_specs=[
            pl.BlockSpec((tm, d), lambda i, pe, pd: (i, 0)),
            pl.BlockSpec((1, 1, 1, d), lambda i, pe, pd: (3, i // per_b, 0, 0)),
            pl.BlockSpec((1, 1, 1, d), lambda i, pe, pd: (4, i // per_b, 0, 0)),
            pl.BlockSpec((TOP_K, tm), lambda i, pe, pd: (0, i), memory_space=pltpu.SMEM),
        ],
        out_specs=pl.BlockSpec(memory_space=pl.ANY),
        scratch_shapes=[
            pltpu.VMEM((tm, n_seg, LANES), F32),
            pltpu.VMEM((bm, n_seg, LANES), F32),
            pltpu.SemaphoreType.DMA(()),
            pltpu.SemaphoreType.DMA(()),
        ],
    )
    return pl.pallas_call(
        kern,
        grid_spec=grid_spec,
        out_shape=jax.ShapeDtypeStruct((n_slots, n_seg, LANES), F32),
        compiler_params=_cparams(1),
        name="moe_dispatch",
    )(pad_ends, padded, x.reshape(t, d), mod, mod, dest)


def _cast_kernel(*refs):
    n = len(refs) // 2
    for src, dst in zip(refs[:n], refs[n:]):
        dst[0] = src[0, 0].astype(dst.dtype)


def _layer_weights_bf16(layer, *stacks):
    n_exp = stacks[0].shape[1]
    in_specs = [pl.BlockSpec((1, 1) + w.shape[2:], lambda e: (layer, e, 0, 0)) for w in stacks]
    out_specs = [pl.BlockSpec((1,) + w.shape[2:], lambda e: (e, 0, 0)) for w in stacks]
    return pl.pallas_call(
        _cast_kernel,
        grid=(n_exp,),
        in_specs=in_specs,
        out_specs=out_specs,
        out_shape=[jax.ShapeDtypeStruct(w.shape[1:], BF16) for w in stacks],
        compiler_params=_cparams(1),
        name="expert_weights_bf16",
    )(*stacks)


def _experts_kernel(be_ref, nu_ref, xs_ref, wg_ref, wu_ref, wd_ref, ys_ref):
    b = pl.program_id(0)

    @pl.when(b < nu_ref[0])
    def _():
        x = _from_slabs(xs_ref[...]).astype(BF16)
        g = _bdot(x, wg_ref[0])
        u = _bdot(x, wu_ref[0])
        ys_ref[...] = _to_slabs(_bdot((_silu(g) * u).astype(BF16), wd_ref[0]))

    @pl.when(b >= nu_ref[0])
    def _():
        ys_ref[...] = jnp.zeros_like(ys_ref)


def _grouped_experts(xs, blk_e, n_used, w_gate, w_up, w_down, bm):
    d, f = w_gate.shape[1:]
    n_seg = d // LANES
    n_blk = xs.shape[0] // bm

    def blk(b, be, nu):
        return jnp.minimum(b, nu[0] - 1)

    grid_spec = pltpu.PrefetchScalarGridSpec(
        num_scalar_prefetch=2,
        grid=(n_blk,),
        in_specs=[
            pl.BlockSpec((bm, n_seg, LANES), lambda b, be, nu: (blk(b, be, nu), 0, 0)),
            pl.BlockSpec((1, d, f), lambda b, be, nu: (be[blk(b, be, nu)], 0, 0)),
            pl.BlockSpec((1, d, f), lambda b, be, nu: (be[blk(b, be, nu)], 0, 0)),
            pl.BlockSpec((1, f, d), lambda b, be, nu: (be[blk(b, be, nu)], 0, 0)),
        ],
        out_specs=pl.BlockSpec((bm, n_seg, LANES), lambda b, be, nu: (jnp.where(b < nu[0], b, n_blk - 1), 0, 0)),
    )
    return pl.pallas_call(
        _experts_kernel,
        grid_spec=grid_spec,
        out_shape=jax.ShapeDtypeStruct(xs.shape, F32),
        compiler_params=_cparams(1),
        name="moe_experts",
    )(blk_e, n_used, xs, w_gate, w_up, w_down)


def _combine_kernel(x_ref, sh_ref, sc_ref, gt_ref, dest_ref, dest_next_ref, wt_ref, ys_ref,
                    wsg_ref, wsu_ref, wsd_ref, lng_ref, lnb_ref, xo_ref, rows_ref, sem, *, tm, alpha):
    i = pl.program_id(0)
    n_steps = pl.num_programs(0)
    slot = i % 2

    def gather(d_ref, buf):
        def issue(r, _):
            for k in range(TOP_K):
                pltpu.make_async_copy(ys_ref.at[d_ref[k, r]], rows_ref.at[buf, k, r],
                                      sem.at[buf]).start(priority=k % 2)
            return 0
        lax.fori_loop(0, tm, issue, 0)

    @pl.when(i == 0)
    def _():
        gather(dest_ref, 0)

    @pl.when(i + 1 < n_steps)
    def _():
        gather(dest_next_ref, 1 - slot)

    x = x_ref[...]
    tok = (x * (1.0 + sc_ref[0, 0]) + sh_ref[0, 0]).astype(BF16)
    hid = (_silu(_bdot(tok, wsg_ref[...])) * _bdot(tok, wsu_ref[...])).astype(BF16)
    f = _bdot(hid, wsd_ref[...])

    for k in range(TOP_K):
        pltpu.make_async_copy(ys_ref.at[pl.ds(0, tm)], rows_ref.at[slot, k], sem.at[slot]).wait()
    wt = wt_ref[...]
    for k in range(TOP_K):
        f = f + wt[:, k:k + 1] * _from_slabs(rows_ref[slot, k])
    v = alpha * x + gt_ref[0, 0] * f
    xo_ref[...] = _layer_norm(v, lng_ref[...], lnb_ref[...])


def _combine(x, mod, dest, wt_rows, ys, ws_gate, ws_up, ws_down, ln_g, ln_b, alpha):
    nb, s, d = x.shape
    t = nb * s
    tm = _tile(TM_COMBINE, s)
    per_b = s // tm
    f = ws_gate.shape[-1]
    n_seg = d // LANES
    kern = functools.partial(_combine_kernel, tm=tm, alpha=alpha)
    vspec = pl.BlockSpec((1, d), lambda i: (0, 0))
    n_steps = t // tm
    out = pl.pallas_call(
        kern,
        grid=(n_steps,),
        in_specs=[
            pl.BlockSpec((tm, d), lambda i: (i, 0)),
            pl.BlockSpec((1, 1, 1, d), lambda i: (3, i // per_b, 0, 0)),
            pl.BlockSpec((1, 1, 1, d), lambda i: (4, i // per_b, 0, 0)),
            pl.BlockSpec((1, 1, 1, d), lambda i: (5, i // per_b, 0, 0)),
            pl.BlockSpec((TOP_K, tm), lambda i: (0, i), memory_space=pltpu.SMEM),
            pl.BlockSpec((TOP_K, tm), lambda i: (0, jnp.minimum(i + 1, n_steps - 1)), memory_space=pltpu.SMEM),
            pl.BlockSpec((tm, TOP_K), lambda i: (i, 0)),
            pl.BlockSpec(memory_space=pl.ANY),
            pl.BlockSpec((d, f), lambda i: (0, 0)),
            pl.BlockSpec((d, f), lambda i: (0, 0)),
            pl.BlockSpec((f, d), lambda i: (0, 0)),
            vspec, vspec,
        ],
        out_specs=pl.BlockSpec((tm, d), lambda i: (i, 0)),
        out_shape=jax.ShapeDtypeStruct((t, d), F32),
        scratch_shapes=[pltpu.VMEM((2, TOP_K, tm, n_seg, LANES), F32), pltpu.SemaphoreType.DMA((2,))],
        compiler_params=_cparams(1),
        name="moe_combine",
    )(x.reshape(t, d), mod, mod, mod, dest, dest, wt_rows, ys, ws_gate, ws_up, ws_down,
      ln_g.reshape(1, d), ln_b.reshape(1, d))
    return out.reshape(nb, s, d)


def _moe_block(x, mod, w_router, b_router, w_gate, w_up, w_down, ws_gate, ws_up, ws_down, ln_g, ln_b, alpha):
    nb, s, d = x.shape
    t = nb * s
    n_exp = w_router.shape[1]
    bm = _tile(MOE_BLOCK, t * TOP_K)
    idx, wt, rank, counts = _route(x, mod, w_router, b_router)

    counts = counts.reshape(n_exp)
    padded = (counts + bm - 1) // bm * bm
    pad_ends = jnp.cumsum(padded).astype(jnp.int32)
    pad_starts = pad_ends - padded
    dest = _slots(idx, rank, pad_starts.astype(jnp.int32))
    n_blk = t * TOP_K // bm + n_exp
    n_used = (pad_ends[-1:] // bm).astype(jnp.int32)
    blk_start = jnp.arange(n_blk, dtype=jnp.int32) * bm
    blk_e = jnp.sum((pad_ends[None, :] <= blk_start[:, None]).astype(jnp.int32), axis=1)
    blk_e = jnp.minimum(blk_e, n_exp - 1).astype(jnp.int32)

    xs = _dispatch(x, mod, dest, pad_ends, padded.astype(jnp.int32), n_blk * bm, bm)
    ys = _grouped_experts(xs, blk_e, n_used, w_gate, w_up, w_down, bm)
    return _combine(x, mod, dest, wt.T, ys, ws_gate.astype(BF16), ws_up.astype(BF16), ws_down.astype(BF16),
                    ln_g, ln_b, alpha)


def kernel(x, c, ctx, c_ctx, w_mod, b_mod, ln1_g, ln1_b, ln2_g, ln2_b, rg_w_in, rg_b_in, rg_conv_w, rg_conv_b, rg_w_gates, rg_b_gates, rg_lambda, rg_w_out, rg_b_out, cf_w_pw1, cf_b_pw1, cf_w_dw, cf_b_dw, cf_norm_g, cf_norm_b, cf_w_pw2, cf_b_pw2, moe_w_router, moe_b_router, moe_w_gate, moe_w_up, moe_w_down, sh_w_gate, sh_w_up, sh_w_down):
    nb, s, d = x.shape
    depth = w_mod.shape[0]
    assert depth == 2 and nb + 1 <= SUBLANES
    alpha = (2 * depth) ** 0.25

    c_rows = jnp.zeros((SUBLANES, d), F32).at[:nb].set(c).at[nb].set(c_ctx)
    mod_all = _adaln_vectors(c_rows, w_mod, b_mod)
    mod_all = mod_all.reshape(depth, SUBLANES, 6, 1, d).transpose(0, 2, 1, 3, 4)
    mod_lat = [mod_all[l, :, :nb] for l in range(depth)]
    mod_ctx = jnp.broadcast_to(mod_all[0, :, nb:nb + 1], (6, nb, 1, d))

    n_h, hd = rg_w_gates.shape[3], rg_w_gates.shape[4]
    r = n_h * hd
    w_in = rg_w_in[0].astype(BF16)
    wg = [jnp.concatenate([rg_w_gates[0, dr, 0], rg_w_gates[0, dr, 1]], axis=-1).astype(BF16) for dr in range(2)]
    bg = [jnp.concatenate([rg_b_gates[0, dr, 0], rg_b_gates[0, dr, 1]], axis=-1).reshape(n_h, 1, 2 * hd)
          for dr in range(2)]
    conv_w, conv_b = rg_conv_w[0], rg_conv_b[0]
    zero_state = jnp.zeros((nb, n_h, 1, hd), F32)

    proj_c = _modulated_projection(ctx, mod_ctx, w_in, rg_b_in[0], BF16)
    _, hcf = _rglru_scan(proj_c, conv_w, conv_b, wg[0], bg[0], rg_lambda[0, 0], zero_state, reverse=False)
    _, hcb = _rglru_scan(proj_c, conv_w, conv_b, wg[1], bg[1], rg_lambda[0, 1], zero_state, reverse=True)

    proj_x = _modulated_projection(x, mod_lat[0], w_in, rg_b_in[0], BF16)
    hxf, _ = _rglru_scan(proj_x, conv_w, conv_b, wg[0], bg[0], rg_lambda[0, 0], hcf, reverse=False)
    readout = (hxf, rg_w_out[0].astype(BF16), rg_b_out[0], x, mod_lat[0], ln1_g[0], ln1_b[0])
    x, _ = _rglru_scan(proj_x, conv_w, conv_b, wg[1], bg[1], rg_lambda[0, 1], hcb, reverse=True,
                       readout=readout, alpha=alpha)
    x = _moe_block(x, mod_lat[0], moe_w_router[0], moe_b_router[0],
                   *_layer_weights_bf16(0, moe_w_gate, moe_w_up, moe_w_down),
                   sh_w_gate[0], sh_w_up[0], sh_w_down[0], ln2_g[0], ln2_b[0], alpha)

    z = _modulated_glu(x, mod_lat[1], cf_w_pw1[0].astype(BF16), cf_b_pw1[0])
    x = _conformer_tail(z, cf_w_dw[0], cf_b_dw[0], cf_norm_g[0], cf_norm_b[0], cf_w_pw2[0].astype(BF16),
                        cf_b_pw2[0], x, mod_lat[1], ln1_g[1], ln1_b[1], alpha)
    x = _moe_block(x, mod_lat[1], moe_w_router[1], moe_b_router[1],
                   *_layer_weights_bf16(1, moe_w_gate, moe_w_up, moe_w_down),
                   sh_w_gate[1], sh_w_up[1], sh_w_down[1], ln2_g[1], ln2_b[1], alpha)
    return x
```

```python
import functools

import jax
import jax.numpy as jnp
from jax import lax
from jax.experimental import pallas as pl
from jax.experimental.pallas import tpu as pltpu

F32 = jnp.float32
BF16 = jnp.bfloat16

RG_C = 8.0
N_GROUPS = 8
TOPK_GROUPS = 4
TOP_K = 8
ROUTED_SCALE = 2.5
LN_EPS = 1e-5

VMEM_LIMIT_BYTES = 56 * 1024 * 1024
SUBLANES = 8
LANES = 128
BF16_ROWS = 16

TM_PROJ = 512
TM_SCAN = 512
TM_CONV = 256
TM_ROUTE = 512
TM_DISPATCH = 256
TM_COMBINE = 128
MOE_BLOCK = 512


def _tile(default, n):
    t = min(default, n)
    assert n % t == 0, (default, n)
    return t


def _cparams(n_axes):
    return pltpu.CompilerParams(
        dimension_semantics=("arbitrary",) * n_axes,
        vmem_limit_bytes=VMEM_LIMIT_BYTES,
    )


def _sigmoid(v):
    return 0.5 * jnp.tanh(0.5 * v) + 0.5


def _sqrt_nonneg(v):
    return jnp.where(v > 0.0, v * lax.rsqrt(v), 0.0)


def _silu(v):
    return v * _sigmoid(v)


def _gelu_tanh(v):
    c = 0.7978845608028654
    return 0.5 * v * (1.0 + jnp.tanh(c * (v + 0.044715 * (v * v * v))))


def _softplus(v):
    return jnp.maximum(v, 0.0) + jnp.log(1.0 + jnp.exp(-jnp.abs(v)))


def _layer_norm(v, g, b):
    mu = jnp.mean(v, axis=-1, keepdims=True)
    d = v - mu
    var = jnp.mean(d * d, axis=-1, keepdims=True)
    return d * lax.rsqrt(var + LN_EPS) * g + b


def _bdot(a, b):
    return jnp.dot(a, b, preferred_element_type=F32)


def _mod_kernel(c_ref, w_ref, b_ref, o_ref):
    s = _silu(c_ref[...])
    o_ref[0] = _bdot(s.astype(BF16), w_ref[0].astype(BF16)) + b_ref[0]


def _adaln_vectors(c_rows, w_mod, b_mod):
    depth, d, d6 = w_mod.shape
    tn = _tile(1024, d6)
    return pl.pallas_call(
        _mod_kernel,
        grid=(depth, d6 // tn),
        in_specs=[
            pl.BlockSpec((SUBLANES, d), lambda l, n: (0, 0)),
            pl.BlockSpec((1, d, tn), lambda l, n: (l, 0, n)),
            pl.BlockSpec((1, 1, tn), lambda l, n: (l, 0, n)),
        ],
        out_specs=pl.BlockSpec((1, SUBLANES, tn), lambda l, n: (l, 0, n)),
        out_shape=jax.ShapeDtypeStruct((depth, SUBLANES, d6), F32),
        compiler_params=_cparams(2),
        name="adaln_vectors",
    )(c_rows, w_mod, b_mod.reshape(depth, 1, d6))


def _mod_spec(chunk, batch_of):
    def index(*ids):
        return (chunk, batch_of(*ids), 0, 0)
    return index


def _proj_kernel(x_ref, sh_ref, sc_ref, w_ref, b_ref, o_ref):
    h = x_ref[0] * (1.0 + sc_ref[0, 0]) + sh_ref[0, 0]
    o_ref[0] = (_bdot(h.astype(BF16), w_ref[...]) + b_ref[...]).astype(o_ref.dtype)


def _modulated_projection(x, mod, w_bf16, bias, out_dtype):
    nb, s, d = x.shape
    n = w_bf16.shape[1]
    tm = _tile(TM_PROJ, s)
    tn = _tile(1280, n)
    dm = mod.shape[-1]
    return pl.pallas_call(
        _proj_kernel,
        grid=(n // tn, nb, s // tm),
        in_specs=[
            pl.BlockSpec((1, tm, d), lambda j, b, i: (b, i, 0)),
            pl.BlockSpec((1, 1, 1, dm), lambda j, b, i: (0, b, 0, 0)),
            pl.BlockSpec((1, 1, 1, dm), lambda j, b, i: (1, b, 0, 0)),
            pl.BlockSpec((d, tn), lambda j, b, i: (0, j)),
            pl.BlockSpec((1, tn), lambda j, b, i: (0, j)),
        ],
        out_specs=pl.BlockSpec((1, tm, tn), lambda j, b, i: (b, i, j)),
        out_shape=jax.ShapeDtypeStruct((nb, s, n), out_dtype),
        compiler_params=_cparams(3),
        name="modulated_projection",
    )(x, mod, mod, w_bf16, bias.reshape(1, n))


def _rglru_kernel(*refs, reverse, readout, n_t, tm, alpha):
    if readout:
        (cur_ref, prev_ref, next_ref, perm_ref, cw_ref, cb_ref, wg_ref, bg_ref, lam_ref, h0_ref,
         hf_ref, gate_ref, wout_ref, bout_ref, x_ref, g1_ref, lng_ref, lnb_ref,
         xo_ref, hlast_ref,
         win_ref, a_ref, hl_ref, carry_ref, acc_ref) = refs
    else:
        (cur_ref, prev_ref, next_ref, perm_ref, cw_ref, cb_ref, wg_ref, bg_ref, lam_ref, h0_ref,
         ho_ref, hlast_ref,
         win_ref, a_ref, hl_ref, carry_ref) = refs
    i = pl.program_id(1)
    h = pl.program_id(2)
    n_h = pl.num_programs(2)
    ti = (n_t - 1 - i) if reverse else i
    hd = cur_ref.shape[-1]
    chunk = tm // SUBLANES
    sub = lax.broadcasted_iota(jnp.int32, (SUBLANES, hd), 0)

    rnn = _bdot(perm_ref[...], cur_ref[0])
    prev_last = jnp.where(ti == 0, 0.0, prev_ref[0].astype(F32)[BF16_ROWS - 1:, :])
    nxt = jnp.where(ti == n_t - 1, 0.0, next_ref[0].astype(F32)[:2, :])
    win_ref[0:SUBLANES, :] = jnp.where(
        sub == 0, prev_last, pltpu.roll(rnn[tm - SUBLANES:, :], 1, axis=0))
    win_ref[SUBLANES:SUBLANES + tm, :] = rnn
    for extra in range(2):
        lo = SUBLANES + tm + extra * SUBLANES
        win_ref[lo:lo + SUBLANES, :] = jnp.where(
            sub == SUBLANES - 1, nxt[extra:extra + 1, :],
            pltpu.roll(rnn[extra * SUBLANES:(extra + 1) * SUBLANES, :], SUBLANES - 1, axis=0))
    n_taps = cw_ref.shape[0]
    xb = cb_ref[...] + cw_ref[0:1, :] * win_ref[pl.ds(0, tm), :]
    for k in range(1, n_taps):
        xb = xb + cw_ref[k:k + 1, :] * win_ref[pl.ds(k * SUBLANES, tm), :]

    pre = _bdot(xb.astype(BF16), wg_ref[0]) + bg_ref[0]
    r = _sigmoid(pre[:, :hd])
    gi = _sigmoid(pre[:, hd:])
    log_a = (-RG_C) * r * _softplus(-lam_ref[...])
    a = jnp.exp(log_a)
    a_ref[...] = a
    hl_ref[...] = _sqrt_nonneg(1.0 - a * a) * gi * xb

    if readout:
        gate_act = _gelu_tanh(_bdot(perm_ref[...], gate_ref[0]))

    def scan_step(jj, carry):
        a_cum, h_loc = carry
        j = (chunk - 1 - jj) if reverse else jj
        rows = pl.ds(pl.multiple_of(j * SUBLANES, SUBLANES), SUBLANES)
        aj = a_ref[rows, :]
        h_loc = aj * h_loc + hl_ref[rows, :]
        a_cum = a_cum * aj
        hl_ref[rows, :] = h_loc
        a_ref[rows, :] = a_cum
        return a_cum, h_loc

    a_end, h_end = lax.fori_loop(
        0, chunk, scan_step, (jnp.ones((SUBLANES, hd), F32), jnp.zeros((SUBLANES, hd), F32)),
        unroll=SUBLANES)

    @pl.when(i == 0)
    def _():
        carry_ref[h] = h0_ref[0, h]

    state = carry_ref[h]
    entry = jnp.zeros((SUBLANES, hd), F32)
    for s in (range(SUBLANES - 1, -1, -1) if reverse else range(SUBLANES)):
        entry = jnp.where(sub == s, state, entry)
        state = a_end[s:s + 1, :] * state + h_end[s:s + 1, :]
    carry_ref[h] = state
    hlast_ref[0, h] = state

    h_full = hl_ref[...] + a_ref[...] * jnp.concatenate([entry] * chunk, axis=0)
    if not readout:
        ho_ref[0] = h_full.astype(ho_ref.dtype)
    else:
        h_sum = h_full + hf_ref[0].astype(F32)
        contrib = _bdot((h_sum * gate_act).astype(BF16), wout_ref[...])
        n_lc = acc_ref.shape[0]

        @pl.when(h == 0)
        def _():
            for c in range(n_lc):
                acc_ref[c] = contrib[:, c * LANES:(c + 1) * LANES]

        @pl.when(h != 0)
        def _():
            for c in range(n_lc):
                acc_ref[c] += contrib[:, c * LANES:(c + 1) * LANES]

        @pl.when(h == n_h - 1)
        def _():
            y = jnp.concatenate(
                [jnp.concatenate(
                    [acc_ref[c, pl.ds(q * SUBLANES * SUBLANES + s, SUBLANES, stride=SUBLANES), :]
                     for s in range(SUBLANES) for q in range(chunk // SUBLANES)], axis=0)
                 for c in range(n_lc)], axis=1) + bout_ref[...]
            v = alpha * x_ref[0] + g1_ref[0, 0] * y
            xo_ref[0] = _layer_norm(v, lng_ref[...], lnb_ref[...])


def _rglru_scan(proj, conv_w, conv_b, wg, bg, lam, h0, *, reverse, h_dtype=BF16, readout=None, alpha=None):
    nb, s, r2 = proj.shape
    r = r2 // 2
    n_h, hd = wg.shape[0], wg.shape[1]
    tm = _tile(TM_SCAN, s)
    n_t = s // tm
    hb = tm // BF16_ROWS
    n_hb = s // BF16_ROWS

    def tix(i):
        return (n_t - 1 - i) if reverse else i

    rows = jnp.arange(tm, dtype=jnp.int32)
    time_of_row = (rows % SUBLANES) * (tm // SUBLANES) + rows // SUBLANES
    perm = (time_of_row[:, None] == rows[None, :]).astype(BF16)
    perm_spec = pl.BlockSpec((tm, tm), lambda b, i, h: (0, 0))

    in_specs = [
        pl.BlockSpec((1, tm, hd), lambda b, i, h: (b, tix(i), n_h + h)),
        pl.BlockSpec((1, BF16_ROWS, hd), lambda b, i, h: (b, jnp.maximum(tix(i) * hb - 1, 0), n_h + h)),
        pl.BlockSpec((1, BF16_ROWS, hd), lambda b, i, h: (b, jnp.minimum((tix(i) + 1) * hb, n_hb - 1), n_h + h)),
        perm_spec,
        pl.BlockSpec((conv_w.shape[0], hd), lambda b, i, h: (0, h)),
        pl.BlockSpec((1, hd), lambda b, i, h: (0, h)),
        pl.BlockSpec((1, hd, 2 * hd), lambda b, i, h: (h, 0, 0)),
        pl.BlockSpec((1, 1, 2 * hd), lambda b, i, h: (h, 0, 0)),
        pl.BlockSpec((1, hd), lambda b, i, h: (0, h)),
        pl.BlockSpec((1, n_h, 1, hd), lambda b, i, h: (b, 0, 0, 0)),
    ]
    args = [proj, proj, proj, perm, conv_w, conv_b.reshape(1, r), wg, bg, lam.reshape(1, r), h0]
    assert conv_w.shape[0] == 4 and tm // SUBLANES >= 2, "window below covers taps at time offsets -1 .. +2"
    scratch = [
        pltpu.VMEM((tm + 3 * SUBLANES, hd), F32),
        pltpu.VMEM((tm, hd), F32),
        pltpu.VMEM((tm, hd), F32),
        pltpu.VMEM((n_h, 1, hd), F32),
    ]
    hlast_spec = pl.BlockSpec((1, n_h, 1, hd), lambda b, i, h: (b, 0, 0, 0))
    hlast_shape = jax.ShapeDtypeStruct((nb, n_h, 1, hd), F32)
    if readout is None:
        out_specs = [pl.BlockSpec((1, tm, hd), lambda b, i, h: (b, tix(i), h)), hlast_spec]
        out_shape = [jax.ShapeDtypeStruct((nb, s, r), h_dtype), hlast_shape]
    else:
        h_other, w_out, b_out, x, mod, ln_g, ln_b = readout
        d = x.shape[-1]
        in_specs += [
            pl.BlockSpec((1, tm, hd), lambda b, i, h: (b, tix(i), h)),
            pl.BlockSpec((1, tm, hd), lambda b, i, h: (b, tix(i), h)),
            pl.BlockSpec((hd, d), lambda b, i, h: (h, 0)),
            pl.BlockSpec((1, d), lambda b, i, h: (0, 0)),
            pl.BlockSpec((1, tm, d), lambda b, i, h: (b, tix(i), 0)),
            pl.BlockSpec((1, 1, 1, d), lambda b, i, h: (2, b, 0, 0)),
            pl.BlockSpec((1, d), lambda b, i, h: (0, 0)),
            pl.BlockSpec((1, d), lambda b, i, h: (0, 0)),
        ]
        args += [h_other, proj, w_out, b_out.reshape(1, d), x, mod, ln_g.reshape(1, d), ln_b.reshape(1, d)]
        assert (tm // SUBLANES) % SUBLANES == 0, "readout un-interleaves 8 vreg rows at a time"
        scratch += [pltpu.VMEM((d // LANES, tm, LANES), F32)]
        out_specs = [pl.BlockSpec((1, tm, d), lambda b, i, h: (b, tix(i), 0)), hlast_spec]
        out_shape = [jax.ShapeDtypeStruct((nb, s, d), F32), hlast_shape]
    kern = functools.partial(_rglru_kernel, reverse=reverse, readout=readout is not None,
                             n_t=n_t, tm=tm, alpha=alpha)
    return pl.pallas_call(
        kern,
        grid=(nb, n_t, n_h),
        in_specs=in_specs,
        out_specs=out_specs,
        out_shape=out_shape,
        scratch_shapes=scratch,
        compiler_params=_cparams(3),
        name="rglru_bwd_readout" if readout is not None else ("rglru_bwd" if reverse else "rglru_fwd"),
    )(*args)


def _glu_kernel(x_ref, sh_ref, sc_ref, wv_ref, wg_ref, bv_ref, bgate_ref, o_ref):
    h = (x_ref[0] * (1.0 + sc_ref[0, 0]) + sh_ref[0, 0]).astype(BF16)
    val = _bdot(h, wv_ref[...]) + bv_ref[...]
    gate = _bdot(h, wg_ref[...]) + bgate_ref[...]
    o_ref[0] = (val * _sigmoid(gate)).astype(o_ref.dtype)


def _modulated_glu(x, mod, w_bf16, bias):
    nb, s, d = x.shape
    n = w_bf16.shape[1] // 2
    tm = _tile(TM_PROJ, s)
    tn = _tile(1024, n)
    nj = n // tn
    b2 = bias.reshape(1, 2 * n)
    return pl.pallas_call(
        _glu_kernel,
        grid=(nj, nb, s // tm),
        in_specs=[
            pl.BlockSpec((1, tm, d), lambda j, b, i: (b, i, 0)),
            pl.BlockSpec((1, 1, 1, d), lambda j, b, i: (0, b, 0, 0)),
            pl.BlockSpec((1, 1, 1, d), lambda j, b, i: (1, b, 0, 0)),
            pl.BlockSpec((d, tn), lambda j, b, i: (0, j)),
            pl.BlockSpec((d, tn), lambda j, b, i: (0, nj + j)),
            pl.BlockSpec((1, tn), lambda j, b, i: (0, j)),
            pl.BlockSpec((1, tn), lambda j, b, i: (0, nj + j)),
        ],
        out_specs=pl.BlockSpec((1, tm, tn), lambda j, b, i: (b, i, j)),
        out_shape=jax.ShapeDtypeStruct((nb, s, n), BF16),
        compiler_params=_cparams(3),
        name="modulated_glu",
    )(x, mod, mod, w_bf16, w_bf16, b2, b2)


def _conformer_tail_kernel(cur_ref, prev_ref, next_ref, wdw_ref, bdw_ref, ng_ref, nb_ref, w2_ref, b2_ref,
                           x_ref, g1_ref, lng_ref, lnb_ref, xo_ref, win_ref, z_ref, shift_ref, *, n_t, tm, alpha,
                           lane_chunk, row_chunk):
    i = pl.program_id(1)
    d = cur_ref.shape[-1]
    halo = BF16_ROWS
    win_ref[0:halo, :] = jnp.where(i == 0, 0.0, prev_ref[0].astype(F32))
    win_ref[halo:halo + tm, :] = cur_ref[0].astype(F32)
    win_ref[halo + tm:, :] = jnp.where(i == n_t - 1, 0.0, next_ref[0].astype(F32))
    n_taps = wdw_ref.shape[0]
    base = halo - (n_taps - 1) // 2

    n_shift = shift_ref.shape[1]
    for c in range(d // lane_chunk):
        lanes = slice(c * lane_chunk, (c + 1) * lane_chunk)
        for s in range(SUBLANES):
            shift_ref[s] = win_ref[pl.ds(s, n_shift), lanes]

        def rows_step(rc, _, lanes=lanes):
            r0 = pl.multiple_of(rc * row_chunk, row_chunk)
            acc = jnp.zeros((row_chunk, lane_chunk), F32) + bdw_ref[:, lanes]
            for k in range(n_taps):
                q, s = divmod(base + k, SUBLANES)
                rows = pl.ds(pl.multiple_of(r0 + q * SUBLANES, SUBLANES), row_chunk)
                acc = acc + wdw_ref[k:k + 1, lanes] * shift_ref[s, rows, :]
            z_ref[pl.ds(r0, row_chunk), lanes] = acc
            return 0

        lax.fori_loop(0, tm // row_chunk, rows_step, 0)

    z = _layer_norm(z_ref[...], ng_ref[...], nb_ref[...])
    y = _bdot(_silu(z).astype(BF16), w2_ref[...]) + b2_ref[...]
    v = alpha * x_ref[0] + g1_ref[0, 0] * y
    xo_ref[0] = _layer_norm(v, lng_ref[...], lnb_ref[...])


def _conformer_tail(z, w_dw, b_dw, n_g, n_b, w2_bf16, b2, x, mod, ln_g, ln_b, alpha):
    nb, s, d = z.shape
    tm = _tile(TM_CONV, s)
    n_t = s // tm
    hb = tm // BF16_ROWS
    n_hb = s // BF16_ROWS
    lane_chunk = min(256, d)
    n_taps = w_dw.shape[0]
    last_tap_row = BF16_ROWS + (n_taps - 1) // 2
    n_shift = tm + last_tap_row // SUBLANES * SUBLANES
    assert n_shift + SUBLANES - 1 <= tm + 2 * BF16_ROWS and (n_taps - 1) // 2 <= BF16_ROWS
    kern = functools.partial(_conformer_tail_kernel, n_t=n_t, tm=tm, alpha=alpha,
                             lane_chunk=lane_chunk, row_chunk=min(32, tm))
    vec = lambda a: a.reshape(1, d)
    vspec = pl.BlockSpec((1, d), lambda b, i: (0, 0))
    return pl.pallas_call(
        kern,
        grid=(nb, n_t),
        in_specs=[
            pl.BlockSpec((1, tm, d), lambda b, i: (b, i, 0)),
            pl.BlockSpec((1, BF16_ROWS, d), lambda b, i: (b, jnp.maximum(i * hb - 1, 0), 0)),
            pl.BlockSpec((1, BF16_ROWS, d), lambda b, i: (b, jnp.minimum((i + 1) * hb, n_hb - 1), 0)),
            pl.BlockSpec((w_dw.shape[0], d), lambda b, i: (0, 0)),
            vspec, vspec, vspec,
            pl.BlockSpec((d, d), lambda b, i: (0, 0)),
            vspec,
            pl.BlockSpec((1, tm, d), lambda b, i: (b, i, 0)),
            pl.BlockSpec((1, 1, 1, d), lambda b, i: (2, b, 0, 0)),
            vspec, vspec,
        ],
        out_specs=pl.BlockSpec((1, tm, d), lambda b, i: (b, i, 0)),
        out_shape=jax.ShapeDtypeStruct((nb, s, d), F32),
        scratch_shapes=[pltpu.VMEM((tm + 2 * BF16_ROWS, d), F32), pltpu.VMEM((tm, d), F32),
                        pltpu.VMEM((SUBLANES, n_shift, lane_chunk), F32)],
        compiler_params=_cparams(2),
        name="conformer_tail",
    )(z, z, z, w_dw, vec(b_dw), vec(n_g), vec(n_b), w2_bf16, vec(b2), x, mod, vec(ln_g), vec(ln_b))


def _router_kernel(x_ref, sh_ref, sc_ref, whi_ref, wlo_ref, bc_ref,
                   idx_ref, wt_ref, rank_ref, cnt_ref, carry_ref, *, n_exp, tm):
    i = pl.program_id(0)

    @pl.when(i == 0)
    def _():
        carry_ref[...] = jnp.zeros_like(carry_ref)

    tok = x_ref[...] * (1.0 + sc_ref[0, 0]) + sh_ref[0, 0]
    t_hi = tok.astype(BF16)
    t_lo = (tok - t_hi.astype(F32)).astype(BF16)
    nt = (((1,), (1,)), ((), ()))
    logits = (lax.dot_general(whi_ref[...], t_hi, nt, preferred_element_type=F32)
              + lax.dot_general(whi_ref[...], t_lo, nt, preferred_element_type=F32)
              + lax.dot_general(wlo_ref[...], t_hi, nt, preferred_element_type=F32))
    scores = _sigmoid(logits)
    biased = scores + bc_ref[...]

    per_group = n_exp // N_GROUPS
    sub = lax.broadcasted_iota(jnp.int32, (per_group, tm), 0)
    neg = -jnp.inf
    grp = [biased[g * per_group:(g + 1) * per_group, :] for g in range(N_GROUPS)]
    sc_g = [scores[g * per_group:(g + 1) * per_group, :] for g in range(N_GROUPS)]

    gscore = []
    for g in range(N_GROUPS):
        m1 = jnp.max(grp[g], axis=0, keepdims=True)
        first = jnp.min(jnp.where(grp[g] == m1, sub, per_group), axis=0, keepdims=True)
        m2 = jnp.max(jnp.where(sub == first, neg, grp[g]), axis=0, keepdims=True)
        gscore.append(m1 + m2)
    masked = []
    for g in range(N_GROUPS):
        beaten = jnp.zeros((1, tm), jnp.int32)
        for o in range(N_GROUPS):
            if o == g:
                continue
            wins = (gscore[o] > gscore[g]) | ((gscore[o] == gscore[g]) & (o < g))
            beaten = beaten + wins.astype(jnp.int32)
        masked.append(jnp.where(beaten < TOPK_GROUPS, grp[g], neg))

    eid = [sub + g * per_group for g in range(N_GROUPS)]
    member = [jnp.zeros((per_group, tm), F32) for _ in range(N_GROUPS)]
    idx_rows, score_rows = [], []
    for _ in range(TOP_K):
        m = masked[0]
        for g in range(1, N_GROUPS):
            m = jnp.maximum(m, masked[g])
        m = jnp.max(m, axis=0, keepdims=True)
        cand = jnp.where(masked[0] == m, eid[0], n_exp)
        for g in range(1, N_GROUPS):
            cand = jnp.minimum(cand, jnp.where(masked[g] == m, eid[g], n_exp))
        first = jnp.min(cand, axis=0, keepdims=True)
        picked = jnp.zeros((per_group, tm), F32)
        for g in range(N_GROUPS):
            sel = eid[g] == first
            picked = picked + jnp.where(sel, sc_g[g], 0.0)
            member[g] = jnp.where(sel, 1.0, member[g])
            masked[g] = jnp.where(sel, neg, masked[g])
        idx_rows.append(first)
        score_rows.append(jnp.sum(picked, axis=0, keepdims=True))

    total = score_rows[0]
    for k in range(1, TOP_K):
        total = total + score_rows[k]

    memb = jnp.concatenate(member, axis=0)
    earlier = jnp.where(lax.broadcasted_iota(jnp.int32, (tm, tm), 0)
                        < lax.broadcasted_iota(jnp.int32, (tm, tm), 1), 1.0, 0.0).astype(BF16)
    pos = _bdot(memb.astype(BF16), earlier) + carry_ref[...]
    pos_g = [pos[g * per_group:(g + 1) * per_group, :] for g in range(N_GROUPS)]
    for k in range(TOP_K):
        acc = jnp.zeros((per_group, tm), F32)
        for g in range(N_GROUPS):
            acc = acc + jnp.where(eid[g] == idx_rows[k], pos_g[g], 0.0)
        rank_ref[k:k + 1, :] = jnp.sum(acc, axis=0, keepdims=True).astype(jnp.int32)
        idx_ref[k:k + 1, :] = idx_rows[k]
        wt_ref[k:k + 1, :] = score_rows[k] / total * ROUTED_SCALE
    carry_ref[...] += jnp.sum(memb, axis=1, keepdims=True)
    cnt_ref[...] = carry_ref[...].astype(jnp.int32)


def _route(x, mod, w_router, b_corr):
    nb, s, d = x.shape
    t = nb * s
    n_exp = w_router.shape[1]
    tm = _tile(TM_ROUTE, s)
    per_b = s // tm
    w_t = w_router.T
    w_hi = w_t.astype(BF16)
    w_lo = (w_t - w_hi.astype(F32)).astype(BF16)
    kern = functools.partial(_router_kernel, n_exp=n_exp, tm=tm)
    out_spec = pl.BlockSpec((TOP_K, tm), lambda i: (0, i))
    return pl.pallas_call(
        kern,
        grid=(t // tm,),
        in_specs=[
            pl.BlockSpec((tm, d), lambda i: (i, 0)),
            pl.BlockSpec((1, 1, 1, d), lambda i: (3, i // per_b, 0, 0)),
            pl.BlockSpec((1, 1, 1, d), lambda i: (4, i // per_b, 0, 0)),
            pl.BlockSpec((n_exp, d), lambda i: (0, 0)),
            pl.BlockSpec((n_exp, d), lambda i: (0, 0)),
            pl.BlockSpec((n_exp, 1), lambda i: (0, 0)),
        ],
        out_specs=[out_spec, out_spec, out_spec, pl.BlockSpec((n_exp, 1), lambda i: (0, 0))],
        out_shape=[
            jax.ShapeDtypeStruct((TOP_K, t), jnp.int32),
            jax.ShapeDtypeStruct((TOP_K, t), F32),
            jax.ShapeDtypeStruct((TOP_K, t), jnp.int32),
            jax.ShapeDtypeStruct((n_exp, 1), jnp.int32),
        ],
        scratch_shapes=[pltpu.VMEM((n_exp, 1), F32)],
        compiler_params=_cparams(1),
        name="moe_router",
    )(x.reshape(t, d), mod, mod, w_hi, w_lo, b_corr.reshape(n_exp, 1).astype(F32))


def _slots_kernel(ps_ref, idx_ref, rank_ref, dest_ref, *, n_exp):
    idx = idx_ref[...]

    def add_start(e, acc):
        return acc + jnp.where(idx == e, ps_ref[e], 0)

    dest_ref[...] = lax.fori_loop(0, n_exp, add_start, rank_ref[...])


def _slots(idx, rank, pad_starts):
    k, t = idx.shape
    tm = _tile(2048, t)
    grid_spec = pltpu.PrefetchScalarGridSpec(
        num_scalar_prefetch=1,
        grid=(t // tm,),
        in_specs=[pl.BlockSpec((k, tm), lambda i, ps: (0, i)), pl.BlockSpec((k, tm), lambda i, ps: (0, i))],
        out_specs=pl.BlockSpec((k, tm), lambda i, ps: (0, i)),
    )
    return pl.pallas_call(
        functools.partial(_slots_kernel, n_exp=pad_starts.shape[0]),
        grid_spec=grid_spec,
        out_shape=jax.ShapeDtypeStruct((k, t), jnp.int32),
        compiler_params=_cparams(1),
        name="moe_slots",
    )(pad_starts, idx, rank)


def _to_slabs(value):
    n_seg = value.shape[1] // LANES
    by_seg = jnp.stack([value[:, s * LANES:(s + 1) * LANES] for s in range(n_seg)], axis=0)
    return jnp.swapaxes(by_seg, 0, 1)


def _from_slabs(slabs):
    by_seg = jnp.swapaxes(slabs, 0, 1)
    return jnp.concatenate([by_seg[s] for s in range(slabs.shape[1])], axis=1)


def _dispatch_kernel(pend_ref, padded_ref, x_ref, sh_ref, sc_ref, dest_ref, xs_ref,
                     tok_ref, zero_ref, sem, zsem, *, n_exp, tm, bm):
    i = pl.program_id(0)

    def zero_copy(e):
        return pltpu.make_async_copy(zero_ref, xs_ref.at[pl.ds(pend_ref[e] - bm, bm)], zsem)

    @pl.when(i == 0)
    def _():
        zero_ref[...] = jnp.zeros_like(zero_ref)

        def start(e, _):
            @pl.when(padded_ref[e] > 0)
            def _():
                zero_copy(e).start()
            return 0

        def wait(e, _):
            @pl.when(padded_ref[e] > 0)
            def _():
                zero_copy(e).wait()
            return 0

        lax.fori_loop(0, n_exp, start, 0)
        lax.fori_loop(0, n_exp, wait, 0)

    tok_ref[...] = _to_slabs(x_ref[...] * (1.0 + sc_ref[0, 0]) + sh_ref[0, 0])

    def issue(r, _):
        for k in range(TOP_K):
            pltpu.make_async_copy(tok_ref.at[r], xs_ref.at[dest_ref[k, r]], sem).start(priority=k % 2)
        return 0

    lax.fori_loop(0, tm, issue, 0)
    for k in range(TOP_K):
        pltpu.make_async_copy(tok_ref, xs_ref.at[pl.ds(0, tm)], sem).wait()


def _dispatch(x, mod, dest, pad_ends, padded, n_slots, bm):
    nb, s, d = x.shape
    t = nb * s
    tm = _tile(TM_DISPATCH, s)
    per_b = s // tm
    n_exp = pad_ends.shape[0]
    n_seg = d // LANES
    assert n_seg % SUBLANES == 0, "token slabs must be whole (8, 128) tiles"
    kern = functools.partial(_dispatch_kernel, n_exp=n_exp, tm=tm, bm=bm)
    grid_spec = pltpu.PrefetchScalarGridSpec(
        num_scalar_prefetch=2,
        grid=(t // tm,),
        in_specs=[
            pl.BlockSpec((tm, d), lambda i, pe, pd: (i, 0)),
            pl.BlockSpec((1, 1, 1, d), lambda i, pe, pd: (3, i // per_b, 0, 0)),
            pl.BlockSpec((1, 1, 1, d), lambda i, pe, pd: (4, i // per_b, 0, 0)),
            pl.BlockSpec((TOP_K, tm), lambda i, pe, pd: (0, i), memory_space=pltpu.SMEM),
        ],
        out_specs=pl.BlockSpec(memory_space=pl.ANY),
        scratch_shapes=[
            pltpu.VMEM((tm, n_seg, LANES), F32),
            pltpu.VMEM((bm, n_seg, LANES), F32),
            pltpu.SemaphoreType.DMA(()),
            pltpu.SemaphoreType.DMA(()),
        ],
    )
    return pl.pallas_call(
        kern,
        grid_spec=grid_spec,
        out_shape=jax.ShapeDtypeStruct((n_slots, n_seg, LANES), F32),
        compiler_params=_cparams(1),
        name="moe_dispatch",
    )(pad_ends, padded, x.reshape(t, d), mod, mod, dest)


def _cast_kernel(*refs):
    n = len(refs) // 2
    for src, dst in zip(refs[:n], refs[n:]):
        dst[0] = src[0, 0].astype(dst.dtype)


def _layer_weights_bf16(layer, *stacks):
    n_exp = stacks[0].shape[1]
    in_specs = [pl.BlockSpec((1, 1) + w.shape[2:], lambda e: (layer, e, 0, 0)) for w in stacks]
    out_specs = [pl.BlockSpec((1,) + w.shape[2:], lambda e: (e, 0, 0)) for w in stacks]
    return pl.pallas_call(
        _cast_kernel,
        grid=(n_exp,),
        in_specs=in_specs,
        out_specs=out_specs,
        out_shape=[jax.ShapeDtypeStruct(w.shape[1:], BF16) for w in stacks],
        compiler_params=_cparams(1),
        name="expert_weights_bf16",
    )(*stacks)


def _experts_kernel(be_ref, nu_ref, xs_ref, wg_ref, wu_ref, wd_ref, ys_ref):
    b = pl.program_id(0)

    @pl.when(b < nu_ref[0])
    def _():
        x = _from_slabs(xs_ref[...]).astype(BF16)
        g = _bdot(x, wg_ref[0])
        u = _bdot(x, wu_ref[0])
        ys_ref[...] = _to_slabs(_bdot((_silu(g) * u).astype(BF16), wd_ref[0]))

    @pl.when(b >= nu_ref[0])
    def _():
        ys_ref[...] = jnp.zeros_like(ys_ref)


def _grouped_experts(xs, blk_e, n_used, w_gate, w_up, w_down, bm):
    d, f = w_gate.shape[1:]
    n_seg = d // LANES
    n_blk = xs.shape[0] // bm

    def blk(b, be, nu):
        return jnp.minimum(b, nu[0] - 1)

    grid_spec = pltpu.PrefetchScalarGridSpec(
        num_scalar_prefetch=2,
        grid=(n_blk,),
        in_specs=[
            pl.BlockSpec((bm, n_seg, LANES), lambda b, be, nu: (blk(b, be, nu), 0, 0)),
            pl.BlockSpec((1, d, f), lambda b, be, nu: (be[blk(b, be, nu)], 0, 0)),
            pl.BlockSpec((1, d, f), lambda b, be, nu: (be[blk(b, be, nu)], 0, 0)),
            pl.BlockSpec((1, f, d), lambda b, be, nu: (be[blk(b, be, nu)], 0, 0)),
        ],
        out_specs=pl.BlockSpec((bm, n_seg, LANES), lambda b, be, nu: (jnp.where(b < nu[0], b, n_blk - 1), 0, 0)),
    )
    return pl.pallas_call(
        _experts_kernel,
        grid_spec=grid_spec,
        out_shape=jax.ShapeDtypeStruct(xs.shape, F32),
        compiler_params=_cparams(1),
        name="moe_experts",
    )(blk_e, n_used, xs, w_gate, w_up, w_down)


def _combine_kernel(x_ref, sh_ref, sc_ref, gt_ref, dest_ref, dest_next_ref, wt_ref, ys_ref,
                    wsg_ref, wsu_ref, wsd_ref, lng_ref, lnb_ref, xo_ref, rows_ref, sem, *, tm, alpha):
    i = pl.program_id(0)
    n_steps = pl.num_programs(0)
    slot = i % 2

    def gather(d_ref, buf):
        def issue(r, _):
            for k in range(TOP_K):
                pltpu.make_async_copy(ys_ref.at[d_ref[k, r]], rows_ref.at[buf, k, r],
                                      sem.at[buf]).start(priority=k % 2)
            return 0
        lax.fori_loop(0, tm, issue, 0)

    @pl.when(i == 0)
    def _():
        gather(dest_ref, 0)

    @pl.when(i + 1 < n_steps)
    def _():
        gather(dest_next_ref, 1 - slot)

    x = x_ref[...]
    tok = (x * (1.0 + sc_ref[0, 0]) + sh_ref[0, 0]).astype(BF16)
    hid = (_silu(_bdot(tok, wsg_ref[...])) * _bdot(tok, wsu_ref[...])).astype(BF16)
    f = _bdot(hid, wsd_ref[...])

    for k in range(TOP_K):
        pltpu.make_async_copy(ys_ref.at[pl.ds(0, tm)], rows_ref.at[slot, k], sem.at[slot]).wait()
    wt = wt_ref[...]
    for k in range(TOP_K):
        f = f + wt[:, k:k + 1] * _from_slabs(rows_ref[slot, k])
    v = alpha * x + gt_ref[0, 0] * f
    xo_ref[...] = _layer_norm(v, lng_ref[...], lnb_ref[...])


def _combine(x, mod, dest, wt_rows, ys, ws_gate, ws_up, ws_down, ln_g, ln_b, alpha):
    nb, s, d = x.shape
    t = nb * s
    tm = _tile(TM_COMBINE, s)
    per_b = s // tm
    f = ws_gate.shape[-1]
    n_seg = d // LANES
    kern = functools.partial(_combine_kernel, tm=tm, alpha=alpha)
    vspec = pl.BlockSpec((1, d), lambda i: (0, 0))
    n_steps = t // tm
    out = pl.pallas_call(
        kern,
        grid=(n_steps,),
        in_specs=[
            pl.BlockSpec((tm, d), lambda i: (i, 0)),
            pl.BlockSpec((1, 1, 1, d), lambda i: (3, i // per_b, 0, 0)),
            pl.BlockSpec((1, 1, 1, d), lambda i: (4, i // per_b, 0, 0)),
            pl.BlockSpec((1, 1, 1, d), lambda i: (5, i // per_b, 0, 0)),
            pl.BlockSpec((TOP_K, tm), lambda i: (0, i), memory_space=pltpu.SMEM),
            pl.BlockSpec((TOP_K, tm), lambda i: (0, jnp.minimum(i + 1, n_steps - 1)), memory_space=pltpu.SMEM),
            pl.BlockSpec((tm, TOP_K), lambda i: (i, 0)),
            pl.BlockSpec(memory_space=pl.ANY),
            pl.BlockSpec((d, f), lambda i: (0, 0)),
            pl.BlockSpec((d, f), lambda i: (0, 0)),
            pl.BlockSpec((f, d), lambda i: (0, 0)),
            vspec, vspec,
        ],
        out_specs=pl.BlockSpec((tm, d), lambda i: (i, 0)),
        out_shape=jax.ShapeDtypeStruct((t, d), F32),
        scratch_shapes=[pltpu.VMEM((2, TOP_K, tm, n_seg, LANES), F32), pltpu.SemaphoreType.DMA((2,))],
        compiler_params=_cparams(1),
        name="moe_combine",
    )(x.reshape(t, d), mod, mod, mod, dest, dest, wt_rows, ys, ws_gate, ws_up, ws_down,
      ln_g.reshape(1, d), ln_b.reshape(1, d))
    return out.reshape(nb, s, d)


def _moe_block(x, mod, w_router, b_router, w_gate, w_up, w_down, ws_gate, ws_up, ws_down, ln_g, ln_b, alpha):
    nb, s, d = x.shape
    t = nb * s
    n_exp = w_router.shape[1]
    bm = _tile(MOE_BLOCK, t * TOP_K)
    idx, wt, rank, counts = _route(x, mod, w_router, b_router)

    counts = counts.reshape(n_exp)
    padded = (counts + bm - 1) // bm * bm
    pad_ends = jnp.cumsum(padded).astype(jnp.int32)
    pad_starts = pad_ends - padded
    dest = _slots(idx, rank, pad_starts.astype(jnp.int32))
    n_blk = t * TOP_K // bm + n_exp
    n_used = (pad_ends[-1:] // bm).astype(jnp.int32)
    blk_start = jnp.arange(n_blk, dtype=jnp.int32) * bm
    blk_e = jnp.sum((pad_ends[None, :] <= blk_start[:, None]).astype(jnp.int32), axis=1)
    blk_e = jnp.minimum(blk_e, n_exp - 1).astype(jnp.int32)

    xs = _dispatch(x, mod, dest, pad_ends, padded.astype(jnp.int32), n_blk * bm, bm)
    ys = _grouped_experts(xs, blk_e, n_used, w_gate, w_up, w_down, bm)
    return _combine(x, mod, dest, wt.T, ys, ws_gate.astype(BF16), ws_up.astype(BF16), ws_down.astype(BF16),
                    ln_g, ln_b, alpha)


def kernel(x, c, ctx, c_ctx, w_mod, b_mod, ln1_g, ln1_b, ln2_g, ln2_b, rg_w_in, rg_b_in, rg_conv_w, rg_conv_b, rg_w_gates, rg_b_gates, rg_lambda, rg_w_out, rg_b_out, cf_w_pw1, cf_b_pw1, cf_w_dw, cf_b_dw, cf_norm_g, cf_norm_b, cf_w_pw2, cf_b_pw2, moe_w_router, moe_b_router, moe_w_gate, moe_w_up, moe_w_down, sh_w_gate, sh_w_up, sh_w_down):
    nb, s, d = x.shape
    depth = w_mod.shape[0]
    assert depth == 2 and nb + 1 <= SUBLANES
    alpha = (2 * depth) ** 0.25

    c_rows = jnp.zeros((SUBLANES, d), F32).at[:nb].set(c).at[nb].set(c_ctx)
    mod_all = _adaln_vectors(c_rows, w_mod, b_mod)
    mod_all = mod_all.reshape(depth, SUBLANES, 6, 1, d).transpose(0, 2, 1, 3, 4)
    mod_lat = [mod_all[l, :, :nb] for l in range(depth)]
    mod_ctx = jnp.broadcast_to(mod_all[0, :, nb:nb + 1], (6, nb, 1, d))

    n_h, hd = rg_w_gates.shape[3], rg_w_gates.shape[4]
    r = n_h * hd
    w_in = rg_w_in[0].astype(BF16)
    wg = [jnp.concatenate([rg_w_gates[0, dr, 0], rg_w_gates[0, dr, 1]], axis=-1).astype(BF16) for dr in range(2)]
    bg = [jnp.concatenate([rg_b_gates[0, dr, 0], rg_b_gates[0, dr, 1]], axis=-1).reshape(n_h, 1, 2 * hd)
          for dr in range(2)]
    conv_w, conv_b = rg_conv_w[0], rg_conv_b[0]
    zero_state = jnp.zeros((nb, n_h, 1, hd), F32)

    proj_c = _modulated_projection(ctx, mod_ctx, w_in, rg_b_in[0], BF16)
    _, hcf = _rglru_scan(proj_c, conv_w, conv_b, wg[0], bg[0], rg_lambda[0, 0], zero_state, reverse=False)
    _, hcb = _rglru_scan(proj_c, conv_w, conv_b, wg[1], bg[1], rg_lambda[0, 1], zero_state, reverse=True)

    proj_x = _modulated_projection(x, mod_lat[0], w_in, rg_b_in[0], BF16)
    hxf, _ = _rglru_scan(proj_x, conv_w, conv_b, wg[0], bg[0], rg_lambda[0, 0], hcf, reverse=False)
    readout = (hxf, rg_w_out[0].astype(BF16), rg_b_out[0], x, mod_lat[0], ln1_g[0], ln1_b[0])
    x, _ = _rglru_scan(proj_x, conv_w, conv_b, wg[1], bg[1], rg_lambda[0, 1], hcb, reverse=True,
                       readout=readout, alpha=alpha)
    x = _moe_block(x, mod_lat[0], moe_w_router[0], moe_b_router[0],
                   *_layer_weights_bf16(0, moe_w_gate, moe_w_up, moe_w_down),
                   sh_w_gate[0], sh_w_up[0], sh_w_down[0], ln2_g[0], ln2_b[0], alpha)

    z = _modulated_glu(x, mod_lat[1], cf_w_pw1[0].astype(BF16), cf_b_pw1[0])
    x = _conformer_tail(z, cf_w_dw[0], cf_b_dw[0], cf_norm_g[0], cf_norm_b[0], cf_w_pw2[0].astype(BF16),
                        cf_b_pw2[0], x, mod_lat[1], ln1_g[1], ln1_b[1], alpha)
    x = _moe_block(x, mod_lat[1], moe_w_router[1], moe_b_router[1],
                   *_layer_weights_bf16(1, moe_w_gate, moe_w_up, moe_w_down),
                   sh_w_gate[1], sh_w_up[1], sh_w_down[1], ln2_g[1], ln2_b[1], alpha)
    return x
```

```python
import functools

import jax
import jax.numpy as jnp
from jax import lax
from jax.experimental import pallas as pl
from jax.experimental.pallas import tpu as pltpu

F32 = jnp.float32
BF16 = jnp.bfloat16

RG_C = 8.0
N_GROUPS = 8
TOPK_GROUPS = 4
TOP_K = 8
ROUTED_SCALE = 2.5
LN_EPS = 1e-5

VMEM_LIMIT_BYTES = 56 * 1024 * 1024
SUBLANES = 8
LANES = 128
BF16_ROWS = 16

TM_PROJ = 512
TM_SCAN = 512
TM_CONV = 256
TM_ROUTE = 512
TM_DISPATCH = 256
TM_COMBINE = 128
MOE_BLOCK = 512


def _tile(default, n):
    t = min(default, n)
    assert n % t == 0, (default, n)
    return t


def _cparams(n_axes):
    return pltpu.CompilerParams(
        dimension_semantics=("arbitrary",) * n_axes,
        vmem_limit_bytes=VMEM_LIMIT_BYTES,
    )


def _sigmoid(v):
    return 0.5 * jnp.tanh(0.5 * v) + 0.5


def _sqrt_nonneg(v):
    return jnp.where(v > 0.0, v * lax.rsqrt(v), 0.0)


def _silu(v):
    return v * _sigmoid(v)


def _gelu_tanh(v):
    c = 0.7978845608028654
    return 0.5 * v * (1.0 + jnp.tanh(c * (v + 0.044715 * (v * v * v))))


def _softplus(v):
    return jnp.maximum(v, 0.0) + jnp.log(1.0 + jnp.exp(-jnp.abs(v)))


def _layer_norm(v, g, b):
    mu = jnp.mean(v, axis=-1, keepdims=True)
    d = v - mu
    var = jnp.mean(d * d, axis=-1, keepdims=True)
    return d * lax.rsqrt(var + LN_EPS) * g + b


def _bdot(a, b):
    return jnp.dot(a, b, preferred_element_type=F32)


def _mod_kernel(c_ref, w_ref, b_ref, o_ref):
    s = _silu(c_ref[...])
    o_ref[0] = _bdot(s.astype(BF16), w_ref[0].astype(BF16)) + b_ref[0]


def _adaln_vectors(c_rows, w_mod, b_mod):
    depth, d, d6 = w_mod.shape
    tn = _tile(1024, d6)
    return pl.pallas_call(
        _mod_kernel,
        grid=(depth, d6 // tn),
        in_specs=[
            pl.BlockSpec((SUBLANES, d), lambda l, n: (0, 0)),
            pl.BlockSpec((1, d, tn), lambda l, n: (l, 0, n)),
            pl.BlockSpec((1, 1, tn), lambda l, n: (l, 0, n)),
        ],
        out_specs=pl.BlockSpec((1, SUBLANES, tn), lambda l, n: (l, 0, n)),
        out_shape=jax.ShapeDtypeStruct((depth, SUBLANES, d6), F32),
        compiler_params=_cparams(2),
        name="adaln_vectors",
    )(c_rows, w_mod, b_mod.reshape(depth, 1, d6))


def _mod_spec(chunk, batch_of):
    def index(*ids):
        return (chunk, batch_of(*ids), 0, 0)
    return index


def _proj_kernel(x_ref, sh_ref, sc_ref, w_ref, b_ref, o_ref):
    h = x_ref[0] * (1.0 + sc_ref[0, 0]) + sh_ref[0, 0]
    o_ref[0] = (_bdot(h.astype(BF16), w_ref[...]) + b_ref[...]).astype(o_ref.dtype)


def _modulated_projection(x, mod, w_bf16, bias, out_dtype):
    nb, s, d = x.shape
    n = w_bf16.shape[1]
    tm = _tile(TM_PROJ, s)
    tn = _tile(1280, n)
    dm = mod.shape[-1]
    return pl.pallas_call(
        _proj_kernel,
        grid=(n // tn, nb, s // tm),
        in_specs=[
            pl.BlockSpec((1, tm, d), lambda j, b, i: (b, i, 0)),
            pl.BlockSpec((1, 1, 1, dm), lambda j, b, i: (0, b, 0, 0)),
            pl.BlockSpec((1, 1, 1, dm), lambda j, b, i: (1, b, 0, 0)),
            pl.BlockSpec((d, tn), lambda j, b, i: (0, j)),
            pl.BlockSpec((1, tn), lambda j, b, i: (0, j)),
        ],
        out_specs=pl.BlockSpec((1, tm, tn), lambda j, b, i: (b, i, j)),
        out_shape=jax.ShapeDtypeStruct((nb, s, n), out_dtype),
        compiler_params=_cparams(3),
        name="modulated_projection",
    )(x, mod, mod, w_bf16, bias.reshape(1, n))


def _rglru_kernel(*refs, reverse, readout, n_t, tm, alpha):
    if readout:
        (cur_ref, prev_ref, next_ref, perm_ref, cw_ref, cb_ref, wg_ref, bg_ref, lam_ref, h0_ref,
         hf_ref, gate_ref, wout_ref, bout_ref, x_ref, g1_ref, lng_ref, lnb_ref,
         xo_ref, hlast_ref,
         win_ref, a_ref, hl_ref, carry_ref, acc_ref) = refs
    else:
        (cur_ref, prev_ref, next_ref, perm_ref, cw_ref, cb_ref, wg_ref, bg_ref, lam_ref, h0_ref,
         ho_ref, hlast_ref,
         win_ref, a_ref, hl_ref, carry_ref) = refs
    i = pl.program_id(1)
    h = pl.program_id(2)
    n_h = pl.num_programs(2)
    ti = (n_t - 1 - i) if reverse else i
    hd = cur_ref.shape[-1]
    chunk = tm // SUBLANES
    sub = lax.broadcasted_iota(jnp.int32, (SUBLANES, hd), 0)

    rnn = _bdot(perm_ref[...], cur_ref[0])
    prev_last = jnp.where(ti == 0, 0.0, prev_ref[0].astype(F32)[BF16_ROWS - 1:, :])
    nxt = jnp.where(ti == n_t - 1, 0.0, next_ref[0].astype(F32)[:2, :])
    win_ref[0:SUBLANES, :] = jnp.where(
        sub == 0, prev_last, pltpu.roll(rnn[tm - SUBLANES:, :], 1, axis=0))
    win_ref[SUBLANES:SUBLANES + tm, :] = rnn
    for extra in range(2):
        lo = SUBLANES + tm + extra * SUBLANES
        win_ref[lo:lo + SUBLANES, :] = jnp.where(
            sub == SUBLANES - 1, nxt[extra:extra + 1, :],
            pltpu.roll(rnn[extra * SUBLANES:(extra + 1) * SUBLANES, :], SUBLANES - 1, axis=0))
    n_taps = cw_ref.shape[0]
    xb = cb_ref[...] + cw_ref[0:1, :] * win_ref[pl.ds(0, tm), :]
    for k in range(1, n_taps):
        xb = xb + cw_ref[k:k + 1, :] * win_ref[pl.ds(k * SUBLANES, tm), :]

    pre = _bdot(xb.astype(BF16), wg_ref[0]) + bg_ref[0]
    r = _sigmoid(pre[:, :hd])
    gi = _sigmoid(pre[:, hd:])
    log_a = (-RG_C) * r * _softplus(-lam_ref[...])
    a = jnp.exp(log_a)
    a_ref[...] = a
    hl_ref[...] = _sqrt_nonneg(1.0 - a * a) * gi * xb

    if readout:
        gate_act = _gelu_tanh(_bdot(perm_ref[...], gate_ref[0]))

    def scan_step(jj, carry):
        a_cum, h_loc = carry
        j = (chunk - 1 - jj) if reverse else jj
        rows = pl.ds(pl.multiple_of(j * SUBLANES, SUBLANES), SUBLANES)
        aj = a_ref[rows, :]
        h_loc = aj * h_loc + hl_ref[rows, :]
        a_cum = a_cum * aj
        hl_ref[rows, :] = h_loc
        a_ref[rows, :] = a_cum
        return a_cum, h_loc

    a_end, h_end = lax.fori_loop(
        0, chunk, scan_step, (jnp.ones((SUBLANES, hd), F32), jnp.zeros((SUBLANES, hd), F32)),
        unroll=SUBLANES)

    @pl.when(i == 0)
    def _():
        carry_ref[h] = h0_ref[0, h]

    state = carry_ref[h]
    entry = jnp.zeros((SUBLANES, hd), F32)
    for s in (range(SUBLANES - 1, -1, -1) if reverse else range(SUBLANES)):
        entry = jnp.where(sub == s, state, entry)
        state = a_end[s:s + 1, :] * state + h_end[s:s + 1, :]
    carry_ref[h] = state
    hlast_ref[0, h] = state

    h_full = hl_ref[...] + a_ref[...] * jnp.concatenate([entry] * chunk, axis=0)
    if not readout:
        ho_ref[0] = h_full.astype(ho_ref.dtype)
    else:
        h_sum = h_full + hf_ref[0].astype(F32)
        contrib = _bdot((h_sum * gate_act).astype(BF16), wout_ref[...])
        n_lc = acc_ref.shape[0]

        @pl.when(h == 0)
        def _():
            for c in range(n_lc):
                acc_ref[c] = contrib[:, c * LANES:(c + 1) * LANES]

        @pl.when(h != 0)
        def _():
            for c in range(n_lc):
                acc_ref[c] += contrib[:, c * LANES:(c + 1) * LANES]

        @pl.when(h == n_h - 1)
        def _():
            y = jnp.concatenate(
                [jnp.concatenate(
                    [acc_ref[c, pl.ds(q * SUBLANES * SUBLANES + s, SUBLANES, stride=SUBLANES), :]
                     for s in range(SUBLANES) for q in range(chunk // SUBLANES)], axis=0)
                 for c in range(n_lc)], axis=1) + bout_ref[...]
            v = alpha * x_ref[0] + g1_ref[0, 0] * y
            xo_ref[0] = _layer_norm(v, lng_ref[...], lnb_ref[...])


def _rglru_scan(proj, conv_w, conv_b, wg, bg, lam, h0, *, reverse, h_dtype=BF16, readout=None, alpha=None):
    nb, s, r2 = proj.shape
    r = r2 // 2
    n_h, hd = wg.shape[0], wg.shape[1]
    tm = _tile(TM_SCAN, s)
    n_t = s // tm
    hb = tm // BF16_ROWS
    n_hb = s // BF16_ROWS

    def tix(i):
        return (n_t - 1 - i) if reverse else i

    rows = jnp.arange(tm, dtype=jnp.int32)
    time_of_row = (rows % SUBLANES) * (tm // SUBLANES) + rows // SUBLANES
    perm = (time_of_row[:, None] == rows[None, :]).astype(BF16)
    perm_spec = pl.BlockSpec((tm, tm), lambda b, i, h: (0, 0))

    in_specs = [
        pl.BlockSpec((1, tm, hd), lambda b, i, h: (b, tix(i), n_h + h)),
        pl.BlockSpec((1, BF16_ROWS, hd), lambda b, i, h: (b, jnp.maximum(tix(i) * hb - 1, 0), n_h + h)),
        pl.BlockSpec((1, BF16_ROWS, hd), lambda b, i, h: (b, jnp.minimum((tix(i) + 1) * hb, n_hb - 1), n_h + h)),
        perm_spec,
        pl.BlockSpec((conv_w.shape[0], hd), lambda b, i, h: (0, h)),
        pl.BlockSpec((1, hd), lambda b, i, h: (0, h)),
        pl.BlockSpec((1, hd, 2 * hd), lambda b, i, h: (h, 0, 0)),
        pl.BlockSpec((1, 1, 2 * hd), lambda b, i, h: (h, 0, 0)),
        pl.BlockSpec((1, hd), lambda b, i, h: (0, h)),
        pl.BlockSpec((1, n_h, 1, hd), lambda b, i, h: (b, 0, 0, 0)),
    ]
    args = [proj, proj, proj, perm, conv_w, conv_b.reshape(1, r), wg, bg, lam.reshape(1, r), h0]
    assert conv_w.shape[0] == 4 and tm // SUBLANES >= 2, "window below covers taps at time offsets -1 .. +2"
    scratch = [
        pltpu.VMEM((tm + 3 * SUBLANES, hd), F32),
        pltpu.VMEM((tm, hd), F32),
        pltpu.VMEM((tm, hd), F32),
        pltpu.VMEM((n_h, 1, hd), F32),
    ]
    hlast_spec = pl.BlockSpec((1, n_h, 1, hd), lambda b, i, h: (b, 0, 0, 0))
    hlast_shape = jax.ShapeDtypeStruct((nb, n_h, 1, hd), F32)
    if readout is None:
        out_specs = [pl.BlockSpec((1, tm, hd), lambda b, i, h: (b, tix(i), h)), hlast_spec]
        out_shape = [jax.ShapeDtypeStruct((nb, s, r), h_dtype), hlast_shape]
    else:
        h_other, w_out, b_out, x, mod, ln_g, ln_b = readout
        d = x.shape[-1]
        in_specs += [
            pl.BlockSpec((1, tm, hd), lambda b, i, h: (b, tix(i), h)),
            pl.BlockSpec((1, tm, hd), lambda b, i, h: (b, tix(i), h)),
            pl.BlockSpec((hd, d), lambda b, i, h: (h, 0)),
            pl.BlockSpec((1, d), lambda b, i, h: (0, 0)),
            pl.BlockSpec((1, tm, d), lambda b, i, h: (b, tix(i), 0)),
            pl.BlockSpec((1, 1, 1, d), lambda b, i, h: (2, b, 0, 0)),
            pl.BlockSpec((1, d), lambda b, i, h: (0, 0)),
            pl.BlockSpec((1, d), lambda b, i, h: (0, 0)),
        ]
        args += [h_other, proj, w_out, b_out.reshape(1, d), x, mod, ln_g.reshape(1, d), ln_b.reshape(1, d)]
        assert (tm // SUBLANES) % SUBLANES == 0, "readout un-interleaves 8 vreg rows at a time"
        scratch += [pltpu.VMEM((d // LANES, tm, LANES), F32)]
        out_specs = [pl.BlockSpec((1, tm, d), lambda b, i, h: (b, tix(i), 0)), hlast_spec]
        out_shape = [jax.ShapeDtypeStruct((nb, s, d), F32), hlast_shape]
    kern = functools.partial(_rglru_kernel, reverse=reverse, readout=readout is not None,
                             n_t=n_t, tm=tm, alpha=alpha)
    return pl.pallas_call(
        kern,
        grid=(nb, n_t, n_h),
        in_specs=in_specs,
        out_specs=out_specs,
        out_shape=out_shape,
        scratch_shapes=scratch,
        compiler_params=_cparams(3),
        name="rglru_bwd_readout" if readout is not None else ("rglru_bwd" if reverse else "rglru_fwd"),
    )(*args)


def _glu_kernel(x_ref, sh_ref, sc_ref, wv_ref, wg_ref, bv_ref, bgate_ref, o_ref):
    h = (x_ref[0] * (1.0 + sc_ref[0, 0]) + sh_ref[0, 0]).astype(BF16)
    val = _bdot(h, wv_ref[...]) + bv_ref[...]
    gate = _bdot(h, wg_ref[...]) + bgate_ref[...]
    o_ref[0] = (val * _sigmoid(gate)).astype(o_ref.dtype)


def _modulated_glu(x, mod, w_bf16, bias):
    nb, s, d = x.shape
    n = w_bf16.shape[1] // 2
    tm = _tile(TM_PROJ, s)
    tn = _tile(1024, n)
    nj = n // tn
    b2 = bias.reshape(1, 2 * n)
    return pl.pallas_call(
        _glu_kernel,
        grid=(nj, nb, s // tm),
        in_specs=[
            pl.BlockSpec((1, tm, d), lambda j, b, i: (b, i, 0)),
            pl.BlockSpec((1, 1, 1, d), lambda j, b, i: (0, b, 0, 0)),
            pl.BlockSpec((1, 1, 1, d), lambda j, b, i: (1, b, 0, 0)),
            pl.BlockSpec((d, tn), lambda j, b, i: (0, j)),
            pl.BlockSpec((d, tn), lambda j, b, i: (0, nj + j)),
            pl.BlockSpec((1, tn), lambda j, b, i: (0, j)),
            pl.BlockSpec((1, tn), lambda j, b, i: (0, nj + j)),
        ],
        out_specs=pl.BlockSpec((1, tm, tn), lambda j, b, i: (b, i, j)),
        out_shape=jax.ShapeDtypeStruct((nb, s, n), BF16),
        compiler_params=_cparams(3),
        name="modulated_glu",
    )(x, mod, mod, w_bf16, w_bf16, b2, b2)


def _conformer_tail_kernel(cur_ref, prev_ref, next_ref, wdw_ref, bdw_ref, ng_ref, nb_ref, w2_ref, b2_ref,
                           x_ref, g1_ref, lng_ref, lnb_ref, xo_ref, win_ref, z_ref, shift_ref, *, n_t, tm, alpha,
                           lane_chunk, row_chunk):
    i = pl.program_id(1)
    d = cur_ref.shape[-1]
    halo = BF16_ROWS
    win_ref[0:halo, :] = jnp.where(i == 0, 0.0, prev_ref[0].astype(F32))
    win_ref[halo:halo + tm, :] = cur_ref[0].astype(F32)
    win_ref[halo + tm:, :] = jnp.where(i == n_t - 1, 0.0, next_ref[0].astype(F32))
    n_taps = wdw_ref.shape[0]
    base = halo - (n_taps - 1) // 2

    n_shift = shift_ref.shape[1]
    for c in range(d // lane_chunk):
        lanes = slice(c * lane_chunk, (c + 1) * lane_chunk)
        for s in range(SUBLANES):
            shift_ref[s] = win_ref[pl.ds(s, n_shift), lanes]

        def rows_step(rc, _, lanes=lanes):
            r0 = pl.multiple_of(rc * row_chunk, row_chunk)
            acc = jnp.zeros((row_chunk, lane_chunk), F32) + bdw_ref[:, lanes]
            for k in range(n_taps):
                q, s = divmod(base + k, SUBLANES)
                rows = pl.ds(pl.multiple_of(r0 + q * SUBLANES, SUBLANES), row_chunk)
                acc = acc + wdw_ref[k:k + 1, lanes] * shift_ref[s, rows, :]
            z_ref[pl.ds(r0, row_chunk), lanes] = acc
            return 0

        lax.fori_loop(0, tm // row_chunk, rows_step, 0)

    z = _layer_norm(z_ref[...], ng_ref[...], nb_ref[...])
    y = _bdot(_silu(z).astype(BF16), w2_ref[...]) + b2_ref[...]
    v = alpha * x_ref[0] + g1_ref[0, 0] * y
    xo_ref[0] = _layer_norm(v, lng_ref[...], lnb_ref[...])


def _conformer_tail(z, w_dw, b_dw, n_g, n_b, w2_bf16, b2, x, mod, ln_g, ln_b, alpha):
    nb, s, d = z.shape
    tm = _tile(TM_CONV, s)
    n_t = s // tm
    hb = tm // BF16_ROWS
    n_hb = s // BF16_ROWS
    lane_chunk = min(256, d)
    n_taps = w_dw.shape[0]
    last_tap_row = BF16_ROWS + (n_taps - 1) // 2
    n_shift = tm + last_tap_row // SUBLANES * SUBLANES
    assert n_shift + SUBLANES - 1 <= tm + 2 * BF16_ROWS and (n_taps - 1) // 2 <= BF16_ROWS
    kern = functools.partial(_conformer_tail_kernel, n_t=n_t, tm=tm, alpha=alpha,
                             lane_chunk=lane_chunk, row_chunk=min(32, tm))
    vec = lambda a: a.reshape(1, d)
    vspec = pl.BlockSpec((1, d), lambda b, i: (0, 0))
    return pl.pallas_call(
        kern,
        grid=(nb, n_t),
        in_specs=[
            pl.BlockSpec((1, tm, d), lambda b, i: (b, i, 0)),
            pl.BlockSpec((1, BF16_ROWS, d), lambda b, i: (b, jnp.maximum(i * hb - 1, 0), 0)),
            pl.BlockSpec((1, BF16_ROWS, d), lambda b, i: (b, jnp.minimum((i + 1) * hb, n_hb - 1), 0)),
            pl.BlockSpec((w_dw.shape[0], d), lambda b, i: (0, 0)),
            vspec, vspec, vspec,
            pl.BlockSpec((d, d), lambda b, i: (0, 0)),
            vspec,
            pl.BlockSpec((1, tm, d), lambda b, i: (b, i, 0)),
            pl.BlockSpec((1, 1, 1, d), lambda b, i: (2, b, 0, 0)),
            vspec, vspec,
        ],
        out_specs=pl.BlockSpec((1, tm, d), lambda b, i: (b, i, 0)),
        out_shape=jax.ShapeDtypeStruct((nb, s, d), F32),
        scratch_shapes=[pltpu.VMEM((tm + 2 * BF16_ROWS, d), F32), pltpu.VMEM((tm, d), F32),
                        pltpu.VMEM((SUBLANES, n_shift, lane_chunk), F32)],
        compiler_params=_cparams(2),
        name="conformer_tail",
    )(z, z, z, w_dw, vec(b_dw), vec(n_g), vec(n_b), w2_bf16, vec(b2), x, mod, vec(ln_g), vec(ln_b))


def _router_kernel(x_ref, sh_ref, sc_ref, whi_ref, wlo_ref, bc_ref,
                   idx_ref, wt_ref, rank_ref, cnt_ref, carry_ref, *, n_exp, tm):
    i = pl.program_id(0)

    @pl.when(i == 0)
    def _():
        carry_ref[...] = jnp.zeros_like(carry_ref)

    tok = x_ref[...] * (1.0 + sc_ref[0, 0]) + sh_ref[0, 0]
    t_hi = tok.astype(BF16)
    t_lo = (tok - t_hi.astype(F32)).astype(BF16)
    nt = (((1,), (1,)), ((), ()))
    logits = (lax.dot_general(whi_ref[...], t_hi, nt, preferred_element_type=F32)
              + lax.dot_general(whi_ref[...], t_lo, nt, preferred_element_type=F32)
              + lax.dot_general(wlo_ref[...], t_hi, nt, preferred_element_type=F32))
    scores = _sigmoid(logits)
    biased = scores + bc_ref[...]

    per_group = n_exp // N_GROUPS
    sub = lax.broadcasted_iota(jnp.int32, (per_group, tm), 0)
    neg = -jnp.inf
    grp = [biased[g * per_group:(g + 1) * per_group, :] for g in range(N_GROUPS)]
    sc_g = [scores[g * per_group:(g + 1) * per_group, :] for g in range(N_GROUPS)]

    gscore = []
    for g in range(N_GROUPS):
        m1 = jnp.max(grp[g], axis=0, keepdims=True)
        first = jnp.min(jnp.where(grp[g] == m1, sub, per_group), axis=0, keepdims=True)
        m2 = jnp.max(jnp.where(sub == first, neg, grp[g]), axis=0, keepdims=True)
        gscore.append(m1 + m2)
    masked = []
    for g in range(N_GROUPS):
        beaten = jnp.zeros((1, tm), jnp.int32)
        for o in range(N_GROUPS):
            if o == g:
                continue
            wins = (gscore[o] > gscore[g]) | ((gscore[o] == gscore[g]) & (o < g))
            beaten = beaten + wins.astype(jnp.int32)
        masked.append(jnp.where(beaten < TOPK_GROUPS, grp[g], neg))

    eid = [sub + g * per_group for g in range(N_GROUPS)]
    member = [jnp.zeros((per_group, tm), F32) for _ in range(N_GROUPS)]
    idx_rows, score_rows = [], []
    for _ in range(TOP_K):
        m = masked[0]
        for g in range(1, N_GROUPS):
            m = jnp.maximum(m, masked[g])
        m = jnp.max(m, axis=0, keepdims=True)
        cand = jnp.where(masked[0] == m, eid[0], n_exp)
        for g in range(1, N_GROUPS):
            cand = jnp.minimum(cand, jnp.where(masked[g] == m, eid[g], n_exp))
        first = jnp.min(cand, axis=0, keepdims=True)
        picked = jnp.zeros((per_group, tm), F32)
        for g in range(N_GROUPS):
            sel = eid[g] == first
            picked = picked + jnp.where(sel, sc_g[g], 0.0)
            member[g] = jnp.where(sel, 1.0, member[g])
            masked[g] = jnp.where(sel, neg, masked[g])
        idx_rows.append(first)
        score_rows.append(jnp.sum(picked, axis=0, keepdims=True))

    total = score_rows[0]
    for k in range(1, TOP_K):
        total = total + score_rows[k]

    memb = jnp.concatenate(member, axis=0)
    earlier = jnp.where(lax.broadcasted_iota(jnp.int32, (tm, tm), 0)
                        < lax.broadcasted_iota(jnp.int32, (tm, tm), 1), 1.0, 0.0).astype(BF16)
    pos = _bdot(memb.astype(BF16), earlier) + carry_ref[...]
    pos_g = [pos[g * per_group:(g + 1) * per_group, :] for g in range(N_GROUPS)]
    for k in range(TOP_K):
        acc = jnp.zeros((per_group, tm), F32)
        for g in range(N_GROUPS):
            acc = acc + jnp.where(eid[g] == idx_rows[k], pos_g[g], 0.0)
        rank_ref[k:k + 1, :] = jnp.sum(acc, axis=0, keepdims=True).astype(jnp.int32)
        idx_ref[k:k + 1, :] = idx_rows[k]
        wt_ref[k:k + 1, :] = score_rows[k] / total * ROUTED_SCALE
    carry_ref[...] += jnp.sum(memb, axis=1, keepdims=True)
    cnt_ref[...] = carry_ref[...].astype(jnp.int32)


def _route(x, mod, w_router, b_corr):
    nb, s, d = x.shape
    t = nb * s
    n_exp = w_router.shape[1]
    tm = _tile(TM_ROUTE, s)
    per_b = s // tm
    w_t = w_router.T
    w_hi = w_t.astype(BF16)
    w_lo = (w_t - w_hi.astype(F32)).astype(BF16)
    kern = functools.partial(_router_kernel, n_exp=n_exp, tm=tm)
    out_spec = pl.BlockSpec((TOP_K, tm), lambda i: (0, i))
    return pl.pallas_call(
        kern,
        grid=(t // tm,),
        in_specs=[
            pl.BlockSpec((tm, d), lambda i: (i, 0)),
            pl.BlockSpec((1, 1, 1, d), lambda i: (3, i // per_b, 0, 0)),
            pl.BlockSpec((1, 1, 1, d), lambda i: (4, i // per_b, 0, 0)),
            pl.BlockSpec((n_exp, d), lambda i: (0, 0)),
            pl.BlockSpec((n_exp, d), lambda i: (0, 0)),
            pl.BlockSpec((n_exp, 1), lambda i: (0, 0)),
        ],
        out_specs=[out_spec, out_spec, out_spec, pl.BlockSpec((n_exp, 1), lambda i: (0, 0))],
        out_shape=[
            jax.ShapeDtypeStruct((TOP_K, t), jnp.int32),
            jax.ShapeDtypeStruct((TOP_K, t), F32),
            jax.ShapeDtypeStruct((TOP_K, t), jnp.int32),
            jax.ShapeDtypeStruct((n_exp, 1), jnp.int32),
        ],
        scratch_shapes=[pltpu.VMEM((n_exp, 1), F32)],
        compiler_params=_cparams(1),
        name="moe_router",
    )(x.reshape(t, d), mod, mod, w_hi, w_lo, b_corr.reshape(n_exp, 1).astype(F32))


def _slots_kernel(ps_ref, idx_ref, rank_ref, dest_ref, *, n_exp):
    idx = idx_ref[...]

    def add_start(e, acc):
        return acc + jnp.where(idx == e, ps_ref[e], 0)

    dest_ref[...] = lax.fori_loop(0, n_exp, add_start, rank_ref[...])


def _slots(idx, rank, pad_starts):
    k, t = idx.shape
    tm = _tile(2048, t)
    grid_spec = pltpu.PrefetchScalarGridSpec(
        num_scalar_prefetch=1,
        grid=(t // tm,),
        in_specs=[pl.BlockSpec((k, tm), lambda i, ps: (0, i)), pl.BlockSpec((k, tm), lambda i, ps: (0, i))],
        out_specs=pl.BlockSpec((k, tm), lambda i, ps: (0, i)),
    )
    return pl.pallas_call(
        functools.partial(_slots_kernel, n_exp=pad_starts.shape[0]),
        grid_spec=grid_spec,
        out_shape=jax.ShapeDtypeStruct((k, t), jnp.int32),
        compiler_params=_cparams(1),
        name="moe_slots",
    )(pad_starts, idx, rank)


def _to_slabs(value):
    n_seg = value.shape[1] // LANES
    by_seg = jnp.stack([value[:, s * LANES:(s + 1) * LANES] for s in range(n_seg)], axis=0)
    return jnp.swapaxes(by_seg, 0, 1)


def _from_slabs(slabs):
    by_seg = jnp.swapaxes(slabs, 0, 1)
    return jnp.concatenate([by_seg[s] for s in range(slabs.shape[1])], axis=1)


def _dispatch_kernel(pend_ref, padded_ref, x_ref, sh_ref, sc_ref, dest_ref, xs_ref,
                     tok_ref, zero_ref, sem, zsem, *, n_exp, tm, bm):
    i = pl.program_id(0)

    def zero_copy(e):
        return pltpu.make_async_copy(zero_ref, xs_ref.at[pl.ds(pend_ref[e] - bm, bm)], zsem)

    @pl.when(i == 0)
    def _():
        zero_ref[...] = jnp.zeros_like(zero_ref)

        def start(e, _):
            @pl.when(padded_ref[e] > 0)
            def _():
                zero_copy(e).start()
            return 0

        def wait(e, _):
            @pl.when(padded_ref[e] > 0)
            def _():
                zero_copy(e).wait()
            return 0

        lax.fori_loop(0, n_exp, start, 0)
        lax.fori_loop(0, n_exp, wait, 0)

    n_steps = pl.num_programs(0)
    slot = i % 2

    def drain(buf):
        for k in range(TOP_K):
            pltpu.make_async_copy(tok_ref.at[buf], xs_ref.at[pl.ds(0, tm)], sem.at[buf]).wait()

    @pl.when(i >= 2)
    def _():
        drain(slot)

    tok_ref[slot] = _to_slabs(x_ref[...] * (1.0 + sc_ref[0, 0]) + sh_ref[0, 0])

    def issue(r, _):
        for k in range(TOP_K):
            pltpu.make_async_copy(tok_ref.at[slot, r], xs_ref.at[dest_ref[k, r]],
                                  sem.at[slot]).start(priority=k % 2)
        return 0

    lax.fori_loop(0, tm, issue, 0)

    @pl.when(i == n_steps - 1)
    def _():
        @pl.when(i >= 1)
        def _():
            drain(1 - slot)
        drain(slot)


def _dispatch(x, mod, dest, pad_ends, padded, n_slots, bm):
    nb, s, d = x.shape
    t = nb * s
    tm = _tile(TM_DISPATCH, s)
    per_b = s // tm
    n_exp = pad_ends.shape[0]
    n_seg = d // LANES
    assert n_seg % SUBLANES == 0, "token slabs must be whole (8, 128) tiles"
    kern = functools.partial(_dispatch_kernel, n_exp=n_exp, tm=tm, bm=bm)
    grid_spec = pltpu.PrefetchScalarGridSpec(
        num_scalar_prefetch=2,
        grid=(t // tm,),
        in_specs=[
            pl.BlockSpec((tm, d), lambda i, pe, pd: (i, 0)),
            pl.BlockSpec((1, 1, 1, d), lambda i, pe, pd: (3, i // per_b, 0, 0)),
            pl.BlockSpec((1, 1, 1, d), lambda i, pe, pd: (4, i // per_b, 0, 0)),
            pl.BlockSpec((TOP_K, tm), lambda i, pe, pd: (0, i), memory_space=pltpu.SMEM),
        ],
        out_specs=pl.BlockSpec(memory_space=pl.ANY),
        scratch_shapes=[
            pltpu.VMEM((2, tm, n_seg, LANES), F32),
            pltpu.VMEM((bm, n_seg, LANES), F32),
            pltpu.SemaphoreType.DMA((2,)),
            pltpu.SemaphoreType.DMA(()),
        ],
    )
    return pl.pallas_call(
        kern,
        grid_spec=grid_spec,
        out_shape=jax.ShapeDtypeStruct((n_slots, n_seg, LANES), F32),
        compiler_params=_cparams(1),
        name="moe_dispatch",
    )(pad_ends, padded, x.reshape(t, d), mod, mod, dest)


def _cast_kernel(*refs):
    n = len(refs) // 2
    for src, dst in zip(refs[:n], refs[n:]):
        dst[0] = src[0, 0].astype(dst.dtype)


def _layer_weights_bf16(layer, *stacks):
    n_exp = stacks[0].shape[1]
    in_specs = [pl.BlockSpec((1, 1) + w.shape[2:], lambda e: (layer, e, 0, 0)) for w in stacks]
    out_specs = [pl.BlockSpec((1,) + w.shape[2:], lambda e: (e, 0, 0)) for w in stacks]
    return pl.pallas_call(
        _cast_kernel,
        grid=(n_exp,),
        in_specs=in_specs,
        out_specs=out_specs,
        out_shape=[jax.ShapeDtypeStruct(w.shape[1:], BF16) for w in stacks],
        compiler_params=_cparams(1),
        name="expert_weights_bf16",
    )(*stacks)


def _experts_kernel(be_ref, nu_ref, xs_ref, wg_ref, wu_ref, wd_ref, ys_ref):
    b = pl.program_id(0)

    @pl.when(b < nu_ref[0])
    def _():
        x = _from_slabs(xs_ref[...]).astype(BF16)
        g = _bdot(x, wg_ref[0])
        u = _bdot(x, wu_ref[0])
        ys_ref[...] = _to_slabs(_bdot((_silu(g) * u).astype(BF16), wd_ref[0]))

    @pl.when(b >= nu_ref[0])
    def _():
        ys_ref[...] = jnp.zeros_like(ys_ref)


def _grouped_experts(xs, blk_e, n_used, w_gate, w_up, w_down, bm):
    d, f = w_gate.shape[1:]
    n_seg = d // LANES
    n_blk = xs.shape[0] // bm

    def blk(b, be, nu):
        return jnp.minimum(b, nu[0] - 1)

    grid_spec = pltpu.PrefetchScalarGridSpec(
        num_scalar_prefetch=2,
        grid=(n_blk,),
        in_specs=[
            pl.BlockSpec((bm, n_seg, LANES), lambda b, be, nu: (blk(b, be, nu), 0, 0)),
            pl.BlockSpec((1, d, f), lambda b, be, nu: (be[blk(b, be, nu)], 0, 0)),
            pl.BlockSpec((1, d, f), lambda b, be, nu: (be[blk(b, be, nu)], 0, 0)),
            pl.BlockSpec((1, f, d), lambda b, be, nu: (be[blk(b, be, nu)], 0, 0)),
        ],
        out_specs=pl.BlockSpec((bm, n_seg, LANES), lambda b, be, nu: (jnp.where(b < nu[0], b, n_blk - 1), 0, 0)),
    )
    return pl.pallas_call(
        _experts_kernel,
        grid_spec=grid_spec,
        out_shape=jax.ShapeDtypeStruct(xs.shape, F32),
        compiler_params=_cparams(1),
        name="moe_experts",
    )(blk_e, n_used, xs, w_gate, w_up, w_down)


def _combine_kernel(x_ref, sh_ref, sc_ref, gt_ref, dest_ref, dest_next_ref, wt_ref, ys_ref,
                    wsg_ref, wsu_ref, wsd_ref, lng_ref, lnb_ref, xo_ref, rows_ref, sem, *, tm, alpha):
    i = pl.program_id(0)
    n_steps = pl.num_programs(0)
    slot = i % 2

    def gather(d_ref, buf):
        def issue(r, _):
            for k in range(TOP_K):
                pltpu.make_async_copy(ys_ref.at[d_ref[k, r]], rows_ref.at[buf, k, r],
                                      sem.at[buf]).start(priority=k % 2)
            return 0
        lax.fori_loop(0, tm, issue, 0)

    @pl.when(i == 0)
    def _():
        gather(dest_ref, 0)

    @pl.when(i + 1 < n_steps)
    def _():
        gather(dest_next_ref, 1 - slot)

    x = x_ref[...]
    tok = (x * (1.0 + sc_ref[0, 0]) + sh_ref[0, 0]).astype(BF16)
    hid = (_silu(_bdot(tok, wsg_ref[...])) * _bdot(tok, wsu_ref[...])).astype(BF16)
    f = _bdot(hid, wsd_ref[...])

    for k in range(TOP_K):
        pltpu.make_async_copy(ys_ref.at[pl.ds(0, tm)], rows_ref.at[slot, k], sem.at[slot]).wait()
    n_seg = rows_ref.shape[3]
    wt = wt_ref[...]
    routed = jnp.broadcast_to(wt[:, 0:1, :], (tm, n_seg, LANES)) * rows_ref[slot, 0]
    for k in range(1, TOP_K):
        routed = routed + jnp.broadcast_to(wt[:, k:k + 1, :], (tm, n_seg, LANES)) * rows_ref[slot, k]
    f = f + _from_slabs(routed)
    v = alpha * x + gt_ref[0, 0] * f
    xo_ref[...] = _layer_norm(v, lng_ref[...], lnb_ref[...])


def _combine(x, mod, dest, wt_rows, ys, ws_gate, ws_up, ws_down, ln_g, ln_b, alpha):
    nb, s, d = x.shape
    t = nb * s
    tm = _tile(TM_COMBINE, s)
    per_b = s // tm
    f = ws_gate.shape[-1]
    n_seg = d // LANES
    kern = functools.partial(_combine_kernel, tm=tm, alpha=alpha)
    vspec = pl.BlockSpec((1, d), lambda i: (0, 0))
    n_steps = t // tm
    out = pl.pallas_call(
        kern,
        grid=(n_steps,),
        in_specs=[
            pl.BlockSpec((tm, d), lambda i: (i, 0)),
            pl.BlockSpec((1, 1, 1, d), lambda i: (3, i // per_b, 0, 0)),
            pl.BlockSpec((1, 1, 1, d), lambda i: (4, i // per_b, 0, 0)),
            pl.BlockSpec((1, 1, 1, d), lambda i: (5, i // per_b, 0, 0)),
            pl.BlockSpec((TOP_K, tm), lambda i: (0, i), memory_space=pltpu.SMEM),
            pl.BlockSpec((TOP_K, tm), lambda i: (0, jnp.minimum(i + 1, n_steps - 1)), memory_space=pltpu.SMEM),
            pl.BlockSpec((tm, TOP_K, LANES), lambda i: (i, 0, 0)),
            pl.BlockSpec(memory_space=pl.ANY),
            pl.BlockSpec((d, f), lambda i: (0, 0)),
            pl.BlockSpec((d, f), lambda i: (0, 0)),
            pl.BlockSpec((f, d), lambda i: (0, 0)),
            vspec, vspec,
        ],
        out_specs=pl.BlockSpec((tm, d), lambda i: (i, 0)),
        out_shape=jax.ShapeDtypeStruct((t, d), F32),
        scratch_shapes=[pltpu.VMEM((2, TOP_K, tm, n_seg, LANES), F32), pltpu.SemaphoreType.DMA((2,))],
        compiler_params=_cparams(1),
        name="moe_combine",
    )(x.reshape(t, d), mod, mod, mod, dest, dest, wt_rows, ys, ws_gate, ws_up, ws_down,
      ln_g.reshape(1, d), ln_b.reshape(1, d))
    return out.reshape(nb, s, d)


def _moe_block(x, mod, w_router, b_router, w_gate, w_up, w_down, ws_gate, ws_up, ws_down, ln_g, ln_b, alpha):
    nb, s, d = x.shape
    t = nb * s
    n_exp = w_router.shape[1]
    bm = _tile(MOE_BLOCK, t * TOP_K)
    idx, wt, rank, counts = _route(x, mod, w_router, b_router)

    counts = counts.reshape(n_exp)
    padded = (counts + bm - 1) // bm * bm
    pad_ends = jnp.cumsum(padded).astype(jnp.int32)
    pad_starts = pad_ends - padded
    dest = _slots(idx, rank, pad_starts.astype(jnp.int32))
    n_blk = t * TOP_K // bm + n_exp
    n_used = (pad_ends[-1:] // bm).astype(jnp.int32)
    blk_start = jnp.arange(n_blk, dtype=jnp.int32) * bm
    blk_e = jnp.sum((pad_ends[None, :] <= blk_start[:, None]).astype(jnp.int32), axis=1)
    blk_e = jnp.minimum(blk_e, n_exp - 1).astype(jnp.int32)

    xs = _dispatch(x, mod, dest, pad_ends, padded.astype(jnp.int32), n_blk * bm, bm)
    ys = _grouped_experts(xs, blk_e, n_used, w_gate, w_up, w_down, bm)
    wt_splat = jnp.broadcast_to(wt.T[:, :, None], (t, TOP_K, LANES))
    return _combine(x, mod, dest, wt_splat, ys,ws_gate.astype(BF16), ws_up.astype(BF16), ws_down.astype(BF16),
                    ln_g, ln_b, alpha)


def kernel(x, c, ctx, c_ctx, w_mod, b_mod, ln1_g, ln1_b, ln2_g, ln2_b, rg_w_in, rg_b_in, rg_conv_w, rg_conv_b, rg_w_gates, rg_b_gates, rg_lambda, rg_w_out, rg_b_out, cf_w_pw1, cf_b_pw1, cf_w_dw, cf_b_dw, cf_norm_g, cf_norm_b, cf_w_pw2, cf_b_pw2, moe_w_router, moe_b_router, moe_w_gate, moe_w_up, moe_w_down, sh_w_gate, sh_w_up, sh_w_down):
    nb, s, d = x.shape
    depth = w_mod.shape[0]
    assert depth == 2 and nb + 1 <= SUBLANES
    alpha = (2 * depth) ** 0.25

    c_rows = jnp.zeros((SUBLANES, d), F32).at[:nb].set(c).at[nb].set(c_ctx)
    mod_all = _adaln_vectors(c_rows, w_mod, b_mod)
    mod_all = mod_all.reshape(depth, SUBLANES, 6, 1, d).transpose(0, 2, 1, 3, 4)
    mod_lat = [mod_all[l, :, :nb] for l in range(depth)]
    mod_ctx = jnp.broadcast_to(mod_all[0, :, nb:nb + 1], (6, nb, 1, d))

    n_h, hd = rg_w_gates.shape[3], rg_w_gates.shape[4]
    r = n_h * hd
    w_in = rg_w_in[0].astype(BF16)
    wg = [jnp.concatenate([rg_w_gates[0, dr, 0], rg_w_gates[0, dr, 1]], axis=-1).astype(BF16) for dr in range(2)]
    bg = [jnp.concatenate([rg_b_gates[0, dr, 0], rg_b_gates[0, dr, 1]], axis=-1).reshape(n_h, 1, 2 * hd)
          for dr in range(2)]
    conv_w, conv_b = rg_conv_w[0], rg_conv_b[0]
    zero_state = jnp.zeros((nb, n_h, 1, hd), F32)

    proj_c = _modulated_projection(ctx, mod_ctx, w_in, rg_b_in[0], BF16)
    _, hcf = _rglru_scan(proj_c, conv_w, conv_b, wg[0], bg[0], rg_lambda[0, 0], zero_state, reverse=False)
    _, hcb = _rglru_scan(proj_c, conv_w, conv_b, wg[1], bg[1], rg_lambda[0, 1], zero_state, reverse=True)

    proj_x = _modulated_projection(x, mod_lat[0], w_in, rg_b_in[0], BF16)
    hxf, _ = _rglru_scan(proj_x, conv_w, conv_b, wg[0], bg[0], rg_lambda[0, 0], hcf, reverse=False)
    readout = (hxf, rg_w_out[0].astype(BF16), rg_b_out[0], x, mod_lat[0], ln1_g[0], ln1_b[0])
    x, _ = _rglru_scan(proj_x, conv_w, conv_b, wg[1], bg[1], rg_lambda[0, 1], hcb, reverse=True,
                       readout=readout, alpha=alpha)
    x = _moe_block(x, mod_lat[0], moe_w_router[0], moe_b_router[0],
                   *_layer_weights_bf16(0, moe_w_gate, moe_w_up, moe_w_down),
                   sh_w_gate[0], sh_w_up[0], sh_w_down[0], ln2_g[0], ln2_b[0], alpha)

    z = _modulated_glu(x, mod_lat[1], cf_w_pw1[0].astype(BF16), cf_b_pw1[0])
    x = _conformer_tail(z, cf_w_dw[0], cf_b_dw[0], cf_norm_g[0], cf_norm_b[0], cf_w_pw2[0].astype(BF16),
                        cf_b_pw2[0], x, mod_lat[1], ln1_g[1], ln1_b[1], alpha)
    x = _moe_block(x, mod_lat[1], moe_w_router[1], moe_b_router[1],
                   *_layer_weights_bf16(1, moe_w_gate, moe_w_up, moe_w_down),
                   sh_w_gate[1], sh_w_up[1], sh_w_down[1], ln2_g[1], ln2_b[1], alpha)
    return x
```

```python
import functools

import jax
import jax.numpy as jnp
from jax import lax
from jax.experimental import pallas as pl
from jax.experimental.pallas import tpu as pltpu

F32 = jnp.float32
BF16 = jnp.bfloat16

RG_C = 8.0
N_GROUPS = 8
TOPK_GROUPS = 4
TOP_K = 8
ROUTED_SCALE = 2.5
LN_EPS = 1e-5

VMEM_LIMIT_BYTES = 56 * 1024 * 1024
SUBLANES = 8
LANES = 128
BF16_ROWS = 16

TM_PROJ = 512
TM_SCAN = 512
TM_CONV = 256
TM_ROUTE = 512
TM_DISPATCH = 256
TM_COMBINE = 128
MOE_BLOCK = 512


def _tile(default, n):
    t = min(default, n)
    assert n % t == 0, (default, n)
    return t


def _cparams(n_axes):
    return pltpu.CompilerParams(
        dimension_semantics=("arbitrary",) * n_axes,
        vmem_limit_bytes=VMEM_LIMIT_BYTES,
    )


def _sigmoid(v):
    return 0.5 * jnp.tanh(0.5 * v) + 0.5


def _sqrt_nonneg(v):
    return jnp.where(v > 0.0, v * lax.rsqrt(v), 0.0)


def _silu(v):
    return v * _sigmoid(v)


def _gelu_tanh(v):
    c = 0.7978845608028654
    return 0.5 * v * (1.0 + jnp.tanh(c * (v + 0.044715 * (v * v * v))))


def _softplus(v):
    return jnp.maximum(v, 0.0) + jnp.log(1.0 + jnp.exp(-jnp.abs(v)))


def _layer_norm(v, g, b):
    mu = jnp.mean(v, axis=-1, keepdims=True)
    d = v - mu
    var = jnp.mean(d * d, axis=-1, keepdims=True)
    return d * lax.rsqrt(var + LN_EPS) * g + b


def _bdot(a, b):
    return jnp.dot(a, b, preferred_element_type=F32)


def _mod_kernel(c_ref, w_ref, b_ref, o_ref):
    s = _silu(c_ref[...])
    o_ref[0] = _bdot(s.astype(BF16), w_ref[0].astype(BF16)) + b_ref[0]


def _adaln_vectors(c_rows, w_mod, b_mod):
    depth, d, d6 = w_mod.shape
    tn = _tile(1024, d6)
    return pl.pallas_call(
        _mod_kernel,
        grid=(depth, d6 // tn),
        in_specs=[
            pl.BlockSpec((SUBLANES, d), lambda l, n: (0, 0)),
            pl.BlockSpec((1, d, tn), lambda l, n: (l, 0, n)),
            pl.BlockSpec((1, 1, tn), lambda l, n: (l, 0, n)),
        ],
        out_specs=pl.BlockSpec((1, SUBLANES, tn), lambda l, n: (l, 0, n)),
        out_shape=jax.ShapeDtypeStruct((depth, SUBLANES, d6), F32),
        compiler_params=_cparams(2),
        name="adaln_vectors",
    )(c_rows, w_mod, b_mod.reshape(depth, 1, d6))


def _mod_spec(chunk, batch_of):
    def index(*ids):
        return (chunk, batch_of(*ids), 0, 0)
    return index


def _proj_kernel(x_ref, sh_ref, sc_ref, w_ref, b_ref, o_ref):
    h = x_ref[0] * (1.0 + sc_ref[0, 0]) + sh_ref[0, 0]
    o_ref[0] = (_bdot(h.astype(BF16), w_ref[...]) + b_ref[...]).astype(o_ref.dtype)


def _modulated_projection(x, mod, w_bf16, bias, out_dtype):
    nb, s, d = x.shape
    n = w_bf16.shape[1]
    tm = _tile(TM_PROJ, s)
    tn = _tile(2560, n)
    dm = mod.shape[-1]
    return pl.pallas_call(
        _proj_kernel,
        grid=(n // tn, nb, s // tm),
        in_specs=[
            pl.BlockSpec((1, tm, d), lambda j, b, i: (b, i, 0)),
            pl.BlockSpec((1, 1, 1, dm), lambda j, b, i: (0, b, 0, 0)),
            pl.BlockSpec((1, 1, 1, dm), lambda j, b, i: (1, b, 0, 0)),
            pl.BlockSpec((d, tn), lambda j, b, i: (0, j)),
            pl.BlockSpec((1, tn), lambda j, b, i: (0, j)),
        ],
        out_specs=pl.BlockSpec((1, tm, tn), lambda j, b, i: (b, i, j)),
        out_shape=jax.ShapeDtypeStruct((nb, s, n), out_dtype),
        compiler_params=_cparams(3),
        name="modulated_projection",
    )(x, mod, mod, w_bf16, bias.reshape(1, n))


def _rglru_kernel(*refs, reverse, readout, n_t, tm, alpha):
    if readout:
        (cur_ref, prev_ref, next_ref, perm_ref, cw_ref, cb_ref, wg_ref, bg_ref, lam_ref, h0_ref,
         hf_ref, gate_ref, wout_ref, bout_ref, x_ref, g1_ref, lng_ref, lnb_ref,
         xo_ref, hlast_ref,
         win_ref, a_ref, hl_ref, carry_ref, acc_ref) = refs
    else:
        (cur_ref, prev_ref, next_ref, perm_ref, cw_ref, cb_ref, wg_ref, bg_ref, lam_ref, h0_ref,
         ho_ref, hlast_ref,
         win_ref, a_ref, hl_ref, carry_ref) = refs
    i = pl.program_id(1)
    h = pl.program_id(2)
    n_h = pl.num_programs(2)
    ti = (n_t - 1 - i) if reverse else i
    hd = cur_ref.shape[-1]
    chunk = tm // SUBLANES
    sub = lax.broadcasted_iota(jnp.int32, (SUBLANES, hd), 0)

    rnn = _bdot(perm_ref[...], cur_ref[0])
    prev_last = jnp.where(ti == 0, 0.0, prev_ref[0].astype(F32)[BF16_ROWS - 1:, :])
    nxt = jnp.where(ti == n_t - 1, 0.0, next_ref[0].astype(F32)[:2, :])
    win_ref[0:SUBLANES, :] = jnp.where(
        sub == 0, prev_last, pltpu.roll(rnn[tm - SUBLANES:, :], 1, axis=0))
    win_ref[SUBLANES:SUBLANES + tm, :] = rnn
    for extra in range(2):
        lo = SUBLANES + tm + extra * SUBLANES
        win_ref[lo:lo + SUBLANES, :] = jnp.where(
            sub == SUBLANES - 1, nxt[extra:extra + 1, :],
            pltpu.roll(rnn[extra * SUBLANES:(extra + 1) * SUBLANES, :], SUBLANES - 1, axis=0))
    n_taps = cw_ref.shape[0]
    xb = cb_ref[...] + cw_ref[0:1, :] * win_ref[pl.ds(0, tm), :]
    for k in range(1, n_taps):
        xb = xb + cw_ref[k:k + 1, :] * win_ref[pl.ds(k * SUBLANES, tm), :]

    pre = _bdot(xb.astype(BF16), wg_ref[0]) + bg_ref[0]
    r = _sigmoid(pre[:, :hd])
    gi = _sigmoid(pre[:, hd:])
    log_a = (-RG_C) * r * _softplus(-lam_ref[...])
    a = jnp.exp(log_a)
    a_ref[...] = a
    hl_ref[...] = _sqrt_nonneg(1.0 - a * a) * gi * xb

    if readout:
        gate_act = _gelu_tanh(_bdot(perm_ref[...], gate_ref[0]))

    def scan_step(jj, carry):
        a_cum, h_loc = carry
        j = (chunk - 1 - jj) if reverse else jj
        rows = pl.ds(pl.multiple_of(j * SUBLANES, SUBLANES), SUBLANES)
        aj = a_ref[rows, :]
        h_loc = aj * h_loc + hl_ref[rows, :]
        a_cum = a_cum * aj
        hl_ref[rows, :] = h_loc
        a_ref[rows, :] = a_cum
        return a_cum, h_loc

    a_end, h_end = lax.fori_loop(
        0, chunk, scan_step, (jnp.ones((SUBLANES, hd), F32), jnp.zeros((SUBLANES, hd), F32)),
        unroll=SUBLANES)

    @pl.when(i == 0)
    def _():
        carry_ref[h] = h0_ref[0, h]

    state = carry_ref[h]
    entry = jnp.zeros((SUBLANES, hd), F32)
    for s in (range(SUBLANES - 1, -1, -1) if reverse else range(SUBLANES)):
        entry = jnp.where(sub == s, state, entry)
        state = a_end[s:s + 1, :] * state + h_end[s:s + 1, :]
    carry_ref[h] = state
    hlast_ref[0, h] = state

    h_full = hl_ref[...] + a_ref[...] * jnp.concatenate([entry] * chunk, axis=0)
    if not readout:
        ho_ref[0] = h_full.astype(ho_ref.dtype)
    else:
        h_sum = h_full + hf_ref[0].astype(F32)
        contrib = _bdot((h_sum * gate_act).astype(BF16), wout_ref[...])
        n_lc = acc_ref.shape[0]

        @pl.when(h == 0)
        def _():
            for c in range(n_lc):
                acc_ref[c] = contrib[:, c * LANES:(c + 1) * LANES]

        @pl.when(h != 0)
        def _():
            for c in range(n_lc):
                acc_ref[c] += contrib[:, c * LANES:(c + 1) * LANES]

        @pl.when(h == n_h - 1)
        def _():
            y = jnp.concatenate(
                [jnp.concatenate(
                    [acc_ref[c, pl.ds(q * SUBLANES * SUBLANES + s, SUBLANES, stride=SUBLANES), :]
                     for s in range(SUBLANES) for q in range(chunk // SUBLANES)], axis=0)
                 for c in range(n_lc)], axis=1) + bout_ref[...]
            v = alpha * x_ref[0] + g1_ref[0, 0] * y
            xo_ref[0] = _layer_norm(v, lng_ref[...], lnb_ref[...])


def _rglru_scan(proj, conv_w, conv_b, wg, bg, lam, h0, *, reverse, h_dtype=BF16, readout=None, alpha=None):
    nb, s, r2 = proj.shape
    r = r2 // 2
    n_h, hd = wg.shape[0], wg.shape[1]
    tm = _tile(TM_SCAN, s)
    n_t = s // tm
    hb = tm // BF16_ROWS
    n_hb = s // BF16_ROWS

    def tix(i):
        return (n_t - 1 - i) if reverse else i

    rows = jnp.arange(tm, dtype=jnp.int32)
    time_of_row = (rows % SUBLANES) * (tm // SUBLANES) + rows // SUBLANES
    perm = (time_of_row[:, None] == rows[None, :]).astype(BF16)
    perm_spec = pl.BlockSpec((tm, tm), lambda b, i, h: (0, 0))

    in_specs = [
        pl.BlockSpec((1, tm, hd), lambda b, i, h: (b, tix(i), n_h + h)),
        pl.BlockSpec((1, BF16_ROWS, hd), lambda b, i, h: (b, jnp.maximum(tix(i) * hb - 1, 0), n_h + h)),
        pl.BlockSpec((1, BF16_ROWS, hd), lambda b, i, h: (b, jnp.minimum((tix(i) + 1) * hb, n_hb - 1), n_h + h)),
        perm_spec,
        pl.BlockSpec((conv_w.shape[0], hd), lambda b, i, h: (0, h)),
        pl.BlockSpec((1, hd), lambda b, i, h: (0, h)),
        pl.BlockSpec((1, hd, 2 * hd), lambda b, i, h: (h, 0, 0)),
        pl.BlockSpec((1, 1, 2 * hd), lambda b, i, h: (h, 0, 0)),
        pl.BlockSpec((1, hd), lambda b, i, h: (0, h)),
        pl.BlockSpec((1, n_h, 1, hd), lambda b, i, h: (b, 0, 0, 0)),
    ]
    args = [proj, proj, proj, perm, conv_w, conv_b.reshape(1, r), wg, bg, lam.reshape(1, r), h0]
    assert conv_w.shape[0] == 4 and tm // SUBLANES >= 2, "window below covers taps at time offsets -1 .. +2"
    scratch = [
        pltpu.VMEM((tm + 3 * SUBLANES, hd), F32),
        pltpu.VMEM((tm, hd), F32),
        pltpu.VMEM((tm, hd), F32),
        pltpu.VMEM((n_h, 1, hd), F32),
    ]
    hlast_spec = pl.BlockSpec((1, n_h, 1, hd), lambda b, i, h: (b, 0, 0, 0))
    hlast_shape = jax.ShapeDtypeStruct((nb, n_h, 1, hd), F32)
    if readout is None:
        out_specs = [pl.BlockSpec((1, tm, hd), lambda b, i, h: (b, tix(i), h)), hlast_spec]
        out_shape = [jax.ShapeDtypeStruct((nb, s, r), h_dtype), hlast_shape]
    else:
        h_other, w_out, b_out, x, mod, ln_g, ln_b = readout
        d = x.shape[-1]
        in_specs += [
            pl.BlockSpec((1, tm, hd), lambda b, i, h: (b, tix(i), h)),
            pl.BlockSpec((1, tm, hd), lambda b, i, h: (b, tix(i), h)),
            pl.BlockSpec((hd, d), lambda b, i, h: (h, 0)),
            pl.BlockSpec((1, d), lambda b, i, h: (0, 0)),
            pl.BlockSpec((1, tm, d), lambda b, i, h: (b, tix(i), 0)),
            pl.BlockSpec((1, 1, 1, d), lambda b, i, h: (2, b, 0, 0)),
            pl.BlockSpec((1, d), lambda b, i, h: (0, 0)),
            pl.BlockSpec((1, d), lambda b, i, h: (0, 0)),
        ]
        args += [h_other, proj, w_out, b_out.reshape(1, d), x, mod, ln_g.reshape(1, d), ln_b.reshape(1, d)]
        assert (tm // SUBLANES) % SUBLANES == 0, "readout un-interleaves 8 vreg rows at a time"
        scratch += [pltpu.VMEM((d // LANES, tm, LANES), F32)]
        out_specs = [pl.BlockSpec((1, tm, d), lambda b, i, h: (b, tix(i), 0)), hlast_spec]
        out_shape = [jax.ShapeDtypeStruct((nb, s, d), F32), hlast_shape]
    kern = functools.partial(_rglru_kernel, reverse=reverse, readout=readout is not None,
                             n_t=n_t, tm=tm, alpha=alpha)
    return pl.pallas_call(
        kern,
        grid=(nb, n_t, n_h),
        in_specs=in_specs,
        out_specs=out_specs,
        out_shape=out_shape,
        scratch_shapes=scratch,
        compiler_params=_cparams(3),
        name="rglru_bwd_readout" if readout is not None else ("rglru_bwd" if reverse else "rglru_fwd"),
    )(*args)


def _glu_kernel(x_ref, sh_ref, sc_ref, wv_ref, wg_ref, bv_ref, bgate_ref, o_ref):
    h = (x_ref[0] * (1.0 + sc_ref[0, 0]) + sh_ref[0, 0]).astype(BF16)
    val = _bdot(h, wv_ref[...]) + bv_ref[...]
    gate = _bdot(h, wg_ref[...]) + bgate_ref[...]
    o_ref[0] = (val * _sigmoid(gate)).astype(o_ref.dtype)


def _modulated_glu(x, mod, w_bf16, bias):
    nb, s, d = x.shape
    n = w_bf16.shape[1] // 2
    tm = _tile(TM_PROJ, s)
    tn = _tile(1024, n)
    nj = n // tn
    b2 = bias.reshape(1, 2 * n)
    return pl.pallas_call(
        _glu_kernel,
        grid=(nj, nb, s // tm),
        in_specs=[
            pl.BlockSpec((1, tm, d), lambda j, b, i: (b, i, 0)),
            pl.BlockSpec((1, 1, 1, d), lambda j, b, i: (0, b, 0, 0)),
            pl.BlockSpec((1, 1, 1, d), lambda j, b, i: (1, b, 0, 0)),
            pl.BlockSpec((d, tn), lambda j, b, i: (0, j)),
            pl.BlockSpec((d, tn), lambda j, b, i: (0, nj + j)),
            pl.BlockSpec((1, tn), lambda j, b, i: (0, j)),
            pl.BlockSpec((1, tn), lambda j, b, i: (0, nj + j)),
        ],
        out_specs=pl.BlockSpec((1, tm, tn), lambda j, b, i: (b, i, j)),
        out_shape=jax.ShapeDtypeStruct((nb, s, n), BF16),
        compiler_params=_cparams(3),
        name="modulated_glu",
    )(x, mod, mod, w_bf16, w_bf16, b2, b2)


def _conformer_tail_kernel(cur_ref, prev_ref, next_ref, wdw_ref, bdw_ref, ng_ref, nb_ref, w2_ref, b2_ref,
                           x_ref, g1_ref, lng_ref, lnb_ref, xo_ref, win_ref, z_ref, shift_ref, *, n_t, tm, alpha,
                           lane_chunk, row_chunk):
    i = pl.program_id(1)
    d = cur_ref.shape[-1]
    halo = BF16_ROWS
    win_ref[0:halo, :] = jnp.where(i == 0, 0.0, prev_ref[0].astype(F32))
    win_ref[halo:halo + tm, :] = cur_ref[0].astype(F32)
    win_ref[halo + tm:, :] = jnp.where(i == n_t - 1, 0.0, next_ref[0].astype(F32))
    n_taps = wdw_ref.shape[0]
    base = halo - (n_taps - 1) // 2

    n_shift = shift_ref.shape[1]
    for c in range(d // lane_chunk):
        lanes = slice(c * lane_chunk, (c + 1) * lane_chunk)
        for s in range(SUBLANES):
            shift_ref[s] = win_ref[pl.ds(s, n_shift), lanes]

        def rows_step(rc, _, lanes=lanes):
            r0 = pl.multiple_of(rc * row_chunk, row_chunk)
            acc = jnp.zeros((row_chunk, lane_chunk), F32) + bdw_ref[:, lanes]
            for k in range(n_taps):
                q, s = divmod(base + k, SUBLANES)
                rows = pl.ds(pl.multiple_of(r0 + q * SUBLANES, SUBLANES), row_chunk)
                acc = acc + wdw_ref[k:k + 1, lanes] * shift_ref[s, rows, :]
            z_ref[pl.ds(r0, row_chunk), lanes] = acc
            return 0

        lax.fori_loop(0, tm // row_chunk, rows_step, 0)

    z = _layer_norm(z_ref[...], ng_ref[...], nb_ref[...])
    y = _bdot(_silu(z).astype(BF16), w2_ref[...]) + b2_ref[...]
    v = alpha * x_ref[0] + g1_ref[0, 0] * y
    xo_ref[0] = _layer_norm(v, lng_ref[...], lnb_ref[...])


def _conformer_tail(z, w_dw, b_dw, n_g, n_b, w2_bf16, b2, x, mod, ln_g, ln_b, alpha):
    nb, s, d = z.shape
    tm = _tile(TM_CONV, s)
    n_t = s // tm
    hb = tm // BF16_ROWS
    n_hb = s // BF16_ROWS
    lane_chunk = min(256, d)
    n_taps = w_dw.shape[0]
    last_tap_row = BF16_ROWS + (n_taps - 1) // 2
    n_shift = tm + last_tap_row // SUBLANES * SUBLANES
    assert n_shift + SUBLANES - 1 <= tm + 2 * BF16_ROWS and (n_taps - 1) // 2 <= BF16_ROWS
    kern = functools.partial(_conformer_tail_kernel, n_t=n_t, tm=tm, alpha=alpha,
                             lane_chunk=lane_chunk, row_chunk=min(32, tm))
    vec = lambda a: a.reshape(1, d)
    vspec = pl.BlockSpec((1, d), lambda b, i: (0, 0))
    return pl.pallas_call(
        kern,
        grid=(nb, n_t),
        in_specs=[
            pl.BlockSpec((1, tm, d), lambda b, i: (b, i, 0)),
            pl.BlockSpec((1, BF16_ROWS, d), lambda b, i: (b, jnp.maximum(i * hb - 1, 0), 0)),
            pl.BlockSpec((1, BF16_ROWS, d), lambda b, i: (b, jnp.minimum((i + 1) * hb, n_hb - 1), 0)),
            pl.BlockSpec((w_dw.shape[0], d), lambda b, i: (0, 0)),
            vspec, vspec, vspec,
            pl.BlockSpec((d, d), lambda b, i: (0, 0)),
            vspec,
            pl.BlockSpec((1, tm, d), lambda b, i: (b, i, 0)),
            pl.BlockSpec((1, 1, 1, d), lambda b, i: (2, b, 0, 0)),
            vspec, vspec,
        ],
        out_specs=pl.BlockSpec((1, tm, d), lambda b, i: (b, i, 0)),
        out_shape=jax.ShapeDtypeStruct((nb, s, d), F32),
        scratch_shapes=[pltpu.VMEM((tm + 2 * BF16_ROWS, d), F32), pltpu.VMEM((tm, d), F32),
                        pltpu.VMEM((SUBLANES, n_shift, lane_chunk), F32)],
        compiler_params=_cparams(2),
        name="conformer_tail",
    )(z, z, z, w_dw, vec(b_dw), vec(n_g), vec(n_b), w2_bf16, vec(b2), x, mod, vec(ln_g), vec(ln_b))


def _router_kernel(x_ref, sh_ref, sc_ref, whi_ref, wlo_ref, bc_ref,
                   idx_ref, wt_ref, rank_ref, cnt_ref, carry_ref, *, n_exp, tm):
    i = pl.program_id(0)

    @pl.when(i == 0)
    def _():
        carry_ref[...] = jnp.zeros_like(carry_ref)

    tok = x_ref[...] * (1.0 + sc_ref[0, 0]) + sh_ref[0, 0]
    t_hi = tok.astype(BF16)
    t_lo = (tok - t_hi.astype(F32)).astype(BF16)
    nt = (((1,), (1,)), ((), ()))
    logits = (lax.dot_general(whi_ref[...], t_hi, nt, preferred_element_type=F32)
              + lax.dot_general(whi_ref[...], t_lo, nt, preferred_element_type=F32)
              + lax.dot_general(wlo_ref[...], t_hi, nt, preferred_element_type=F32))
    scores = _sigmoid(logits)
    biased = scores + bc_ref[...]

    per_group = n_exp // N_GROUPS
    sub = lax.broadcasted_iota(jnp.int32, (per_group, tm), 0)
    neg = -jnp.inf
    grp = [biased[g * per_group:(g + 1) * per_group, :] for g in range(N_GROUPS)]
    sc_g = [scores[g * per_group:(g + 1) * per_group, :] for g in range(N_GROUPS)]

    gscore = []
    for g in range(N_GROUPS):
        m1 = jnp.max(grp[g], axis=0, keepdims=True)
        first = jnp.min(jnp.where(grp[g] == m1, sub, per_group), axis=0, keepdims=True)
        m2 = jnp.max(jnp.where(sub == first, neg, grp[g]), axis=0, keepdims=True)
        gscore.append(m1 + m2)
    masked = []
    for g in range(N_GROUPS):
        beaten = jnp.zeros((1, tm), jnp.int32)
        for o in range(N_GROUPS):
            if o == g:
                continue
            wins = (gscore[o] > gscore[g]) | ((gscore[o] == gscore[g]) & (o < g))
            beaten = beaten + wins.astype(jnp.int32)
        masked.append(jnp.where(beaten < TOPK_GROUPS, grp[g], neg))

    eid = [sub + g * per_group for g in range(N_GROUPS)]
    member = [jnp.zeros((per_group, tm), F32) for _ in range(N_GROUPS)]
    idx_rows, score_rows = [], []
    for _ in range(TOP_K):
        m = masked[0]
        for g in range(1, N_GROUPS):
            m = jnp.maximum(m, masked[g])
        m = jnp.max(m, axis=0, keepdims=True)
        cand = jnp.where(masked[0] == m, eid[0], n_exp)
        for g in range(1, N_GROUPS):
            cand = jnp.minimum(cand, jnp.where(masked[g] == m, eid[g], n_exp))
        first = jnp.min(cand, axis=0, keepdims=True)
        picked = jnp.zeros((per_group, tm), F32)
        for g in range(N_GROUPS):
            sel = eid[g] == first
            picked = picked + jnp.where(sel, sc_g[g], 0.0)
            member[g] = jnp.where(sel, 1.0, member[g])
            masked[g] = jnp.where(sel, neg, masked[g])
        idx_rows.append(first)
        score_rows.append(jnp.sum(picked, axis=0, keepdims=True))

    total = score_rows[0]
    for k in range(1, TOP_K):
        total = total + score_rows[k]

    memb = jnp.concatenate(member, axis=0)
    earlier = jnp.where(lax.broadcasted_iota(jnp.int32, (tm, tm), 0)
                        < lax.broadcasted_iota(jnp.int32, (tm, tm), 1), 1.0, 0.0).astype(BF16)
    pos = _bdot(memb.astype(BF16), earlier) + carry_ref[...]
    pos_g = [pos[g * per_group:(g + 1) * per_group, :] for g in range(N_GROUPS)]
    for k in range(TOP_K):
        acc = jnp.zeros((per_group, tm), F32)
        for g in range(N_GROUPS):
            acc = acc + jnp.where(eid[g] == idx_rows[k], pos_g[g], 0.0)
        rank_ref[k:k + 1, :] = jnp.sum(acc, axis=0, keepdims=True).astype(jnp.int32)
        idx_ref[k:k + 1, :] = idx_rows[k]
        wt_ref[k:k + 1, :] = score_rows[k] / total * ROUTED_SCALE
    carry_ref[...] += jnp.sum(memb, axis=1, keepdims=True)
    cnt_ref[...] = carry_ref[...].astype(jnp.int32)


def _route(x, mod, w_router, b_corr):
    nb, s, d = x.shape
    t = nb * s
    n_exp = w_router.shape[1]
    tm = _tile(TM_ROUTE, s)
    per_b = s // tm
    w_t = w_router.T
    w_hi = w_t.astype(BF16)
    w_lo = (w_t - w_hi.astype(F32)).astype(BF16)
    kern = functools.partial(_router_kernel, n_exp=n_exp, tm=tm)
    out_spec = pl.BlockSpec((TOP_K, tm), lambda i: (0, i))
    return pl.pallas_call(
        kern,
        grid=(t // tm,),
        in_specs=[
            pl.BlockSpec((tm, d), lambda i: (i, 0)),
            pl.BlockSpec((1, 1, 1, d), lambda i: (3, i // per_b, 0, 0)),
            pl.BlockSpec((1, 1, 1, d), lambda i: (4, i // per_b, 0, 0)),
            pl.BlockSpec((n_exp, d), lambda i: (0, 0)),
            pl.BlockSpec((n_exp, d), lambda i: (0, 0)),
            pl.BlockSpec((n_exp, 1), lambda i: (0, 0)),
        ],
        out_specs=[out_spec, out_spec, out_spec, pl.BlockSpec((n_exp, 1), lambda i: (0, 0))],
        out_shape=[
            jax.ShapeDtypeStruct((TOP_K, t), jnp.int32),
            jax.ShapeDtypeStruct((TOP_K, t), F32),
            jax.ShapeDtypeStruct((TOP_K, t), jnp.int32),
            jax.ShapeDtypeStruct((n_exp, 1), jnp.int32),
        ],
        scratch_shapes=[pltpu.VMEM((n_exp, 1), F32)],
        compiler_params=_cparams(1),
        name="moe_router",
    )(x.reshape(t, d), mod, mod, w_hi, w_lo, b_corr.reshape(n_exp, 1).astype(F32))


def _slots_kernel(ps_ref, idx_ref, rank_ref, dest_ref, *, n_exp):
    idx = idx_ref[...]

    def add_start(e, acc):
        return acc + jnp.where(idx == e, ps_ref[e], 0)

    dest_ref[...] = lax.fori_loop(0, n_exp, add_start, rank_ref[...])


def _slots(idx, rank, pad_starts):
    k, t = idx.shape
    tm = _tile(2048, t)
    grid_spec = pltpu.PrefetchScalarGridSpec(
        num_scalar_prefetch=1,
        grid=(t // tm,),
        in_specs=[pl.BlockSpec((k, tm), lambda i, ps: (0, i)), pl.BlockSpec((k, tm), lambda i, ps: (0, i))],
        out_specs=pl.BlockSpec((k, tm), lambda i, ps: (0, i)),
    )
    return pl.pallas_call(
        functools.partial(_slots_kernel, n_exp=pad_starts.shape[0]),
        grid_spec=grid_spec,
        out_shape=jax.ShapeDtypeStruct((k, t), jnp.int32),
        compiler_params=_cparams(1),
        name="moe_slots",
    )(pad_starts, idx, rank)


def _to_slabs(value):
    n_seg = value.shape[1] // LANES
    by_seg = jnp.stack([value[:, s * LANES:(s + 1) * LANES] for s in range(n_seg)], axis=0)
    return jnp.swapaxes(by_seg, 0, 1)


def _from_slabs(slabs):
    by_seg = jnp.swapaxes(slabs, 0, 1)
    return jnp.concatenate([by_seg[s] for s in range(slabs.shape[1])], axis=1)


def _dispatch_kernel(pend_ref, padded_ref, x_ref, sh_ref, sc_ref, dest_ref, xs_ref,
                     tok_ref, zero_ref, sem, zsem, *, n_exp, tm, bm):
    i = pl.program_id(0)

    def zero_copy(e):
        return pltpu.make_async_copy(zero_ref, xs_ref.at[pl.ds(pend_ref[e] - bm, bm)], zsem)

    @pl.when(i == 0)
    def _():
        zero_ref[...] = jnp.zeros_like(zero_ref)

        def start(e, _):
            @pl.when(padded_ref[e] > 0)
            def _():
                zero_copy(e).start()
            return 0

        def wait(e, _):
            @pl.when(padded_ref[e] > 0)
            def _():
                zero_copy(e).wait()
            return 0

        def tail_copy(blk):
            return pltpu.make_async_copy(zero_ref, xs_ref.at[pl.ds(blk * bm, bm)], zsem)

        def start_tail(blk, _):
            tail_copy(blk).start()
            return 0

        def wait_tail(blk, _):
            tail_copy(blk).wait()
            return 0

        first_unused = pend_ref[n_exp - 1] // bm
        n_blk = xs_ref.shape[0] // bm
        lax.fori_loop(0, n_exp, start, 0)
        lax.fori_loop(first_unused, n_blk, start_tail, 0)
        lax.fori_loop(0, n_exp, wait, 0)
        lax.fori_loop(first_unused, n_blk, wait_tail, 0)

    n_steps = pl.num_programs(0)
    slot = i % 2

    def drain(buf):
        for k in range(TOP_K):
            pltpu.make_async_copy(tok_ref.at[buf], xs_ref.at[pl.ds(0, tm)], sem.at[buf]).wait()

    @pl.when(i >= 2)
    def _():
        drain(slot)

    tok_ref[slot] = _to_slabs(x_ref[...] * (1.0 + sc_ref[0, 0]) + sh_ref[0, 0])

    def issue(r, _):
        for k in range(TOP_K):
            pltpu.make_async_copy(tok_ref.at[slot, r], xs_ref.at[dest_ref[k, r]],
                                  sem.at[slot]).start(priority=k % 2)
        return 0

    lax.fori_loop(0, tm, issue, 0)

    @pl.when(i == n_steps - 1)
    def _():
        @pl.when(i >= 1)
        def _():
            drain(1 - slot)
        drain(slot)


def _dispatch(x, mod, dest, pad_ends, padded, n_slots, bm):
    nb, s, d = x.shape
    t = nb * s
    tm = _tile(TM_DISPATCH, s)
    per_b = s // tm
    n_exp = pad_ends.shape[0]
    n_seg = d // LANES
    assert n_seg % SUBLANES == 0, "token slabs must be whole (8, 128) tiles"
    kern = functools.partial(_dispatch_kernel, n_exp=n_exp, tm=tm, bm=bm)
    grid_spec = pltpu.PrefetchScalarGridSpec(
        num_scalar_prefetch=2,
        grid=(t // tm,),
        in_specs=[
            pl.BlockSpec((tm, d), lambda i, pe, pd: (i, 0)),
            pl.BlockSpec((1, 1, 1, d), lambda i, pe, pd: (3, i // per_b, 0, 0)),
            pl.BlockSpec((1, 1, 1, d), lambda i, pe, pd: (4, i // per_b, 0, 0)),
            pl.BlockSpec((TOP_K, tm), lambda i, pe, pd: (0, i), memory_space=pltpu.SMEM),
        ],
        out_specs=pl.BlockSpec(memory_space=pl.ANY),
        scratch_shapes=[
            pltpu.VMEM((2, tm, n_seg, LANES), F32),
            pltpu.VMEM((bm, n_seg, LANES), F32),
            pltpu.SemaphoreType.DMA((2,)),
            pltpu.SemaphoreType.DMA(()),
        ],
    )
    return pl.pallas_call(
        kern,
        grid_spec=grid_spec,
        out_shape=jax.ShapeDtypeStruct((n_slots, n_seg, LANES), F32),
        compiler_params=_cparams(1),
        name="moe_dispatch",
    )(pad_ends, padded, x.reshape(t, d), mod, mod, dest)


def _cast_kernel(*refs):
    n = len(refs) // 2
    for src, dst in zip(refs[:n], refs[n:]):
        dst[0] = src[0, 0].astype(dst.dtype)


def _layer_weights_bf16(layer, *stacks):
    n_exp = stacks[0].shape[1]
    in_specs = [pl.BlockSpec((1, 1) + w.shape[2:], lambda e: (layer, e, 0, 0)) for w in stacks]
    out_specs = [pl.BlockSpec((1,) + w.shape[2:], lambda e: (e, 0, 0)) for w in stacks]
    return pl.pallas_call(
        _cast_kernel,
        grid=(n_exp,),
        in_specs=in_specs,
        out_specs=out_specs,
        out_shape=[jax.ShapeDtypeStruct(w.shape[1:], BF16) for w in stacks],
        compiler_params=_cparams(1),
        name="expert_weights_bf16",
    )(*stacks)


def _experts_kernel(be_ref, nu_ref, xs_ref, wg_ref, wu_ref, wd_ref, ys_ref):
    b = pl.program_id(0)

    @pl.when(b < nu_ref[0])
    def _():
        x = _from_slabs(xs_ref[...]).astype(BF16)
        g = _bdot(x, wg_ref[0])
        u = _bdot(x, wu_ref[0])
        ys_ref[...] = _to_slabs(_bdot((_silu(g) * u).astype(BF16), wd_ref[0]))

    @pl.when(b >= nu_ref[0])
    def _():
        ys_ref[...] = jnp.zeros_like(ys_ref)


def _grouped_experts(xs, blk_e, n_used, w_gate, w_up, w_down, bm):
    d, f = w_gate.shape[1:]
    n_seg = d // LANES
    n_blk = xs.shape[0] // bm

    def blk(b, be, nu):
        return jnp.minimum(b, nu[0] - 1)

    grid_spec = pltpu.PrefetchScalarGridSpec(
        num_scalar_prefetch=2,
        grid=(n_blk,),
        in_specs=[
            pl.BlockSpec((bm, n_seg, LANES), lambda b, be, nu: (blk(b, be, nu), 0, 0)),
            pl.BlockSpec((1, d, f), lambda b, be, nu: (be[blk(b, be, nu)], 0, 0)),
            pl.BlockSpec((1, d, f), lambda b, be, nu: (be[blk(b, be, nu)], 0, 0)),
            pl.BlockSpec((1, f, d), lambda b, be, nu: (be[blk(b, be, nu)], 0, 0)),
        ],
        out_specs=pl.BlockSpec((bm, n_seg, LANES), lambda b, be, nu: (b, 0, 0)),
    )
    return pl.pallas_call(
        _experts_kernel,
        grid_spec=grid_spec,
        out_shape=jax.ShapeDtypeStruct(xs.shape, F32),
        compiler_params=_cparams(1),
        name="moe_experts",
    )(blk_e, n_used, xs, w_gate, w_up, w_down)


def _combine_kernel(x_ref, sh_ref, sc_ref, gt_ref, dest_ref, dest_next_ref, wt_ref, ys_ref,
                    wsg_ref, wsu_ref, wsd_ref, lng_ref, lnb_ref, xo_ref, rows_ref, sem, *, tm, alpha):
    i = pl.program_id(0)
    n_steps = pl.num_programs(0)
    slot = i % 2

    def gather(d_ref, buf):
        def issue(r, _):
            for k in range(TOP_K):
                pltpu.make_async_copy(ys_ref.at[d_ref[k, r]], rows_ref.at[buf, k, r],
                                      sem.at[buf]).start(priority=k % 2)
            return 0
        lax.fori_loop(0, tm, issue, 0)

    @pl.when(i == 0)
    def _():
        gather(dest_ref, 0)

    for r in range(tm):
        for k in range(TOP_K):
            pltpu.make_async_copy(ys_ref.at[dest_next_ref[k, r]], rows_ref.at[1 - slot, k, r],
                                  sem.at[1 - slot]).start(priority=k % 2)

    x = x_ref[...]
    tok = (x * (1.0 + sc_ref[0, 0]) + sh_ref[0, 0]).astype(BF16)
    hid = (_silu(_bdot(tok, wsg_ref[...])) * _bdot(tok, wsu_ref[...])).astype(BF16)
    f = _bdot(hid, wsd_ref[...])

    for k in range(TOP_K):
        pltpu.make_async_copy(ys_ref.at[pl.ds(0, tm)], rows_ref.at[slot, k], sem.at[slot]).wait()
    n_seg = rows_ref.shape[3]
    wt = wt_ref[...]
    routed = jnp.broadcast_to(wt[:, 0:1, :], (tm, n_seg, LANES)) * rows_ref[slot, 0]
    for k in range(1, TOP_K):
        routed = routed + jnp.broadcast_to(wt[:, k:k + 1, :], (tm, n_seg, LANES)) * rows_ref[slot, k]
    f = f + _from_slabs(routed)
    v = alpha * x + gt_ref[0, 0] * f
    xo_ref[...] = _layer_norm(v, lng_ref[...], lnb_ref[...])

    @pl.when(i == n_steps - 1)
    def _():
        for k in range(TOP_K):
            pltpu.make_async_copy(ys_ref.at[pl.ds(0, tm)], rows_ref.at[1 - slot, k], sem.at[1 - slot]).wait()


def _combine(x, mod, dest, wt_rows, ys, ws_gate, ws_up, ws_down, ln_g, ln_b, alpha):
    nb, s, d = x.shape
    t = nb * s
    tm = _tile(TM_COMBINE, s)
    per_b = s // tm
    f = ws_gate.shape[-1]
    n_seg = d // LANES
    kern = functools.partial(_combine_kernel, tm=tm, alpha=alpha)
    vspec = pl.BlockSpec((1, d), lambda i: (0, 0))
    n_steps = t // tm
    out = pl.pallas_call(
        kern,
        grid=(n_steps,),
        in_specs=[
            pl.BlockSpec((tm, d), lambda i: (i, 0)),
            pl.BlockSpec((1, 1, 1, d), lambda i: (3, i // per_b, 0, 0)),
            pl.BlockSpec((1, 1, 1, d), lambda i: (4, i // per_b, 0, 0)),
            pl.BlockSpec((1, 1, 1, d), lambda i: (5, i // per_b, 0, 0)),
            pl.BlockSpec((TOP_K, tm), lambda i: (0, i), memory_space=pltpu.SMEM),
            pl.BlockSpec((TOP_K, tm), lambda i: (0, jnp.minimum(i + 1, n_steps - 1)), memory_space=pltpu.SMEM),
            pl.BlockSpec((tm, TOP_K, LANES), lambda i: (i, 0, 0)),
            pl.BlockSpec(memory_space=pl.ANY),
            pl.BlockSpec((d, f), lambda i: (0, 0)),
            pl.BlockSpec((d, f), lambda i: (0, 0)),
            pl.BlockSpec((f, d), lambda i: (0, 0)),
            vspec, vspec,
        ],
        out_specs=pl.BlockSpec((tm, d), lambda i: (i, 0)),
        out_shape=jax.ShapeDtypeStruct((t, d), F32),
        scratch_shapes=[pltpu.VMEM((2, TOP_K, tm, n_seg, LANES), F32), pltpu.SemaphoreType.DMA((2,))],
        compiler_params=_cparams(1),
        name="moe_combine",
    )(x.reshape(t, d), mod, mod, mod, dest, dest, wt_rows, ys, ws_gate, ws_up, ws_down,
      ln_g.reshape(1, d), ln_b.reshape(1, d))
    return out.reshape(nb, s, d)


def _moe_block(x, mod, w_router, b_router, w_gate, w_up, w_down, ws_gate, ws_up, ws_down, ln_g, ln_b, alpha):
    nb, s, d = x.shape
    t = nb * s
    n_exp = w_router.shape[1]
    bm = _tile(MOE_BLOCK, t * TOP_K)
    idx, wt, rank, counts = _route(x, mod, w_router, b_router)

    counts = counts.reshape(n_exp)
    padded = (counts + bm - 1) // bm * bm
    pad_ends = jnp.cumsum(padded).astype(jnp.int32)
    pad_starts = pad_ends - padded
    dest = _slots(idx, rank, pad_starts.astype(jnp.int32))
    n_blk = t * TOP_K // bm + n_exp
    n_used = (pad_ends[-1:] // bm).astype(jnp.int32)
    blk_start = jnp.arange(n_blk, dtype=jnp.int32) * bm
    blk_e = jnp.sum((pad_ends[None, :] <= blk_start[:, None]).astype(jnp.int32), axis=1)
    blk_e = jnp.minimum(blk_e, n_exp - 1).astype(jnp.int32)

    xs = _dispatch(x, mod, dest, pad_ends, padded.astype(jnp.int32), n_blk * bm, bm)
    ys = _grouped_experts(xs, blk_e, n_used, w_gate, w_up, w_down, bm)
    wt_splat = jnp.broadcast_to(wt.T[:, :, None], (t, TOP_K, LANES))
    return _combine(x, mod, dest, wt_splat, ys,ws_gate.astype(BF16), ws_up.astype(BF16), ws_down.astype(BF16),
                    ln_g, ln_b, alpha)


def kernel(x, c, ctx, c_ctx, w_mod, b_mod, ln1_g, ln1_b, ln2_g, ln2_b, rg_w_in, rg_b_in, rg_conv_w, rg_conv_b, rg_w_gates, rg_b_gates, rg_lambda, rg_w_out, rg_b_out, cf_w_pw1, cf_b_pw1, cf_w_dw, cf_b_dw, cf_norm_g, cf_norm_b, cf_w_pw2, cf_b_pw2, moe_w_router, moe_b_router, moe_w_gate, moe_w_up, moe_w_down, sh_w_gate, sh_w_up, sh_w_down):
    nb, s, d = x.shape
    depth = w_mod.shape[0]
    assert depth == 2 and nb + 1 <= SUBLANES
    alpha = (2 * depth) ** 0.25

    c_rows = jnp.zeros((SUBLANES, d), F32).at[:nb].set(c).at[nb].set(c_ctx)
    mod_all = _adaln_vectors(c_rows, w_mod, b_mod)
    mod_all = mod_all.reshape(depth, SUBLANES, 6, 1, d).transpose(0, 2, 1, 3, 4)
    mod_lat = [mod_all[l, :, :nb] for l in range(depth)]
    mod_ctx = jnp.broadcast_to(mod_all[0, :, nb:nb + 1], (6, nb, 1, d))

    n_h, hd = rg_w_gates.shape[3], rg_w_gates.shape[4]
    r = n_h * hd
    w_in = rg_w_in[0].astype(BF16)
    wg = [jnp.concatenate([rg_w_gates[0, dr, 0], rg_w_gates[0, dr, 1]], axis=-1).astype(BF16) for dr in range(2)]
    bg = [jnp.concatenate([rg_b_gates[0, dr, 0], rg_b_gates[0, dr, 1]], axis=-1).reshape(n_h, 1, 2 * hd)
          for dr in range(2)]
    conv_w, conv_b = rg_conv_w[0], rg_conv_b[0]
    zero_state = jnp.zeros((nb, n_h, 1, hd), F32)

    proj_c = _modulated_projection(ctx, mod_ctx, w_in, rg_b_in[0], BF16)
    _, hcf = _rglru_scan(proj_c, conv_w, conv_b, wg[0], bg[0], rg_lambda[0, 0], zero_state, reverse=False)
    _, hcb = _rglru_scan(proj_c, conv_w, conv_b, wg[1], bg[1], rg_lambda[0, 1], zero_state, reverse=True)

    proj_x = _modulated_projection(x, mod_lat[0], w_in, rg_b_in[0], BF16)
    hxf, _ = _rglru_scan(proj_x, conv_w, conv_b, wg[0], bg[0], rg_lambda[0, 0], hcf, reverse=False)
    readout = (hxf, rg_w_out[0].astype(BF16), rg_b_out[0], x, mod_lat[0], ln1_g[0], ln1_b[0])
    x, _ = _rglru_scan(proj_x, conv_w, conv_b, wg[1], bg[1], rg_lambda[0, 1], hcb, reverse=True,
                       readout=readout, alpha=alpha)
    x = _moe_block(x, mod_lat[0], moe_w_router[0], moe_b_router[0],
                   *_layer_weights_bf16(0, moe_w_gate, moe_w_up, moe_w_down),
                   sh_w_gate[0], sh_w_up[0], sh_w_down[0], ln2_g[0], ln2_b[0], alpha)

    z = _modulated_glu(x, mod_lat[1], cf_w_pw1[0].astype(BF16), cf_b_pw1[0])
    x = _conformer_tail(z, cf_w_dw[0], cf_b_dw[0], cf_norm_g[0], cf_norm_b[0], cf_w_pw2[0].astype(BF16),
                        cf_b_pw2[0], x, mod_lat[1], ln1_g[1], ln1_b[1], alpha)
    x = _moe_block(x, mod_lat[1], moe_w_router[1], moe_b_router[1],
                   *_layer_weights_bf16(1, moe_w_gate, moe_w_up, moe_w_down),
                   sh_w_gate[1], sh_w_up[1], sh_w_down[1], ln2_g[1], ln2_b[1], alpha)
    return x
```

```python
import functools

import jax
import jax.numpy as jnp
from jax import lax
from jax.experimental import pallas as pl
from jax.experimental.pallas import tpu as pltpu

F32 = jnp.float32
BF16 = jnp.bfloat16

RG_C = 8.0
N_GROUPS = 8
TOPK_GROUPS = 4
TOP_K = 8
ROUTED_SCALE = 2.5
LN_EPS = 1e-5

VMEM_LIMIT_BYTES = 56 * 1024 * 1024
SUBLANES = 8
LANES = 128
BF16_ROWS = 16

TM_PROJ = 512
TM_SCAN = 512
TM_CONV = 256
TM_ROUTE = 512
TM_DISPATCH = 256
TM_COMBINE = 128
MOE_BLOCK = 512


def _tile(default, n):
    t = min(default, n)
    assert n % t == 0, (default, n)
    return t


def _cparams(n_axes):
    return pltpu.CompilerParams(
        dimension_semantics=("arbitrary",) * n_axes,
        vmem_limit_bytes=VMEM_LIMIT_BYTES,
    )


def _sigmoid(v):
    return 0.5 * jnp.tanh(0.5 * v) + 0.5


def _sqrt_nonneg(v):
    return jnp.where(v > 0.0, v * lax.rsqrt(v), 0.0)


def _silu(v):
    return v * _sigmoid(v)


def _gelu_tanh(v):
    c = 0.7978845608028654
    return 0.5 * v * (1.0 + jnp.tanh(c * (v + 0.044715 * (v * v * v))))


def _softplus(v):
    return jnp.maximum(v, 0.0) + jnp.log(1.0 + jnp.exp(-jnp.abs(v)))


def _layer_norm(v, g, b):
    mu = jnp.mean(v, axis=-1, keepdims=True)
    d = v - mu
    var = jnp.mean(d * d, axis=-1, keepdims=True)
    return d * lax.rsqrt(var + LN_EPS) * g + b


def _bdot(a, b):
    return jnp.dot(a, b, preferred_element_type=F32)


def _mod_kernel(c_ref, w_ref, b_ref, o_ref):
    s = _silu(c_ref[...])
    o_ref[0] = _bdot(s.astype(BF16), w_ref[0].astype(BF16)) + b_ref[0]


def _adaln_vectors(c_rows, w_mod, b_mod):
    depth, d, d6 = w_mod.shape
    tn = _tile(1024, d6)
    return pl.pallas_call(
        _mod_kernel,
        grid=(depth, d6 // tn),
        in_specs=[
            pl.BlockSpec((SUBLANES, d), lambda l, n: (0, 0)),
            pl.BlockSpec((1, d, tn), lambda l, n: (l, 0, n)),
            pl.BlockSpec((1, 1, tn), lambda l, n: (l, 0, n)),
        ],
        out_specs=pl.BlockSpec((1, SUBLANES, tn), lambda l, n: (l, 0, n)),
        out_shape=jax.ShapeDtypeStruct((depth, SUBLANES, d6), F32),
        compiler_params=_cparams(2),
        name="adaln_vectors",
    )(c_rows, w_mod, b_mod.reshape(depth, 1, d6))


def _mod_spec(chunk, batch_of):
    def index(*ids):
        return (chunk, batch_of(*ids), 0, 0)
    return index


class _SideCast:
    def __init__(self, stacks, layer, grid):
        self.n = len(stacks)
        self.shapes = [w.shape[1:] for w in stacks]
        n_steps = 1
        for g in grid:
            n_steps *= g

        def step(*ids):
            lin = ids[0]
            for g, v in zip(grid[1:], ids[1:]):
                lin = lin * g + v
            return lin

        self.in_specs, self.out_specs, self.out_shape, self.args = [], [], [], []
        for w in stacks:
            n_l, n_e, a, b = w.shape
            rows = n_e * a
            blk = rows // n_steps
            assert rows % n_steps == 0 and blk % BF16_ROWS == 0, (w.shape, n_steps)
            self.in_specs.append(pl.BlockSpec((1, blk, b), lambda *ids: (layer, step(*ids), 0)))
            self.out_specs.append(pl.BlockSpec((blk, b), lambda *ids: (step(*ids), 0)))
            self.out_shape.append(jax.ShapeDtypeStruct((rows, b), BF16))
            self.args.append(w.reshape(n_l, rows, b))

    @staticmethod
    def run(src_refs, dst_refs):
        for src, dst in zip(src_refs, dst_refs):
            dst[...] = src[0].astype(dst.dtype)

    def unpack(self, outs):
        return [o.reshape(shape) for o, shape in zip(outs, self.shapes)]


def _proj_kernel(x_ref, sh_ref, sc_ref, w_ref, b_ref, *rest, n_side):
    o_ref = rest[n_side]
    h = x_ref[0] * (1.0 + sc_ref[0, 0]) + sh_ref[0, 0]
    o_ref[0] = (_bdot(h.astype(BF16), w_ref[...]) + b_ref[...]).astype(o_ref.dtype)
    _SideCast.run(rest[:n_side], rest[n_side + 1:])


def _modulated_projection(x, mod, w_bf16, bias, out_dtype, side_stacks=(), side_layer=0):
    nb, s, d = x.shape
    n = w_bf16.shape[1]
    tm = _tile(TM_PROJ, s)
    tn = _tile(2560, n)
    dm = mod.shape[-1]
    grid = (n // tn, nb, s // tm)
    side = _SideCast(side_stacks, side_layer, grid)
    outs = pl.pallas_call(
        functools.partial(_proj_kernel, n_side=side.n),
        grid=grid,
        in_specs=[
            pl.BlockSpec((1, tm, d), lambda j, b, i: (b, i, 0)),
            pl.BlockSpec((1, 1, 1, dm), lambda j, b, i: (0, b, 0, 0)),
            pl.BlockSpec((1, 1, 1, dm), lambda j, b, i: (1, b, 0, 0)),
            pl.BlockSpec((d, tn), lambda j, b, i: (0, j)),
            pl.BlockSpec((1, tn), lambda j, b, i: (0, j)),
        ] + side.in_specs,
        out_specs=[pl.BlockSpec((1, tm, tn), lambda j, b, i: (b, i, j))] + side.out_specs,
        out_shape=[jax.ShapeDtypeStruct((nb, s, n), out_dtype)] + side.out_shape,
        compiler_params=_cparams(3),
        name="modulated_projection",
    )(x, mod, mod, w_bf16, bias.reshape(1, n), *side.args)
    return outs[0], side.unpack(outs[1:])


def _rglru_kernel(*refs, reverse, readout, n_t, tm, alpha):
    if readout:
        (cur_ref, prev_ref, next_ref, perm_ref, cw_ref, cb_ref, wg_ref, bg_ref, lam_ref, h0_ref,
         hf_ref, gate_ref, wout_ref, bout_ref, x_ref, g1_ref, lng_ref, lnb_ref,
         xo_ref, hlast_ref,
         win_ref, a_ref, hl_ref, carry_ref, acc_ref) = refs
    else:
        (cur_ref, prev_ref, next_ref, perm_ref, cw_ref, cb_ref, wg_ref, bg_ref, lam_ref, h0_ref,
         ho_ref, hlast_ref,
         win_ref, a_ref, hl_ref, carry_ref) = refs
    i = pl.program_id(1)
    h = pl.program_id(2)
    n_h = pl.num_programs(2)
    ti = (n_t - 1 - i) if reverse else i
    hd = cur_ref.shape[-1]
    chunk = tm // SUBLANES
    sub = lax.broadcasted_iota(jnp.int32, (SUBLANES, hd), 0)

    rnn = _bdot(perm_ref[...], cur_ref[0])
    prev_last = jnp.where(ti == 0, 0.0, prev_ref[0].astype(F32)[BF16_ROWS - 1:, :])
    nxt = jnp.where(ti == n_t - 1, 0.0, next_ref[0].astype(F32)[:2, :])
    win_ref[0:SUBLANES, :] = jnp.where(
        sub == 0, prev_last, pltpu.roll(rnn[tm - SUBLANES:, :], 1, axis=0))
    win_ref[SUBLANES:SUBLANES + tm, :] = rnn
    for extra in range(2):
        lo = SUBLANES + tm + extra * SUBLANES
        win_ref[lo:lo + SUBLANES, :] = jnp.where(
            sub == SUBLANES - 1, nxt[extra:extra + 1, :],
            pltpu.roll(rnn[extra * SUBLANES:(extra + 1) * SUBLANES, :], SUBLANES - 1, axis=0))
    n_taps = cw_ref.shape[0]
    xb = cb_ref[...] + cw_ref[0:1, :] * win_ref[pl.ds(0, tm), :]
    for k in range(1, n_taps):
        xb = xb + cw_ref[k:k + 1, :] * win_ref[pl.ds(k * SUBLANES, tm), :]

    pre = _bdot(xb.astype(BF16), wg_ref[0]) + bg_ref[0]
    r = _sigmoid(pre[:, :hd])
    gi = _sigmoid(pre[:, hd:])
    log_a = (-RG_C) * r * _softplus(-lam_ref[...])
    a = jnp.exp(log_a)
    a_ref[...] = a
    hl_ref[...] = _sqrt_nonneg(1.0 - a * a) * gi * xb

    if readout:
        gate_act = _gelu_tanh(_bdot(perm_ref[...], gate_ref[0]))

    def scan_step(jj, carry):
        a_cum, h_loc = carry
        j = (chunk - 1 - jj) if reverse else jj
        rows = pl.ds(pl.multiple_of(j * SUBLANES, SUBLANES), SUBLANES)
        aj = a_ref[rows, :]
        h_loc = aj * h_loc + hl_ref[rows, :]
        a_cum = a_cum * aj
        hl_ref[rows, :] = h_loc
        a_ref[rows, :] = a_cum
        return a_cum, h_loc

    a_end, h_end = lax.fori_loop(
        0, chunk, scan_step, (jnp.ones((SUBLANES, hd), F32), jnp.zeros((SUBLANES, hd), F32)),
        unroll=SUBLANES)

    @pl.when(i == 0)
    def _():
        carry_ref[h] = h0_ref[0, h]

    state = carry_ref[h]
    entry = jnp.zeros((SUBLANES, hd), F32)
    for s in (range(SUBLANES - 1, -1, -1) if reverse else range(SUBLANES)):
        entry = jnp.where(sub == s, state, entry)
        state = a_end[s:s + 1, :] * state + h_end[s:s + 1, :]
    carry_ref[h] = state
    hlast_ref[0, h] = state

    h_full = hl_ref[...] + a_ref[...] * jnp.concatenate([entry] * chunk, axis=0)
    if not readout:
        ho_ref[0] = h_full.astype(ho_ref.dtype)
    else:
        h_sum = h_full + hf_ref[0].astype(F32)
        contrib = _bdot((h_sum * gate_act).astype(BF16), wout_ref[...])
        n_lc = acc_ref.shape[0]

        @pl.when(h == 0)
        def _():
            for c in range(n_lc):
                acc_ref[c] = contrib[:, c * LANES:(c + 1) * LANES]

        @pl.when(h != 0)
        def _():
            for c in range(n_lc):
                acc_ref[c] += contrib[:, c * LANES:(c + 1) * LANES]

        @pl.when(h == n_h - 1)
        def _():
            y = jnp.concatenate(
                [jnp.concatenate(
                    [acc_ref[c, pl.ds(q * SUBLANES * SUBLANES + s, SUBLANES, stride=SUBLANES), :]
                     for s in range(SUBLANES) for q in range(chunk // SUBLANES)], axis=0)
                 for c in range(n_lc)], axis=1) + bout_ref[...]
            v = alpha * x_ref[0] + g1_ref[0, 0] * y
            xo_ref[0] = _layer_norm(v, lng_ref[...], lnb_ref[...])


def _rglru_scan(proj, conv_w, conv_b, wg, bg, lam, h0, *, reverse, h_dtype=BF16, readout=None, alpha=None):
    nb, s, r2 = proj.shape
    r = r2 // 2
    n_h, hd = wg.shape[0], wg.shape[1]
    tm = _tile(TM_SCAN, s)
    n_t = s // tm
    hb = tm // BF16_ROWS
    n_hb = s // BF16_ROWS

    def tix(i):
        return (n_t - 1 - i) if reverse else i

    rows = jnp.arange(tm, dtype=jnp.int32)
    time_of_row = (rows % SUBLANES) * (tm // SUBLANES) + rows // SUBLANES
    perm = (time_of_row[:, None] == rows[None, :]).astype(BF16)
    perm_spec = pl.BlockSpec((tm, tm), lambda b, i, h: (0, 0))

    in_specs = [
        pl.BlockSpec((1, tm, hd), lambda b, i, h: (b, tix(i), n_h + h)),
        pl.BlockSpec((1, BF16_ROWS, hd), lambda b, i, h: (b, jnp.maximum(tix(i) * hb - 1, 0), n_h + h)),
        pl.BlockSpec((1, BF16_ROWS, hd), lambda b, i, h: (b, jnp.minimum((tix(i) + 1) * hb, n_hb - 1), n_h + h)),
        perm_spec,
        pl.BlockSpec((conv_w.shape[0], hd), lambda b, i, h: (0, h)),
        pl.BlockSpec((1, hd), lambda b, i, h: (0, h)),
        pl.BlockSpec((1, hd, 2 * hd), lambda b, i, h: (h, 0, 0)),
        pl.BlockSpec((1, 1, 2 * hd), lambda b, i, h: (h, 0, 0)),
        pl.BlockSpec((1, hd), lambda b, i, h: (0, h)),
        pl.BlockSpec((1, n_h, 1, hd), lambda b, i, h: (b, 0, 0, 0)),
    ]
    args = [proj, proj, proj, perm, conv_w, conv_b.reshape(1, r), wg, bg, lam.reshape(1, r), h0]
    assert conv_w.shape[0] == 4 and tm // SUBLANES >= 2, "window below covers taps at time offsets -1 .. +2"
    scratch = [
        pltpu.VMEM((tm + 3 * SUBLANES, hd), F32),
        pltpu.VMEM((tm, hd), F32),
        pltpu.VMEM((tm, hd), F32),
        pltpu.VMEM((n_h, 1, hd), F32),
    ]
    hlast_spec = pl.BlockSpec((1, n_h, 1, hd), lambda b, i, h: (b, 0, 0, 0))
    hlast_shape = jax.ShapeDtypeStruct((nb, n_h, 1, hd), F32)
    if readout is None:
        out_specs = [pl.BlockSpec((1, tm, hd), lambda b, i, h: (b, tix(i), h)), hlast_spec]
        out_shape = [jax.ShapeDtypeStruct((nb, s, r), h_dtype), hlast_shape]
    else:
        h_other, w_out, b_out, x, mod, ln_g, ln_b = readout
        d = x.shape[-1]
        in_specs += [
            pl.BlockSpec((1, tm, hd), lambda b, i, h: (b, tix(i), h)),
            pl.BlockSpec((1, tm, hd), lambda b, i, h: (b, tix(i), h)),
            pl.BlockSpec((hd, d), lambda b, i, h: (h, 0)),
            pl.BlockSpec((1, d), lambda b, i, h: (0, 0)),
            pl.BlockSpec((1, tm, d), lambda b, i, h: (b, tix(i), 0)),
            pl.BlockSpec((1, 1, 1, d), lambda b, i, h: (2, b, 0, 0)),
            pl.BlockSpec((1, d), lambda b, i, h: (0, 0)),
            pl.BlockSpec((1, d), lambda b, i, h: (0, 0)),
        ]
        args += [h_other, proj, w_out, b_out.reshape(1, d), x, mod, ln_g.reshape(1, d), ln_b.reshape(1, d)]
        assert (tm // SUBLANES) % SUBLANES == 0, "readout un-interleaves 8 vreg rows at a time"
        scratch += [pltpu.VMEM((d // LANES, tm, LANES), F32)]
        out_specs = [pl.BlockSpec((1, tm, d), lambda b, i, h: (b, tix(i), 0)), hlast_spec]
        out_shape = [jax.ShapeDtypeStruct((nb, s, d), F32), hlast_shape]
    kern = functools.partial(_rglru_kernel, reverse=reverse, readout=readout is not None,
                             n_t=n_t, tm=tm, alpha=alpha)
    return pl.pallas_call(
        kern,
        grid=(nb, n_t, n_h),
        in_specs=in_specs,
        out_specs=out_specs,
        out_shape=out_shape,
        scratch_shapes=scratch,
        compiler_params=_cparams(3),
        name="rglru_bwd_readout" if readout is not None else ("rglru_bwd" if reverse else "rglru_fwd"),
    )(*args)


def _glu_kernel(x_ref, sh_ref, sc_ref, wv_ref, wg_ref, bv_ref, bgate_ref, *rest, n_side):
    o_ref = rest[n_side]
    h = (x_ref[0] * (1.0 + sc_ref[0, 0]) + sh_ref[0, 0]).astype(BF16)
    val = _bdot(h, wv_ref[...]) + bv_ref[...]
    gate = _bdot(h, wg_ref[...]) + bgate_ref[...]
    o_ref[0] = (val * _sigmoid(gate)).astype(o_ref.dtype)
    _SideCast.run(rest[:n_side], rest[n_side + 1:])


def _modulated_glu(x, mod, w_bf16, bias, side_stacks=(), side_layer=0):
    nb, s, d = x.shape
    n = w_bf16.shape[1] // 2
    tm = _tile(TM_PROJ, s)
    tn = _tile(1024, n)
    nj = n // tn
    b2 = bias.reshape(1, 2 * n)
    grid = (nj, nb, s // tm)
    side = _SideCast(side_stacks, side_layer, grid)
    outs = pl.pallas_call(
        functools.partial(_glu_kernel, n_side=side.n),
        grid=grid,
        in_specs=[
            pl.BlockSpec((1, tm, d), lambda j, b, i: (b, i, 0)),
            pl.BlockSpec((1, 1, 1, d), lambda j, b, i: (0, b, 0, 0)),
            pl.BlockSpec((1, 1, 1, d), lambda j, b, i: (1, b, 0, 0)),
            pl.BlockSpec((d, tn), lambda j, b, i: (0, j)),
            pl.BlockSpec((d, tn), lambda j, b, i: (0, nj + j)),
            pl.BlockSpec((1, tn), lambda j, b, i: (0, j)),
            pl.BlockSpec((1, tn), lambda j, b, i: (0, nj + j)),
        ] + side.in_specs,
        out_specs=[pl.BlockSpec((1, tm, tn), lambda j, b, i: (b, i, j))] + side.out_specs,
        out_shape=[jax.ShapeDtypeStruct((nb, s, n), BF16)] + side.out_shape,
        compiler_params=_cparams(3),
        name="modulated_glu",
    )(x, mod, mod, w_bf16, w_bf16, b2, b2, *side.args)
    return outs[0], side.unpack(outs[1:])


def _conformer_tail_kernel(cur_ref, prev_ref, next_ref, wdw_ref, bdw_ref, ng_ref, nb_ref, w2_ref, b2_ref,
                           x_ref, g1_ref, lng_ref, lnb_ref, xo_ref, win_ref, z_ref, shift_ref, *, n_t, tm, alpha,
                           lane_chunk, row_chunk):
    i = pl.program_id(1)
    d = cur_ref.shape[-1]
    halo = BF16_ROWS
    win_ref[0:halo, :] = jnp.where(i == 0, 0.0, prev_ref[0].astype(F32))
    win_ref[halo:halo + tm, :] = cur_ref[0].astype(F32)
    win_ref[halo + tm:, :] = jnp.where(i == n_t - 1, 0.0, next_ref[0].astype(F32))
    n_taps = wdw_ref.shape[0]
    base = halo - (n_taps - 1) // 2

    n_shift = shift_ref.shape[1]
    for c in range(d // lane_chunk):
        lanes = slice(c * lane_chunk, (c + 1) * lane_chunk)
        for s in range(SUBLANES):
            shift_ref[s] = win_ref[pl.ds(s, n_shift), lanes]

        def rows_step(rc, _, lanes=lanes):
            r0 = pl.multiple_of(rc * row_chunk, row_chunk)
            acc = jnp.zeros((row_chunk, lane_chunk), F32) + bdw_ref[:, lanes]
            for k in range(n_taps):
                q, s = divmod(base + k, SUBLANES)
                rows = pl.ds(pl.multiple_of(r0 + q * SUBLANES, SUBLANES), row_chunk)
                acc = acc + wdw_ref[k:k + 1, lanes] * shift_ref[s, rows, :]
            z_ref[pl.ds(r0, row_chunk), lanes] = acc
            return 0

        lax.fori_loop(0, tm // row_chunk, rows_step, 0)

    z = _layer_norm(z_ref[...], ng_ref[...], nb_ref[...])
    y = _bdot(_silu(z).astype(BF16), w2_ref[...]) + b2_ref[...]
    v = alpha * x_ref[0] + g1_ref[0, 0] * y
    xo_ref[0] = _layer_norm(v, lng_ref[...], lnb_ref[...])


def _conformer_tail(z, w_dw, b_dw, n_g, n_b, w2_bf16, b2, x, mod, ln_g, ln_b, alpha):
    nb, s, d = z.shape
    tm = _tile(TM_CONV, s)
    n_t = s // tm
    hb = tm // BF16_ROWS
    n_hb = s // BF16_ROWS
    lane_chunk = min(256, d)
    n_taps = w_dw.shape[0]
    last_tap_row = BF16_ROWS + (n_taps - 1) // 2
    n_shift = tm + last_tap_row // SUBLANES * SUBLANES
    assert n_shift + SUBLANES - 1 <= tm + 2 * BF16_ROWS and (n_taps - 1) // 2 <= BF16_ROWS
    kern = functools.partial(_conformer_tail_kernel, n_t=n_t, tm=tm, alpha=alpha,
                             lane_chunk=lane_chunk, row_chunk=min(32, tm))
    vec = lambda a: a.reshape(1, d)
    vspec = pl.BlockSpec((1, d), lambda b, i: (0, 0))
    return pl.pallas_call(
        kern,
        grid=(nb, n_t),
        in_specs=[
            pl.BlockSpec((1, tm, d), lambda b, i: (b, i, 0)),
            pl.BlockSpec((1, BF16_ROWS, d), lambda b, i: (b, jnp.maximum(i * hb - 1, 0), 0)),
            pl.BlockSpec((1, BF16_ROWS, d), lambda b, i: (b, jnp.minimum((i + 1) * hb, n_hb - 1), 0)),
            pl.BlockSpec((w_dw.shape[0], d), lambda b, i: (0, 0)),
            vspec, vspec, vspec,
            pl.BlockSpec((d, d), lambda b, i: (0, 0)),
            vspec,
            pl.BlockSpec((1, tm, d), lambda b, i: (b, i, 0)),
            pl.BlockSpec((1, 1, 1, d), lambda b, i: (2, b, 0, 0)),
            vspec, vspec,
        ],
        out_specs=pl.BlockSpec((1, tm, d), lambda b, i: (b, i, 0)),
        out_shape=jax.ShapeDtypeStruct((nb, s, d), F32),
        scratch_shapes=[pltpu.VMEM((tm + 2 * BF16_ROWS, d), F32), pltpu.VMEM((tm, d), F32),
                        pltpu.VMEM((SUBLANES, n_shift, lane_chunk), F32)],
        compiler_params=_cparams(2),
        name="conformer_tail",
    )(z, z, z, w_dw, vec(b_dw), vec(n_g), vec(n_b), w2_bf16, vec(b2), x, mod, vec(ln_g), vec(ln_b))


def _router_kernel(x_ref, sh_ref, sc_ref, whi_ref, wlo_ref, bc_ref,
                   idx_ref, wt_ref, rank_ref, cnt_ref, carry_ref, *, n_exp, tm):
    i = pl.program_id(0)

    @pl.when(i == 0)
    def _():
        carry_ref[...] = jnp.zeros_like(carry_ref)

    tok = x_ref[...] * (1.0 + sc_ref[0, 0]) + sh_ref[0, 0]
    t_hi = tok.astype(BF16)
    t_lo = (tok - t_hi.astype(F32)).astype(BF16)
    nt = (((1,), (1,)), ((), ()))
    logits = (lax.dot_general(whi_ref[...], t_hi, nt, preferred_element_type=F32)
              + lax.dot_general(whi_ref[...], t_lo, nt, preferred_element_type=F32)
              + lax.dot_general(wlo_ref[...], t_hi, nt, preferred_element_type=F32))
    scores = _sigmoid(logits)
    biased = scores + bc_ref[...]

    per_group = n_exp // N_GROUPS
    sub = lax.broadcasted_iota(jnp.int32, (per_group, tm), 0)
    neg = -jnp.inf
    grp = [biased[g * per_group:(g + 1) * per_group, :] for g in range(N_GROUPS)]
    sc_g = [scores[g * per_group:(g + 1) * per_group, :] for g in range(N_GROUPS)]

    gscore = []
    for g in range(N_GROUPS):
        m1 = jnp.max(grp[g], axis=0, keepdims=True)
        first = jnp.min(jnp.where(grp[g] == m1, sub, per_group), axis=0, keepdims=True)
        m2 = jnp.max(jnp.where(sub == first, neg, grp[g]), axis=0, keepdims=True)
        gscore.append(m1 + m2)
    masked = []
    for g in range(N_GROUPS):
        beaten = jnp.zeros((1, tm), jnp.int32)
        for o in range(N_GROUPS):
            if o == g:
                continue
            wins = (gscore[o] > gscore[g]) | ((gscore[o] == gscore[g]) & (o < g))
            beaten = beaten + wins.astype(jnp.int32)
        masked.append(jnp.where(beaten < TOPK_GROUPS, grp[g], neg))

    eid = [sub + g * per_group for g in range(N_GROUPS)]
    member = [jnp.zeros((per_group, tm), F32) for _ in range(N_GROUPS)]
    idx_rows, score_rows = [], []
    for _ in range(TOP_K):
        m = masked[0]
        for g in range(1, N_GROUPS):
            m = jnp.maximum(m, masked[g])
        m = jnp.max(m, axis=0, keepdims=True)
        cand = jnp.where(masked[0] == m, eid[0], n_exp)
        for g in range(1, N_GROUPS):
            cand = jnp.minimum(cand, jnp.where(masked[g] == m, eid[g], n_exp))
        first = jnp.min(cand, axis=0, keepdims=True)
        picked = jnp.zeros((per_group, tm), F32)
        for g in range(N_GROUPS):
            sel = eid[g] == first
            picked = picked + jnp.where(sel, sc_g[g], 0.0)
            member[g] = jnp.where(sel, 1.0, member[g])
            masked[g] = jnp.where(sel, neg, masked[g])
        idx_rows.append(first)
        score_rows.append(jnp.sum(picked, axis=0, keepdims=True))

    total = score_rows[0]
    for k in range(1, TOP_K):
        total = total + score_rows[k]

    memb = jnp.concatenate(member, axis=0)
    earlier = jnp.where(lax.broadcasted_iota(jnp.int32, (tm, tm), 0)
                        < lax.broadcasted_iota(jnp.int32, (tm, tm), 1), 1.0, 0.0).astype(BF16)
    pos = _bdot(memb.astype(BF16), earlier) + carry_ref[...]
    pos_g = [pos[g * per_group:(g + 1) * per_group, :] for g in range(N_GROUPS)]
    for k in range(TOP_K):
        acc = jnp.zeros((per_group, tm), F32)
        for g in range(N_GROUPS):
            acc = acc + jnp.where(eid[g] == idx_rows[k], pos_g[g], 0.0)
        rank_ref[k:k + 1, :] = jnp.sum(acc, axis=0, keepdims=True).astype(jnp.int32)
        idx_ref[k:k + 1, :] = idx_rows[k]
        wt_ref[k:k + 1, :] = score_rows[k] / total * ROUTED_SCALE
    carry_ref[...] += jnp.sum(memb, axis=1, keepdims=True)
    cnt_ref[...] = carry_ref[...].astype(jnp.int32)


def _route(x, mod, w_router, b_corr):
    nb, s, d = x.shape
    t = nb * s
    n_exp = w_router.shape[1]
    tm = _tile(TM_ROUTE, s)
    per_b = s // tm
    w_t = w_router.T
    w_hi = w_t.astype(BF16)
    w_lo = (w_t - w_hi.astype(F32)).astype(BF16)
    kern = functools.partial(_router_kernel, n_exp=n_exp, tm=tm)
    out_spec = pl.BlockSpec((TOP_K, tm), lambda i: (0, i))
    return pl.pallas_call(
        kern,
        grid=(t // tm,),
        in_specs=[
            pl.BlockSpec((tm, d), lambda i: (i, 0)),
            pl.BlockSpec((1, 1, 1, d), lambda i: (3, i // per_b, 0, 0)),
            pl.BlockSpec((1, 1, 1, d), lambda i: (4, i // per_b, 0, 0)),
            pl.BlockSpec((n_exp, d), lambda i: (0, 0)),
            pl.BlockSpec((n_exp, d), lambda i: (0, 0)),
            pl.BlockSpec((n_exp, 1), lambda i: (0, 0)),
        ],
        out_specs=[out_spec, out_spec, out_spec, pl.BlockSpec((n_exp, 1), lambda i: (0, 0))],
        out_shape=[
            jax.ShapeDtypeStruct((TOP_K, t), jnp.int32),
            jax.ShapeDtypeStruct((TOP_K, t), F32),
            jax.ShapeDtypeStruct((TOP_K, t), jnp.int32),
            jax.ShapeDtypeStruct((n_exp, 1), jnp.int32),
        ],
        scratch_shapes=[pltpu.VMEM((n_exp, 1), F32)],
        compiler_params=_cparams(1),
        name="moe_router",
    )(x.reshape(t, d), mod, mod, w_hi, w_lo, b_corr.reshape(n_exp, 1).astype(F32))


def _slots_kernel(ps_ref, idx_ref, rank_ref, dest_ref, *, n_exp):
    idx = idx_ref[...]

    def add_start(e, acc):
        return acc + jnp.where(idx == e, ps_ref[e], 0)

    dest_ref[...] = lax.fori_loop(0, n_exp, add_start, rank_ref[...])


def _slots(idx, rank, pad_starts):
    k, t = idx.shape
    tm = _tile(2048, t)
    grid_spec = pltpu.PrefetchScalarGridSpec(
        num_scalar_prefetch=1,
        grid=(t // tm,),
        in_specs=[pl.BlockSpec((k, tm), lambda i, ps: (0, i)), pl.BlockSpec((k, tm), lambda i, ps: (0, i))],
        out_specs=pl.BlockSpec((k, tm), lambda i, ps: (0, i)),
    )
    return pl.pallas_call(
        functools.partial(_slots_kernel, n_exp=pad_starts.shape[0]),
        grid_spec=grid_spec,
        out_shape=jax.ShapeDtypeStruct((k, t), jnp.int32),
        compiler_params=_cparams(1),
        name="moe_slots",
    )(pad_starts, idx, rank)


def _to_slabs(value):
    n_seg = value.shape[1] // LANES
    by_seg = jnp.stack([value[:, s * LANES:(s + 1) * LANES] for s in range(n_seg)], axis=0)
    return jnp.swapaxes(by_seg, 0, 1)


def _from_slabs(slabs):
    by_seg = jnp.swapaxes(slabs, 0, 1)
    return jnp.concatenate([by_seg[s] for s in range(slabs.shape[1])], axis=1)


def _dispatch_kernel(pend_ref, padded_ref, x_ref, sh_ref, sc_ref, dest_ref, xs_ref,
                     tok_ref, zero_ref, sem, zsem, *, n_exp, tm, bm):
    i = pl.program_id(0)

    def zero_copy(e):
        return pltpu.make_async_copy(zero_ref, xs_ref.at[pl.ds(pend_ref[e] - bm, bm)], zsem)

    @pl.when(i == 0)
    def _():
        zero_ref[...] = jnp.zeros_like(zero_ref)

        def start(e, _):
            @pl.when(padded_ref[e] > 0)
            def _():
                zero_copy(e).start()
            return 0

        def wait(e, _):
            @pl.when(padded_ref[e] > 0)
            def _():
                zero_copy(e).wait()
            return 0

        def tail_copy(blk):
            return pltpu.make_async_copy(zero_ref, xs_ref.at[pl.ds(blk * bm, bm)], zsem)

        def start_tail(blk, _):
            tail_copy(blk).start()
            return 0

        def wait_tail(blk, _):
            tail_copy(blk).wait()
            return 0

        first_unused = pend_ref[n_exp - 1] // bm
        n_blk = xs_ref.shape[0] // bm
        lax.fori_loop(0, n_exp, start, 0)
        lax.fori_loop(first_unused, n_blk, start_tail, 0)
        lax.fori_loop(0, n_exp, wait, 0)
        lax.fori_loop(first_unused, n_blk, wait_tail, 0)

    n_steps = pl.num_programs(0)
    slot = i % 2

    def drain(buf):
        for k in range(TOP_K):
            pltpu.make_async_copy(tok_ref.at[buf], xs_ref.at[pl.ds(0, tm)], sem.at[buf]).wait()

    @pl.when(i >= 2)
    def _():
        drain(slot)

    tok_ref[slot] = _to_slabs(x_ref[...] * (1.0 + sc_ref[0, 0]) + sh_ref[0, 0])

    def issue(r, _):
        for k in range(TOP_K):
            pltpu.make_async_copy(tok_ref.at[slot, r], xs_ref.at[dest_ref[k, r]],
                                  sem.at[slot]).start(priority=k % 2)
        return 0

    lax.fori_loop(0, tm, issue, 0)

    @pl.when(i == n_steps - 1)
    def _():
        @pl.when(i >= 1)
        def _():
            drain(1 - slot)
        drain(slot)


def _dispatch(x, mod, dest, pad_ends, padded, n_slots, bm):
    nb, s, d = x.shape
    t = nb * s
    tm = _tile(TM_DISPATCH, s)
    per_b = s // tm
    n_exp = pad_ends.shape[0]
    n_seg = d // LANES
    assert n_seg % SUBLANES == 0, "token slabs must be whole (8, 128) tiles"
    kern = functools.partial(_dispatch_kernel, n_exp=n_exp, tm=tm, bm=bm)
    grid_spec = pltpu.PrefetchScalarGridSpec(
        num_scalar_prefetch=2,
        grid=(t // tm,),
        in_specs=[
            pl.BlockSpec((tm, d), lambda i, pe, pd: (i, 0)),
            pl.BlockSpec((1, 1, 1, d), lambda i, pe, pd: (3, i // per_b, 0, 0)),
            pl.BlockSpec((1, 1, 1, d), lambda i, pe, pd: (4, i // per_b, 0, 0)),
            pl.BlockSpec((TOP_K, tm), lambda i, pe, pd: (0, i), memory_space=pltpu.SMEM),
        ],
        out_specs=pl.BlockSpec(memory_space=pl.ANY),
        scratch_shapes=[
            pltpu.VMEM((2, tm, n_seg, LANES), F32),
            pltpu.VMEM((bm, n_seg, LANES), F32),
            pltpu.SemaphoreType.DMA((2,)),
            pltpu.SemaphoreType.DMA(()),
        ],
    )
    return pl.pallas_call(
        kern,
        grid_spec=grid_spec,
        out_shape=jax.ShapeDtypeStruct((n_slots, n_seg, LANES), F32),
        compiler_params=_cparams(1),
        name="moe_dispatch",
    )(pad_ends, padded, x.reshape(t, d), mod, mod, dest)


def _experts_kernel(be_ref, nu_ref, xs_ref, wg_ref, wu_ref, wd_ref, ys_ref):
    b = pl.program_id(0)

    @pl.when(b < nu_ref[0])
    def _():
        x = _from_slabs(xs_ref[...]).astype(BF16)
        g = _bdot(x, wg_ref[0])
        u = _bdot(x, wu_ref[0])
        ys_ref[...] = _to_slabs(_bdot((_silu(g) * u).astype(BF16), wd_ref[0]))

    @pl.when(b >= nu_ref[0])
    def _():
        ys_ref[...] = jnp.zeros_like(ys_ref)


def _grouped_experts(xs, blk_e, n_used, w_gate, w_up, w_down, bm):
    d, f = w_gate.shape[1:]
    n_seg = d // LANES
    n_blk = xs.shape[0] // bm

    def blk(b, be, nu):
        return jnp.minimum(b, nu[0] - 1)

    grid_spec = pltpu.PrefetchScalarGridSpec(
        num_scalar_prefetch=2,
        grid=(n_blk,),
        in_specs=[
            pl.BlockSpec((bm, n_seg, LANES), lambda b, be, nu: (blk(b, be, nu), 0, 0)),
            pl.BlockSpec((1, d, f), lambda b, be, nu: (be[blk(b, be, nu)], 0, 0)),
            pl.BlockSpec((1, d, f), lambda b, be, nu: (be[blk(b, be, nu)], 0, 0)),
            pl.BlockSpec((1, f, d), lambda b, be, nu: (be[blk(b, be, nu)], 0, 0)),
        ],
        out_specs=pl.BlockSpec((bm, n_seg, LANES), lambda b, be, nu: (b, 0, 0)),
    )
    return pl.pallas_call(
        _experts_kernel,
        grid_spec=grid_spec,
        out_shape=jax.ShapeDtypeStruct(xs.shape, F32),
        compiler_params=_cparams(1),
        name="moe_experts",
    )(blk_e, n_used, xs, w_gate, w_up, w_down)


def _combine_kernel(x_ref, sh_ref, sc_ref, gt_ref, dest_ref, dest_next_ref, wt_ref, ys_ref,
                    wsg_ref, wsu_ref, wsd_ref, lng_ref, lnb_ref, xo_ref, rows_ref, sem, *, tm, alpha):
    i = pl.program_id(0)
    n_steps = pl.num_programs(0)
    slot = i % 2

    def gather(d_ref, buf):
        def issue(r, _):
            for k in range(TOP_K):
                pltpu.make_async_copy(ys_ref.at[d_ref[k, r]], rows_ref.at[buf, k, r],
                                      sem.at[buf]).start(priority=k % 2)
            return 0
        lax.fori_loop(0, tm, issue, 0)

    @pl.when(i == 0)
    def _():
        gather(dest_ref, 0)

    for r in range(tm):
        for k in range(TOP_K):
            pltpu.make_async_copy(ys_ref.at[dest_next_ref[k, r]], rows_ref.at[1 - slot, k, r],
                                  sem.at[1 - slot]).start(priority=k % 2)

    x = x_ref[...]
    tok = (x * (1.0 + sc_ref[0, 0]) + sh_ref[0, 0]).astype(BF16)
    hid = (_silu(_bdot(tok, wsg_ref[...])) * _bdot(tok, wsu_ref[...])).astype(BF16)
    f = _bdot(hid, wsd_ref[...])

    for k in range(TOP_K):
        pltpu.make_async_copy(ys_ref.at[pl.ds(0, tm)], rows_ref.at[slot, k], sem.at[slot]).wait()
    n_seg = rows_ref.shape[3]
    wt = wt_ref[...]
    routed = jnp.broadcast_to(wt[:, 0:1, :], (tm, n_seg, LANES)) * rows_ref[slot, 0]
    for k in range(1, TOP_K):
        routed = routed + jnp.broadcast_to(wt[:, k:k + 1, :], (tm, n_seg, LANES)) * rows_ref[slot, k]
    f = f + _from_slabs(routed)
    v = alpha * x + gt_ref[0, 0] * f
    xo_ref[...] = _layer_norm(v, lng_ref[...], lnb_ref[...])

    @pl.when(i == n_steps - 1)
    def _():
        for k in range(TOP_K):
            pltpu.make_async_copy(ys_ref.at[pl.ds(0, tm)], rows_ref.at[1 - slot, k], sem.at[1 - slot]).wait()


def _combine(x, mod, dest, wt_rows, ys, ws_gate, ws_up, ws_down, ln_g, ln_b, alpha):
    nb, s, d = x.shape
    t = nb * s
    tm = _tile(TM_COMBINE, s)
    per_b = s // tm
    f = ws_gate.shape[-1]
    n_seg = d // LANES
    kern = functools.partial(_combine_kernel, tm=tm, alpha=alpha)
    vspec = pl.BlockSpec((1, d), lambda i: (0, 0))
    n_steps = t // tm
    out = pl.pallas_call(
        kern,
        grid=(n_steps,),
        in_specs=[
            pl.BlockSpec((tm, d), lambda i: (i, 0)),
            pl.BlockSpec((1, 1, 1, d), lambda i: (3, i // per_b, 0, 0)),
            pl.BlockSpec((1, 1, 1, d), lambda i: (4, i // per_b, 0, 0)),
            pl.BlockSpec((1, 1, 1, d), lambda i: (5, i // per_b, 0, 0)),
            pl.BlockSpec((TOP_K, tm), lambda i: (0, i), memory_space=pltpu.SMEM),
            pl.BlockSpec((TOP_K, tm), lambda i: (0, jnp.minimum(i + 1, n_steps - 1)), memory_space=pltpu.SMEM),
            pl.BlockSpec((tm, TOP_K, LANES), lambda i: (i, 0, 0)),
            pl.BlockSpec(memory_space=pl.ANY),
            pl.BlockSpec((d, f), lambda i: (0, 0)),
            pl.BlockSpec((d, f), lambda i: (0, 0)),
            pl.BlockSpec((f, d), lambda i: (0, 0)),
            vspec, vspec,
        ],
        out_specs=pl.BlockSpec((tm, d), lambda i: (i, 0)),
        out_shape=jax.ShapeDtypeStruct((t, d), F32),
        scratch_shapes=[pltpu.VMEM((2, TOP_K, tm, n_seg, LANES), F32), pltpu.SemaphoreType.DMA((2,))],
        compiler_params=_cparams(1),
        name="moe_combine",
    )(x.reshape(t, d), mod, mod, mod, dest, dest, wt_rows, ys, ws_gate, ws_up, ws_down,
      ln_g.reshape(1, d), ln_b.reshape(1, d))
    return out.reshape(nb, s, d)


def _moe_block(x, mod, w_router, b_router, w_gate, w_up, w_down, ws_gate, ws_up, ws_down, ln_g, ln_b, alpha):
    nb, s, d = x.shape
    t = nb * s
    n_exp = w_router.shape[1]
    bm = _tile(MOE_BLOCK, t * TOP_K)
    idx, wt, rank, counts = _route(x, mod, w_router, b_router)

    counts = counts.reshape(n_exp)
    padded = (counts + bm - 1) // bm * bm
    pad_ends = jnp.cumsum(padded).astype(jnp.int32)
    pad_starts = pad_ends - padded
    dest = _slots(idx, rank, pad_starts.astype(jnp.int32))
    n_blk = t * TOP_K // bm + n_exp
    n_used = (pad_ends[-1:] // bm).astype(jnp.int32)
    blk_start = jnp.arange(n_blk, dtype=jnp.int32) * bm
    blk_e = jnp.sum((pad_ends[None, :] <= blk_start[:, None]).astype(jnp.int32), axis=1)
    blk_e = jnp.minimum(blk_e, n_exp - 1).astype(jnp.int32)

    xs = _dispatch(x, mod, dest, pad_ends, padded.astype(jnp.int32), n_blk * bm, bm)
    ys = _grouped_experts(xs, blk_e, n_used, w_gate, w_up, w_down, bm)
    wt_splat = jnp.broadcast_to(wt.T[:, :, None], (t, TOP_K, LANES))
    return _combine(x, mod, dest, wt_splat, ys,ws_gate.astype(BF16), ws_up.astype(BF16), ws_down.astype(BF16),
                    ln_g, ln_b, alpha)


def kernel(x, c, ctx, c_ctx, w_mod, b_mod, ln1_g, ln1_b, ln2_g, ln2_b, rg_w_in, rg_b_in, rg_conv_w, rg_conv_b, rg_w_gates, rg_b_gates, rg_lambda, rg_w_out, rg_b_out, cf_w_pw1, cf_b_pw1, cf_w_dw, cf_b_dw, cf_norm_g, cf_norm_b, cf_w_pw2, cf_b_pw2, moe_w_router, moe_b_router, moe_w_gate, moe_w_up, moe_w_down, sh_w_gate, sh_w_up, sh_w_down):
    nb, s, d = x.shape
    depth = w_mod.shape[0]
    assert depth == 2 and nb + 1 <= SUBLANES
    alpha = (2 * depth) ** 0.25

    c_rows = jnp.zeros((SUBLANES, d), F32).at[:nb].set(c).at[nb].set(c_ctx)
    mod_all = _adaln_vectors(c_rows, w_mod, b_mod)
    mod_all = mod_all.reshape(depth, SUBLANES, 6, 1, d).transpose(0, 2, 1, 3, 4)
    mod_lat = [mod_all[l, :, :nb] for l in range(depth)]
    mod_ctx = jnp.broadcast_to(mod_all[0, :, nb:nb + 1], (6, nb, 1, d))

    n_h, hd = rg_w_gates.shape[3], rg_w_gates.shape[4]
    r = n_h * hd
    w_in = rg_w_in[0].astype(BF16)
    wg = [jnp.concatenate([rg_w_gates[0, dr, 0], rg_w_gates[0, dr, 1]], axis=-1).astype(BF16) for dr in range(2)]
    bg = [jnp.concatenate([rg_b_gates[0, dr, 0], rg_b_gates[0, dr, 1]], axis=-1).reshape(n_h, 1, 2 * hd)
          for dr in range(2)]
    conv_w, conv_b = rg_conv_w[0], rg_conv_b[0]
    zero_state = jnp.zeros((nb, n_h, 1, hd), F32)

    proj_c, _ = _modulated_projection(ctx, mod_ctx, w_in, rg_b_in[0], BF16)
    _, hcf = _rglru_scan(proj_c, conv_w, conv_b, wg[0], bg[0], rg_lambda[0, 0], zero_state, reverse=False)
    _, hcb = _rglru_scan(proj_c, conv_w, conv_b, wg[1], bg[1], rg_lambda[0, 1], zero_state, reverse=True)

    expert_stacks = (moe_w_gate, moe_w_up, moe_w_down)
    proj_x, experts0 = _modulated_projection(x, mod_lat[0], w_in, rg_b_in[0], BF16, expert_stacks, 0)
    hxf, _ = _rglru_scan(proj_x, conv_w, conv_b, wg[0], bg[0], rg_lambda[0, 0], hcf, reverse=False)
    readout = (hxf, rg_w_out[0].astype(BF16), rg_b_out[0], x, mod_lat[0], ln1_g[0], ln1_b[0])
    x, _ = _rglru_scan(proj_x, conv_w, conv_b, wg[1], bg[1], rg_lambda[0, 1], hcb, reverse=True,
                       readout=readout, alpha=alpha)
    x = _moe_block(x, mod_lat[0], moe_w_router[0], moe_b_router[0],
                   *experts0,
                   sh_w_gate[0], sh_w_up[0], sh_w_down[0], ln2_g[0], ln2_b[0], alpha)

    z, experts1 = _modulated_glu(x, mod_lat[1], cf_w_pw1[0].astype(BF16), cf_b_pw1[0], expert_stacks, 1)
    x = _conformer_tail(z, cf_w_dw[0], cf_b_dw[0], cf_norm_g[0], cf_norm_b[0], cf_w_pw2[0].astype(BF16),
                        cf_b_pw2[0], x, mod_lat[1], ln1_g[1], ln1_b[1], alpha)
    x = _moe_block(x, mod_lat[1], moe_w_router[1], moe_b_router[1],
                   *experts1,
                   sh_w_gate[1], sh_w_up[1], sh_w_down[1], ln2_g[1], ln2_b[1], alpha)
    return x
```

```python
import functools

import jax
import jax.numpy as jnp
from jax import lax
from jax.experimental import pallas as pl
from jax.experimental.pallas import tpu as pltpu

F32 = jnp.float32
BF16 = jnp.bfloat16

RG_C = 8.0
N_GROUPS = 8
TOPK_GROUPS = 4
TOP_K = 8
ROUTED_SCALE = 2.5
LN_EPS = 1e-5

VMEM_LIMIT_BYTES = 56 * 1024 * 1024
SUBLANES = 8
LANES = 128
BF16_ROWS = 16

TM_PROJ = 512
TM_SCAN = 512
TM_CONV = 256
TM_ROUTE = 512
TM_DISPATCH = 256
TM_COMBINE = 128
MOE_BLOCK = 512


def _tile(default, n):
    t = min(default, n)
    assert n % t == 0, (default, n)
    return t


def _cparams(n_axes):
    return pltpu.CompilerParams(
        dimension_semantics=("arbitrary",) * n_axes,
        vmem_limit_bytes=VMEM_LIMIT_BYTES,
    )


def _sigmoid(v):
    return 0.5 * jnp.tanh(0.5 * v) + 0.5


def _sqrt_nonneg(v):
    return jnp.where(v > 0.0, v * lax.rsqrt(v), 0.0)


def _silu(v):
    return v * _sigmoid(v)


def _gelu_tanh(v):
    c = 0.7978845608028654
    return 0.5 * v * (1.0 + jnp.tanh(c * (v + 0.044715 * (v * v * v))))


def _softplus(v):
    return jnp.maximum(v, 0.0) + jnp.log(1.0 + jnp.exp(-jnp.abs(v)))


def _layer_norm(v, g, b):
    mu = jnp.mean(v, axis=-1, keepdims=True)
    d = v - mu
    var = jnp.mean(d * d, axis=-1, keepdims=True)
    return d * lax.rsqrt(var + LN_EPS) * g + b


def _bdot(a, b):
    return jnp.dot(a, b, preferred_element_type=F32)


def _mod_kernel(c_ref, w_ref, b_ref, o_ref):
    s = _silu(c_ref[...])
    o_ref[0] = _bdot(s.astype(BF16), w_ref[0].astype(BF16)) + b_ref[0]


def _adaln_vectors(c_rows, w_mod, b_mod):
    depth, d, d6 = w_mod.shape
    tn = _tile(1024, d6)
    return pl.pallas_call(
        _mod_kernel,
        grid=(depth, d6 // tn),
        in_specs=[
            pl.BlockSpec((SUBLANES, d), lambda l, n: (0, 0)),
            pl.BlockSpec((1, d, tn), lambda l, n: (l, 0, n)),
            pl.BlockSpec((1, 1, tn), lambda l, n: (l, 0, n)),
        ],
        out_specs=pl.BlockSpec((1, SUBLANES, tn), lambda l, n: (l, 0, n)),
        out_shape=jax.ShapeDtypeStruct((depth, SUBLANES, d6), F32),
        compiler_params=_cparams(2),
        name="adaln_vectors",
    )(c_rows, w_mod, b_mod.reshape(depth, 1, d6))


def _mod_spec(chunk, batch_of):
    def index(*ids):
        return (chunk, batch_of(*ids), 0, 0)
    return index


class _SideCast:
    def __init__(self, stacks, layer, grid):
        self.n = len(stacks)
        self.shapes = [w.shape[1:] for w in stacks]
        n_steps = 1
        for g in grid:
            n_steps *= g

        def step(*ids):
            lin = ids[0]
            for g, v in zip(grid[1:], ids[1:]):
                lin = lin * g + v
            return lin

        self.in_specs, self.out_specs, self.out_shape, self.args = [], [], [], []
        for w in stacks:
            n_l, n_e, a, b = w.shape
            rows = n_e * a
            blk = rows // n_steps
            assert rows % n_steps == 0 and blk % BF16_ROWS == 0, (w.shape, n_steps)
            self.in_specs.append(pl.BlockSpec((1, blk, b), lambda *ids: (layer, step(*ids), 0)))
            self.out_specs.append(pl.BlockSpec((blk, b), lambda *ids: (step(*ids), 0)))
            self.out_shape.append(jax.ShapeDtypeStruct((rows, b), BF16))
            self.args.append(w.reshape(n_l, rows, b))

    @staticmethod
    def run(src_refs, dst_refs):
        for src, dst in zip(src_refs, dst_refs):
            dst[...] = src[0].astype(dst.dtype)

    def unpack(self, outs):
        return [o.reshape(shape) for o, shape in zip(outs, self.shapes)]


def _proj_kernel(x_ref, sh_ref, sc_ref, w_ref, b_ref, *rest, n_side):
    o_ref = rest[n_side]
    h = x_ref[0] * (1.0 + sc_ref[0, 0]) + sh_ref[0, 0]
    o_ref[0] = (_bdot(h.astype(BF16), w_ref[...]) + b_ref[...]).astype(o_ref.dtype)
    _SideCast.run(rest[:n_side], rest[n_side + 1:])


def _modulated_projection(x, mod, w_bf16, bias, out_dtype, side_stacks=(), side_layer=0):
    nb, s, d = x.shape
    n = w_bf16.shape[1]
    tm = _tile(TM_PROJ, s)
    tn = _tile(2560, n)
    dm = mod.shape[-1]
    grid = (n // tn, nb, s // tm)
    side = _SideCast(side_stacks, side_layer, grid)
    outs = pl.pallas_call(
        functools.partial(_proj_kernel, n_side=side.n),
        grid=grid,
        in_specs=[
            pl.BlockSpec((1, tm, d), lambda j, b, i: (b, i, 0)),
            pl.BlockSpec((1, 1, 1, dm), lambda j, b, i: (0, b, 0, 0)),
            pl.BlockSpec((1, 1, 1, dm), lambda j, b, i: (1, b, 0, 0)),
            pl.BlockSpec((d, tn), lambda j, b, i: (0, j)),
            pl.BlockSpec((1, tn), lambda j, b, i: (0, j)),
        ] + side.in_specs,
        out_specs=[pl.BlockSpec((1, tm, tn), lambda j, b, i: (b, i, j))] + side.out_specs,
        out_shape=[jax.ShapeDtypeStruct((nb, s, n), out_dtype)] + side.out_shape,
        compiler_params=_cparams(3),
        name="modulated_projection",
    )(x, mod, mod, w_bf16, bias.reshape(1, n), *side.args)
    return outs[0], side.unpack(outs[1:])


def _rglru_kernel(*refs, reverse, readout, n_t, tm, alpha):
    if readout:
        (cur_ref, prev_ref, next_ref, perm_ref, cw_ref, cb_ref, wg_ref, bg_ref, lam_ref, h0_ref,
         hf_ref, gate_ref, wout_ref, bout_ref, x_ref, g1_ref, lng_ref, lnb_ref,
         xo_ref, hlast_ref,
         win_ref, a_ref, hl_ref, carry_ref, acc_ref) = refs
    else:
        (cur_ref, prev_ref, next_ref, perm_ref, cw_ref, cb_ref, wg_ref, bg_ref, lam_ref, h0_ref,
         ho_ref, hlast_ref,
         win_ref, a_ref, hl_ref, carry_ref) = refs
    i = pl.program_id(1)
    h = pl.program_id(2)
    n_h = pl.num_programs(2)
    ti = (n_t - 1 - i) if reverse else i
    hd = cur_ref.shape[-1]
    chunk = tm // SUBLANES
    sub = lax.broadcasted_iota(jnp.int32, (SUBLANES, hd), 0)

    rnn = _bdot(perm_ref[...], cur_ref[0])
    prev_last = jnp.where(ti == 0, 0.0, prev_ref[0].astype(F32)[BF16_ROWS - 1:, :])
    nxt = jnp.where(ti == n_t - 1, 0.0, next_ref[0].astype(F32)[:2, :])
    win_ref[0:SUBLANES, :] = jnp.where(
        sub == 0, prev_last, pltpu.roll(rnn[tm - SUBLANES:, :], 1, axis=0))
    win_ref[SUBLANES:SUBLANES + tm, :] = rnn
    for extra in range(2):
        lo = SUBLANES + tm + extra * SUBLANES
        win_ref[lo:lo + SUBLANES, :] = jnp.where(
            sub == SUBLANES - 1, nxt[extra:extra + 1, :],
            pltpu.roll(rnn[extra * SUBLANES:(extra + 1) * SUBLANES, :], SUBLANES - 1, axis=0))
    n_taps = cw_ref.shape[0]
    xb = cb_ref[...] + cw_ref[0:1, :] * win_ref[pl.ds(0, tm), :]
    for k in range(1, n_taps):
        xb = xb + cw_ref[k:k + 1, :] * win_ref[pl.ds(k * SUBLANES, tm), :]

    pre = _bdot(xb.astype(BF16), wg_ref[0]) + bg_ref[0]
    r = _sigmoid(pre[:, :hd])
    gi = _sigmoid(pre[:, hd:])
    log_a = (-RG_C) * r * _softplus(-lam_ref[...])
    a = jnp.exp(log_a)
    a_ref[...] = a
    hl_ref[...] = _sqrt_nonneg(1.0 - a * a) * gi * xb

    if readout:
        gate_act = _gelu_tanh(_bdot(perm_ref[...], gate_ref[0]))

    def scan_step(jj, carry):
        a_cum, h_loc = carry
        j = (chunk - 1 - jj) if reverse else jj
        rows = pl.ds(pl.multiple_of(j * SUBLANES, SUBLANES), SUBLANES)
        aj = a_ref[rows, :]
        h_loc = aj * h_loc + hl_ref[rows, :]
        a_cum = a_cum * aj
        hl_ref[rows, :] = h_loc
        a_ref[rows, :] = a_cum
        return a_cum, h_loc

    a_end, h_end = lax.fori_loop(
        0, chunk, scan_step, (jnp.ones((SUBLANES, hd), F32), jnp.zeros((SUBLANES, hd), F32)),
        unroll=SUBLANES)

    @pl.when(i == 0)
    def _():
        carry_ref[h] = h0_ref[0, h]

    state = carry_ref[h]
    entry = jnp.zeros((SUBLANES, hd), F32)
    for s in (range(SUBLANES - 1, -1, -1) if reverse else range(SUBLANES)):
        entry = jnp.where(sub == s, state, entry)
        state = a_end[s:s + 1, :] * state + h_end[s:s + 1, :]
    carry_ref[h] = state
    hlast_ref[0, h] = state

    h_full = hl_ref[...] + a_ref[...] * jnp.concatenate([entry] * chunk, axis=0)
    if not readout:
        ho_ref[0] = h_full.astype(ho_ref.dtype)
    else:
        h_sum = h_full + hf_ref[0].astype(F32)
        contrib = _bdot((h_sum * gate_act).astype(BF16), wout_ref[...])
        n_lc = acc_ref.shape[0]

        @pl.when(h == 0)
        def _():
            for c in range(n_lc):
                acc_ref[c] = contrib[:, c * LANES:(c + 1) * LANES]

        @pl.when(h != 0)
        def _():
            for c in range(n_lc):
                acc_ref[c] += contrib[:, c * LANES:(c + 1) * LANES]

        @pl.when(h == n_h - 1)
        def _():
            y = jnp.concatenate(
                [jnp.concatenate(
                    [acc_ref[c, pl.ds(q * SUBLANES * SUBLANES + s, SUBLANES, stride=SUBLANES), :]
                     for s in range(SUBLANES) for q in range(chunk // SUBLANES)], axis=0)
                 for c in range(n_lc)], axis=1) + bout_ref[...]
            v = alpha * x_ref[0] + g1_ref[0, 0] * y
            xo_ref[0] = _layer_norm(v, lng_ref[...], lnb_ref[...])


def _rglru_scan(proj, conv_w, conv_b, wg, bg, lam, h0, *, reverse, h_dtype=BF16, readout=None, alpha=None):
    nb, s, r2 = proj.shape
    r = r2 // 2
    n_h, hd = wg.shape[0], wg.shape[1]
    tm = _tile(TM_SCAN, s)
    n_t = s // tm
    hb = tm // BF16_ROWS
    n_hb = s // BF16_ROWS

    def tix(i):
        return (n_t - 1 - i) if reverse else i

    rows = jnp.arange(tm, dtype=jnp.int32)
    time_of_row = (rows % SUBLANES) * (tm // SUBLANES) + rows // SUBLANES
    perm = (time_of_row[:, None] == rows[None, :]).astype(BF16)
    perm_spec = pl.BlockSpec((tm, tm), lambda b, i, h: (0, 0))

    in_specs = [
        pl.BlockSpec((1, tm, hd), lambda b, i, h: (b, tix(i), n_h + h)),
        pl.BlockSpec((1, BF16_ROWS, hd), lambda b, i, h: (b, jnp.maximum(tix(i) * hb - 1, 0), n_h + h)),
        pl.BlockSpec((1, BF16_ROWS, hd), lambda b, i, h: (b, jnp.minimum((tix(i) + 1) * hb, n_hb - 1), n_h + h)),
        perm_spec,
        pl.BlockSpec((conv_w.shape[0], hd), lambda b, i, h: (0, h)),
        pl.BlockSpec((1, hd), lambda b, i, h: (0, h)),
        pl.BlockSpec((1, hd, 2 * hd), lambda b, i, h: (h, 0, 0)),
        pl.BlockSpec((1, 1, 2 * hd), lambda b, i, h: (h, 0, 0)),
        pl.BlockSpec((1, hd), lambda b, i, h: (0, h)),
        pl.BlockSpec((1, n_h, 1, hd), lambda b, i, h: (b, 0, 0, 0)),
    ]
    args = [proj, proj, proj, perm, conv_w, conv_b.reshape(1, r), wg, bg, lam.reshape(1, r), h0]
    assert conv_w.shape[0] == 4 and tm // SUBLANES >= 2, "window below covers taps at time offsets -1 .. +2"
    scratch = [
        pltpu.VMEM((tm + 3 * SUBLANES, hd), F32),
        pltpu.VMEM((tm, hd), F32),
        pltpu.VMEM((tm, hd), F32),
        pltpu.VMEM((n_h, 1, hd), F32),
    ]
    hlast_spec = pl.BlockSpec((1, n_h, 1, hd), lambda b, i, h: (b, 0, 0, 0))
    hlast_shape = jax.ShapeDtypeStruct((nb, n_h, 1, hd), F32)
    if readout is None:
        out_specs = [pl.BlockSpec((1, tm, hd), lambda b, i, h: (b, tix(i), h)), hlast_spec]
        out_shape = [jax.ShapeDtypeStruct((nb, s, r), h_dtype), hlast_shape]
    else:
        h_other, w_out, b_out, x, mod, ln_g, ln_b = readout
        d = x.shape[-1]
        in_specs += [
            pl.BlockSpec((1, tm, hd), lambda b, i, h: (b, tix(i), h)),
            pl.BlockSpec((1, tm, hd), lambda b, i, h: (b, tix(i), h)),
            pl.BlockSpec((hd, d), lambda b, i, h: (h, 0)),
            pl.BlockSpec((1, d), lambda b, i, h: (0, 0)),
            pl.BlockSpec((1, tm, d), lambda b, i, h: (b, tix(i), 0)),
            pl.BlockSpec((1, 1, 1, d), lambda b, i, h: (2, b, 0, 0)),
            pl.BlockSpec((1, d), lambda b, i, h: (0, 0)),
            pl.BlockSpec((1, d), lambda b, i, h: (0, 0)),
        ]
        args += [h_other, proj, w_out, b_out.reshape(1, d), x, mod, ln_g.reshape(1, d), ln_b.reshape(1, d)]
        assert (tm // SUBLANES) % SUBLANES == 0, "readout un-interleaves 8 vreg rows at a time"
        scratch += [pltpu.VMEM((d // LANES, tm, LANES), F32)]
        out_specs = [pl.BlockSpec((1, tm, d), lambda b, i, h: (b, tix(i), 0)), hlast_spec]
        out_shape = [jax.ShapeDtypeStruct((nb, s, d), F32), hlast_shape]
    kern = functools.partial(_rglru_kernel, reverse=reverse, readout=readout is not None,
                             n_t=n_t, tm=tm, alpha=alpha)
    return pl.pallas_call(
        kern,
        grid=(nb, n_t, n_h),
        in_specs=in_specs,
        out_specs=out_specs,
        out_shape=out_shape,
        scratch_shapes=scratch,
        compiler_params=_cparams(3),
        name="rglru_bwd_readout" if readout is not None else ("rglru_bwd" if reverse else "rglru_fwd"),
    )(*args)


def _glu_kernel(x_ref, sh_ref, sc_ref, wv_ref, wg_ref, bv_ref, bgate_ref, *rest, n_side):
    o_ref = rest[n_side]
    h = (x_ref[0] * (1.0 + sc_ref[0, 0]) + sh_ref[0, 0]).astype(BF16)
    val = _bdot(h, wv_ref[...]) + bv_ref[...]
    gate = _bdot(h, wg_ref[...]) + bgate_ref[...]
    o_ref[0] = (val * _sigmoid(gate)).astype(o_ref.dtype)
    _SideCast.run(rest[:n_side], rest[n_side + 1:])


def _modulated_glu(x, mod, w_bf16, bias, side_stacks=(), side_layer=0):
    nb, s, d = x.shape
    n = w_bf16.shape[1] // 2
    tm = _tile(TM_PROJ, s)
    tn = _tile(1024, n)
    nj = n // tn
    b2 = bias.reshape(1, 2 * n)
    grid = (nj, nb, s // tm)
    side = _SideCast(side_stacks, side_layer, grid)
    outs = pl.pallas_call(
        functools.partial(_glu_kernel, n_side=side.n),
        grid=grid,
        in_specs=[
            pl.BlockSpec((1, tm, d), lambda j, b, i: (b, i, 0)),
            pl.BlockSpec((1, 1, 1, d), lambda j, b, i: (0, b, 0, 0)),
            pl.BlockSpec((1, 1, 1, d), lambda j, b, i: (1, b, 0, 0)),
            pl.BlockSpec((d, tn), lambda j, b, i: (0, j)),
            pl.BlockSpec((d, tn), lambda j, b, i: (0, nj + j)),
            pl.BlockSpec((1, tn), lambda j, b, i: (0, j)),
            pl.BlockSpec((1, tn), lambda j, b, i: (0, nj + j)),
        ] + side.in_specs,
        out_specs=[pl.BlockSpec((1, tm, tn), lambda j, b, i: (b, i, j))] + side.out_specs,
        out_shape=[jax.ShapeDtypeStruct((nb, s, n), BF16)] + side.out_shape,
        compiler_params=_cparams(3),
        name="modulated_glu",
    )(x, mod, mod, w_bf16, w_bf16, b2, b2, *side.args)
    return outs[0], side.unpack(outs[1:])


def _conformer_tail_kernel(cur_ref, prev_ref, next_ref, wdw_ref, bdw_ref, ng_ref, nb_ref, w2_ref, b2_ref,
                           x_ref, g1_ref, lng_ref, lnb_ref, xo_ref, win_ref, z_ref, shift_ref, *, n_t, tm, alpha,
                           lane_chunk, row_chunk):
    i = pl.program_id(1)
    d = cur_ref.shape[-1]
    halo = BF16_ROWS
    win_ref[0:halo, :] = jnp.where(i == 0, 0.0, prev_ref[0].astype(F32))
    win_ref[halo:halo + tm, :] = cur_ref[0].astype(F32)
    win_ref[halo + tm:, :] = jnp.where(i == n_t - 1, 0.0, next_ref[0].astype(F32))
    n_taps = wdw_ref.shape[0]
    base = halo - (n_taps - 1) // 2

    n_shift = shift_ref.shape[1]
    for c in range(d // lane_chunk):
        lanes = slice(c * lane_chunk, (c + 1) * lane_chunk)
        for s in range(SUBLANES):
            shift_ref[s] = win_ref[pl.ds(s, n_shift), lanes]

        def rows_step(rc, _, lanes=lanes):
            r0 = pl.multiple_of(rc * row_chunk, row_chunk)
            acc = jnp.zeros((row_chunk, lane_chunk), F32) + bdw_ref[:, lanes]
            for k in range(n_taps):
                q, s = divmod(base + k, SUBLANES)
                rows = pl.ds(pl.multiple_of(r0 + q * SUBLANES, SUBLANES), row_chunk)
                acc = acc + wdw_ref[k:k + 1, lanes] * shift_ref[s, rows, :]
            z_ref[pl.ds(r0, row_chunk), lanes] = acc
            return 0

        lax.fori_loop(0, tm // row_chunk, rows_step, 0)

    z = _layer_norm(z_ref[...], ng_ref[...], nb_ref[...])
    y = _bdot(_silu(z).astype(BF16), w2_ref[...]) + b2_ref[...]
    v = alpha * x_ref[0] + g1_ref[0, 0] * y
    xo_ref[0] = _layer_norm(v, lng_ref[...], lnb_ref[...])


def _conformer_tail(z, w_dw, b_dw, n_g, n_b, w2_bf16, b2, x, mod, ln_g, ln_b, alpha):
    nb, s, d = z.shape
    tm = _tile(TM_CONV, s)
    n_t = s // tm
    hb = tm // BF16_ROWS
    n_hb = s // BF16_ROWS
    lane_chunk = min(256, d)
    n_taps = w_dw.shape[0]
    last_tap_row = BF16_ROWS + (n_taps - 1) // 2
    n_shift = tm + last_tap_row // SUBLANES * SUBLANES
    assert n_shift + SUBLANES - 1 <= tm + 2 * BF16_ROWS and (n_taps - 1) // 2 <= BF16_ROWS
    kern = functools.partial(_conformer_tail_kernel, n_t=n_t, tm=tm, alpha=alpha,
                             lane_chunk=lane_chunk, row_chunk=min(32, tm))
    vec = lambda a: a.reshape(1, d)
    vspec = pl.BlockSpec((1, d), lambda b, i: (0, 0))
    return pl.pallas_call(
        kern,
        grid=(nb, n_t),
        in_specs=[
            pl.BlockSpec((1, tm, d), lambda b, i: (b, i, 0)),
            pl.BlockSpec((1, BF16_ROWS, d), lambda b, i: (b, jnp.maximum(i * hb - 1, 0), 0)),
            pl.BlockSpec((1, BF16_ROWS, d), lambda b, i: (b, jnp.minimum((i + 1) * hb, n_hb - 1), 0)),
            pl.BlockSpec((w_dw.shape[0], d), lambda b, i: (0, 0)),
            vspec, vspec, vspec,
            pl.BlockSpec((d, d), lambda b, i: (0, 0)),
            vspec,
            pl.BlockSpec((1, tm, d), lambda b, i: (b, i, 0)),
            pl.BlockSpec((1, 1, 1, d), lambda b, i: (2, b, 0, 0)),
            vspec, vspec,
        ],
        out_specs=pl.BlockSpec((1, tm, d), lambda b, i: (b, i, 0)),
        out_shape=jax.ShapeDtypeStruct((nb, s, d), F32),
        scratch_shapes=[pltpu.VMEM((tm + 2 * BF16_ROWS, d), F32), pltpu.VMEM((tm, d), F32),
                        pltpu.VMEM((SUBLANES, n_shift, lane_chunk), F32)],
        compiler_params=_cparams(2),
        name="conformer_tail",
    )(z, z, z, w_dw, vec(b_dw), vec(n_g), vec(n_b), w2_bf16, vec(b2), x, mod, vec(ln_g), vec(ln_b))


def _router_kernel(x_ref, sh_ref, sc_ref, whi_ref, wlo_ref, bc_ref,
                   idx_ref, wt_ref, rank_ref, cnt_ref, carry_ref, *, n_exp, tm):
    i = pl.program_id(0)

    @pl.when(i == 0)
    def _():
        carry_ref[...] = jnp.zeros_like(carry_ref)

    tok = x_ref[...] * (1.0 + sc_ref[0, 0]) + sh_ref[0, 0]
    t_hi = tok.astype(BF16)
    t_lo = (tok - t_hi.astype(F32)).astype(BF16)
    nt = (((1,), (1,)), ((), ()))
    logits = (lax.dot_general(whi_ref[...], t_hi, nt, preferred_element_type=F32)
              + lax.dot_general(whi_ref[...], t_lo, nt, preferred_element_type=F32)
              + lax.dot_general(wlo_ref[...], t_hi, nt, preferred_element_type=F32))
    scores = _sigmoid(logits)
    biased = scores + bc_ref[...]

    per_group = n_exp // N_GROUPS
    sub = lax.broadcasted_iota(jnp.int32, (per_group, tm), 0)
    neg = -jnp.inf
    grp = [biased[g * per_group:(g + 1) * per_group, :] for g in range(N_GROUPS)]
    sc_g = [scores[g * per_group:(g + 1) * per_group, :] for g in range(N_GROUPS)]

    gscore = []
    for g in range(N_GROUPS):
        m1 = jnp.max(grp[g], axis=0, keepdims=True)
        first = jnp.min(jnp.where(grp[g] == m1, sub, per_group), axis=0, keepdims=True)
        m2 = jnp.max(jnp.where(sub == first, neg, grp[g]), axis=0, keepdims=True)
        gscore.append(m1 + m2)
    masked = []
    for g in range(N_GROUPS):
        beaten = jnp.zeros((1, tm), jnp.int32)
        for o in range(N_GROUPS):
            if o == g:
                continue
            wins = (gscore[o] > gscore[g]) | ((gscore[o] == gscore[g]) & (o < g))
            beaten = beaten + wins.astype(jnp.int32)
        masked.append(jnp.where(beaten < TOPK_GROUPS, grp[g], neg))

    eid = [sub + g * per_group for g in range(N_GROUPS)]
    member = [jnp.zeros((per_group, tm), F32) for _ in range(N_GROUPS)]
    idx_rows, score_rows = [], []
    for _ in range(TOP_K):
        m = masked[0]
        for g in range(1, N_GROUPS):
            m = jnp.maximum(m, masked[g])
        m = jnp.max(m, axis=0, keepdims=True)
        cand = jnp.where(masked[0] == m, eid[0], n_exp)
        for g in range(1, N_GROUPS):
            cand = jnp.minimum(cand, jnp.where(masked[g] == m, eid[g], n_exp))
        first = jnp.min(cand, axis=0, keepdims=True)
        picked = jnp.zeros((per_group, tm), F32)
        for g in range(N_GROUPS):
            sel = eid[g] == first
            picked = picked + jnp.where(sel, sc_g[g], 0.0)
            member[g] = jnp.where(sel, 1.0, member[g])
            masked[g] = jnp.where(sel, neg, masked[g])
        idx_rows.append(first)
        score_rows.append(jnp.sum(picked, axis=0, keepdims=True))

    total = score_rows[0]
    for k in range(1, TOP_K):
        total = total + score_rows[k]

    memb = jnp.concatenate(member, axis=0)
    earlier = jnp.where(lax.broadcasted_iota(jnp.int32, (tm, tm), 0)
                        < lax.broadcasted_iota(jnp.int32, (tm, tm), 1), 1.0, 0.0).astype(BF16)
    pos = _bdot(memb.astype(BF16), earlier) + carry_ref[...]
    pos_g = [pos[g * per_group:(g + 1) * per_group, :] for g in range(N_GROUPS)]
    for k in range(TOP_K):
        acc = jnp.zeros((per_group, tm), F32)
        for g in range(N_GROUPS):
            acc = acc + jnp.where(eid[g] == idx_rows[k], pos_g[g], 0.0)
        rank_ref[k:k + 1, :] = jnp.sum(acc, axis=0, keepdims=True).astype(jnp.int32)
        idx_ref[k:k + 1, :] = idx_rows[k]
        wt_ref[k:k + 1, :] = score_rows[k] / total * ROUTED_SCALE
    carry_ref[...] += jnp.sum(memb, axis=1, keepdims=True)
    cnt_ref[...] = carry_ref[...].astype(jnp.int32)


def _route(x, mod, w_router, b_corr):
    nb, s, d = x.shape
    t = nb * s
    n_exp = w_router.shape[1]
    tm = _tile(TM_ROUTE, s)
    per_b = s // tm
    w_t = w_router.T
    w_hi = w_t.astype(BF16)
    w_lo = (w_t - w_hi.astype(F32)).astype(BF16)
    kern = functools.partial(_router_kernel, n_exp=n_exp, tm=tm)
    out_spec = pl.BlockSpec((TOP_K, tm), lambda i: (0, i))
    return pl.pallas_call(
        kern,
        grid=(t // tm,),
        in_specs=[
            pl.BlockSpec((tm, d), lambda i: (i, 0)),
            pl.BlockSpec((1, 1, 1, d), lambda i: (3, i // per_b, 0, 0)),
            pl.BlockSpec((1, 1, 1, d), lambda i: (4, i // per_b, 0, 0)),
            pl.BlockSpec((n_exp, d), lambda i: (0, 0)),
            pl.BlockSpec((n_exp, d), lambda i: (0, 0)),
            pl.BlockSpec((n_exp, 1), lambda i: (0, 0)),
        ],
        out_specs=[out_spec, out_spec, out_spec, pl.BlockSpec((n_exp, 1), lambda i: (0, 0))],
        out_shape=[
            jax.ShapeDtypeStruct((TOP_K, t), jnp.int32),
            jax.ShapeDtypeStruct((TOP_K, t), F32),
            jax.ShapeDtypeStruct((TOP_K, t), jnp.int32),
            jax.ShapeDtypeStruct((n_exp, 1), jnp.int32),
        ],
        scratch_shapes=[pltpu.VMEM((n_exp, 1), F32)],
        compiler_params=_cparams(1),
        name="moe_router",
    )(x.reshape(t, d), mod, mod, w_hi, w_lo, b_corr.reshape(n_exp, 1).astype(F32))


def _slots_kernel(ps_ref, idx_ref, rank_ref, dest_ref, *, n_exp):
    idx = idx_ref[...]

    def add_start(e, acc):
        return acc + jnp.where(idx == e, ps_ref[e], 0)

    dest_ref[...] = lax.fori_loop(0, n_exp, add_start, rank_ref[...])


def _slots(idx, rank, pad_starts):
    k, t = idx.shape
    tm = _tile(2048, t)
    grid_spec = pltpu.PrefetchScalarGridSpec(
        num_scalar_prefetch=1,
        grid=(t // tm,),
        in_specs=[pl.BlockSpec((k, tm), lambda i, ps: (0, i)), pl.BlockSpec((k, tm), lambda i, ps: (0, i))],
        out_specs=pl.BlockSpec((k, tm), lambda i, ps: (0, i)),
    )
    return pl.pallas_call(
        functools.partial(_slots_kernel, n_exp=pad_starts.shape[0]),
        grid_spec=grid_spec,
        out_shape=jax.ShapeDtypeStruct((k, t), jnp.int32),
        compiler_params=_cparams(1),
        name="moe_slots",
    )(pad_starts, idx, rank)


def _to_slabs(value):
    n_seg = value.shape[1] // LANES
    by_seg = jnp.stack([value[:, s * LANES:(s + 1) * LANES] for s in range(n_seg)], axis=0)
    return jnp.swapaxes(by_seg, 0, 1)


def _from_slabs(slabs):
    by_seg = jnp.swapaxes(slabs, 0, 1)
    return jnp.concatenate([by_seg[s] for s in range(slabs.shape[1])], axis=1)


def _dispatch_kernel(pend_ref, padded_ref, x_ref, sh_ref, sc_ref, dest_ref, wsg_ref, wsu_ref, wsd_ref,
                     xs_ref, shared_ref, tok_ref, zero_ref, sem, zsem, *, n_exp, tm, bm):
    i = pl.program_id(0)

    def zero_copy(e):
        return pltpu.make_async_copy(zero_ref, xs_ref.at[pl.ds(pend_ref[e] - bm, bm)], zsem)

    @pl.when(i == 0)
    def _():
        zero_ref[...] = jnp.zeros_like(zero_ref)

        def start(e, _):
            @pl.when(padded_ref[e] > 0)
            def _():
                zero_copy(e).start()
            return 0

        def wait(e, _):
            @pl.when(padded_ref[e] > 0)
            def _():
                zero_copy(e).wait()
            return 0

        def tail_copy(blk):
            return pltpu.make_async_copy(zero_ref, xs_ref.at[pl.ds(blk * bm, bm)], zsem)

        def start_tail(blk, _):
            tail_copy(blk).start()
            return 0

        def wait_tail(blk, _):
            tail_copy(blk).wait()
            return 0

        first_unused = pend_ref[n_exp - 1] // bm
        n_blk = xs_ref.shape[0] // bm
        lax.fori_loop(0, n_exp, start, 0)
        lax.fori_loop(first_unused, n_blk, start_tail, 0)
        lax.fori_loop(0, n_exp, wait, 0)
        lax.fori_loop(first_unused, n_blk, wait_tail, 0)

    n_steps = pl.num_programs(0)
    slot = i % 2

    def drain(buf):
        for k in range(TOP_K):
            pltpu.make_async_copy(tok_ref.at[buf], xs_ref.at[pl.ds(0, tm)], sem.at[buf]).wait()

    @pl.when(i >= 2)
    def _():
        drain(slot)

    tok = x_ref[...] * (1.0 + sc_ref[0, 0]) + sh_ref[0, 0]
    tok_ref[slot] = _to_slabs(tok)
    for r in range(tm):
        for k in range(TOP_K):
            pltpu.make_async_copy(tok_ref.at[slot, r], xs_ref.at[dest_ref[k, r]],
                                  sem.at[slot]).start(priority=k % 2)
    tok = tok.astype(BF16)
    hid = (_silu(_bdot(tok, wsg_ref[...])) * _bdot(tok, wsu_ref[...])).astype(BF16)
    shared_ref[...] = _bdot(hid, wsd_ref[...])

    @pl.when(i == n_steps - 1)
    def _():
        @pl.when(i >= 1)
        def _():
            drain(1 - slot)
        drain(slot)


def _dispatch(x, mod, dest, pad_ends, padded, n_slots, bm, ws_gate, ws_up, ws_down):
    nb, s, d = x.shape
    f = ws_gate.shape[-1]
    t = nb * s
    tm = _tile(TM_DISPATCH, s)
    per_b = s // tm
    n_exp = pad_ends.shape[0]
    n_seg = d // LANES
    assert n_seg % SUBLANES == 0, "token slabs must be whole (8, 128) tiles"
    kern = functools.partial(_dispatch_kernel, n_exp=n_exp, tm=tm, bm=bm)
    grid_spec = pltpu.PrefetchScalarGridSpec(
        num_scalar_prefetch=2,
        grid=(t // tm,),
        in_specs=[
            pl.BlockSpec((tm, d), lambda i, pe, pd: (i, 0)),
            pl.BlockSpec((1, 1, 1, d), lambda i, pe, pd: (3, i // per_b, 0, 0)),
            pl.BlockSpec((1, 1, 1, d), lambda i, pe, pd: (4, i // per_b, 0, 0)),
            pl.BlockSpec((TOP_K, tm), lambda i, pe, pd: (0, i), memory_space=pltpu.SMEM),
            pl.BlockSpec((d, f), lambda i, pe, pd: (0, 0)),
            pl.BlockSpec((d, f), lambda i, pe, pd: (0, 0)),
            pl.BlockSpec((f, d), lambda i, pe, pd: (0, 0)),
        ],
        out_specs=[pl.BlockSpec(memory_space=pl.ANY), pl.BlockSpec((tm, d), lambda i, pe, pd: (i, 0))],
        scratch_shapes=[
            pltpu.VMEM((2, tm, n_seg, LANES), F32),
            pltpu.VMEM((bm, n_seg, LANES), F32),
            pltpu.SemaphoreType.DMA((2,)),
            pltpu.SemaphoreType.DMA(()),
        ],
    )
    return pl.pallas_call(
        kern,
        grid_spec=grid_spec,
        out_shape=[jax.ShapeDtypeStruct((n_slots, n_seg, LANES), F32), jax.ShapeDtypeStruct((t, d), F32)],
        compiler_params=_cparams(1),
        name="moe_dispatch",
    )(pad_ends, padded, x.reshape(t, d), mod, mod, dest, ws_gate, ws_up, ws_down)


def _experts_kernel(be_ref, nu_ref, xs_ref, wg_ref, wu_ref, wd_ref, ys_ref):
    b = pl.program_id(0)

    @pl.when(b < nu_ref[0])
    def _():
        x = _from_slabs(xs_ref[...]).astype(BF16)
        g = _bdot(x, wg_ref[0])
        u = _bdot(x, wu_ref[0])
        ys_ref[...] = _to_slabs(_bdot((_silu(g) * u).astype(BF16), wd_ref[0]))

    @pl.when(b >= nu_ref[0])
    def _():
        ys_ref[...] = jnp.zeros_like(ys_ref)


def _grouped_experts(xs, blk_e, n_used, w_gate, w_up, w_down, bm):
    d, f = w_gate.shape[1:]
    n_seg = d // LANES
    n_blk = xs.shape[0] // bm

    def blk(b, be, nu):
        return jnp.minimum(b, nu[0] - 1)

    grid_spec = pltpu.PrefetchScalarGridSpec(
        num_scalar_prefetch=2,
        grid=(n_blk,),
        in_specs=[
            pl.BlockSpec((bm, n_seg, LANES), lambda b, be, nu: (blk(b, be, nu), 0, 0)),
            pl.BlockSpec((1, d, f), lambda b, be, nu: (be[blk(b, be, nu)], 0, 0)),
            pl.BlockSpec((1, d, f), lambda b, be, nu: (be[blk(b, be, nu)], 0, 0)),
            pl.BlockSpec((1, f, d), lambda b, be, nu: (be[blk(b, be, nu)], 0, 0)),
        ],
        out_specs=pl.BlockSpec((bm, n_seg, LANES), lambda b, be, nu: (b, 0, 0)),
    )
    return pl.pallas_call(
        _experts_kernel,
        grid_spec=grid_spec,
        out_shape=jax.ShapeDtypeStruct(xs.shape, F32),
        compiler_params=_cparams(1),
        name="moe_experts",
    )(blk_e, n_used, xs, w_gate, w_up, w_down)


def _combine_kernel(x_ref, gt_ref, dest_ref, dest_next_ref, wt_ref, ys_ref, shared_ref,
                    lng_ref, lnb_ref, xo_ref, rows_ref, sem, *, tm, alpha):
    i = pl.program_id(0)
    n_steps = pl.num_programs(0)
    slot = i % 2

    def gather(d_ref, buf):
        def issue(r, _):
            for k in range(TOP_K):
                pltpu.make_async_copy(ys_ref.at[d_ref[k, r]], rows_ref.at[buf, k, r],
                                      sem.at[buf]).start(priority=k % 2)
            return 0
        lax.fori_loop(0, tm, issue, 0)

    @pl.when(i == 0)
    def _():
        gather(dest_ref, 0)

    for r in range(tm):
        for k in range(TOP_K):
            pltpu.make_async_copy(ys_ref.at[dest_next_ref[k, r]], rows_ref.at[1 - slot, k, r],
                                  sem.at[1 - slot]).start(priority=k % 2)

    x = x_ref[...]
    f = shared_ref[...]

    for k in range(TOP_K):
        pltpu.make_async_copy(ys_ref.at[pl.ds(0, tm)], rows_ref.at[slot, k], sem.at[slot]).wait()
    n_seg = rows_ref.shape[3]
    wt = wt_ref[...]
    routed = jnp.broadcast_to(wt[:, 0:1, :], (tm, n_seg, LANES)) * rows_ref[slot, 0]
    for k in range(1, TOP_K):
        routed = routed + jnp.broadcast_to(wt[:, k:k + 1, :], (tm, n_seg, LANES)) * rows_ref[slot, k]
    f = f + _from_slabs(routed)
    v = alpha * x + gt_ref[0, 0] * f
    xo_ref[...] = _layer_norm(v, lng_ref[...], lnb_ref[...])

    @pl.when(i == n_steps - 1)
    def _():
        for k in range(TOP_K):
            pltpu.make_async_copy(ys_ref.at[pl.ds(0, tm)], rows_ref.at[1 - slot, k], sem.at[1 - slot]).wait()


def _combine(x, mod, dest, wt_splat, ys, shared, ln_g, ln_b, alpha):
    nb, s, d = x.shape
    t = nb * s
    tm = _tile(TM_COMBINE, s)
    per_b = s // tm
    n_seg = d // LANES
    kern = functools.partial(_combine_kernel, tm=tm, alpha=alpha)
    vspec = pl.BlockSpec((1, d), lambda i: (0, 0))
    n_steps = t // tm
    out = pl.pallas_call(
        kern,
        grid=(n_steps,),
        in_specs=[
            pl.BlockSpec((tm, d), lambda i: (i, 0)),
            pl.BlockSpec((1, 1, 1, d), lambda i: (5, i // per_b, 0, 0)),
            pl.BlockSpec((TOP_K, tm), lambda i: (0, i), memory_space=pltpu.SMEM),
            pl.BlockSpec((TOP_K, tm), lambda i: (0, jnp.minimum(i + 1, n_steps - 1)), memory_space=pltpu.SMEM),
            pl.BlockSpec((tm, TOP_K, LANES), lambda i: (i, 0, 0)),
            pl.BlockSpec(memory_space=pl.ANY),
            pl.BlockSpec((tm, d), lambda i: (i, 0)),
            vspec, vspec,
        ],
        out_specs=pl.BlockSpec((tm, d), lambda i: (i, 0)),
        out_shape=jax.ShapeDtypeStruct((t, d), F32),
        scratch_shapes=[pltpu.VMEM((2, TOP_K, tm, n_seg, LANES), F32), pltpu.SemaphoreType.DMA((2,))],
        compiler_params=_cparams(1),
        name="moe_combine",
    )(x.reshape(t, d), mod, dest, dest, wt_splat, ys, shared, ln_g.reshape(1, d), ln_b.reshape(1, d))
    return out.reshape(nb, s, d)


def _moe_block(x, mod, w_router, b_router, w_gate, w_up, w_down, ws_gate, ws_up, ws_down, ln_g, ln_b, alpha):
    nb, s, d = x.shape
    t = nb * s
    n_exp = w_router.shape[1]
    bm = _tile(MOE_BLOCK, t * TOP_K)
    idx, wt, rank, counts = _route(x, mod, w_router, b_router)

    counts = counts.reshape(n_exp)
    padded = (counts + bm - 1) // bm * bm
    pad_ends = jnp.cumsum(padded).astype(jnp.int32)
    pad_starts = pad_ends - padded
    dest = _slots(idx, rank, pad_starts.astype(jnp.int32))
    n_blk = t * TOP_K // bm + n_exp
    n_used = (pad_ends[-1:] // bm).astype(jnp.int32)
    blk_start = jnp.arange(n_blk, dtype=jnp.int32) * bm
    blk_e = jnp.sum((pad_ends[None, :] <= blk_start[:, None]).astype(jnp.int32), axis=1)
    blk_e = jnp.minimum(blk_e, n_exp - 1).astype(jnp.int32)

    xs, shared = _dispatch(x, mod, dest, pad_ends, padded.astype(jnp.int32), n_blk * bm, bm,
                           ws_gate.astype(BF16), ws_up.astype(BF16), ws_down.astype(BF16))
    ys = _grouped_experts(xs, blk_e, n_used, w_gate, w_up, w_down, bm)
    wt_splat = jnp.broadcast_to(wt.T[:, :, None], (t, TOP_K, LANES))
    return _combine(x, mod, dest, wt_splat, ys, shared, ln_g, ln_b, alpha)


def kernel(x, c, ctx, c_ctx, w_mod, b_mod, ln1_g, ln1_b, ln2_g, ln2_b, rg_w_in, rg_b_in, rg_conv_w, rg_conv_b, rg_w_gates, rg_b_gates, rg_lambda, rg_w_out, rg_b_out, cf_w_pw1, cf_b_pw1, cf_w_dw, cf_b_dw, cf_norm_g, cf_norm_b, cf_w_pw2, cf_b_pw2, moe_w_router, moe_b_router, moe_w_gate, moe_w_up, moe_w_down, sh_w_gate, sh_w_up, sh_w_down):
    nb, s, d = x.shape
    depth = w_mod.shape[0]
    assert depth == 2 and nb + 1 <= SUBLANES
    alpha = (2 * depth) ** 0.25

    c_rows = jnp.zeros((SUBLANES, d), F32).at[:nb].set(c).at[nb].set(c_ctx)
    mod_all = _adaln_vectors(c_rows, w_mod, b_mod)
    mod_all = mod_all.reshape(depth, SUBLANES, 6, 1, d).transpose(0, 2, 1, 3, 4)
    mod_lat = [mod_all[l, :, :nb] for l in range(depth)]
    mod_ctx = jnp.broadcast_to(mod_all[0, :, nb:nb + 1], (6, nb, 1, d))

    n_h, hd = rg_w_gates.shape[3], rg_w_gates.shape[4]
    r = n_h * hd
    w_in = rg_w_in[0].astype(BF16)
    wg = [jnp.concatenate([rg_w_gates[0, dr, 0], rg_w_gates[0, dr, 1]], axis=-1).astype(BF16) for dr in range(2)]
    bg = [jnp.concatenate([rg_b_gates[0, dr, 0], rg_b_gates[0, dr, 1]], axis=-1).reshape(n_h, 1, 2 * hd)
          for dr in range(2)]
    conv_w, conv_b = rg_conv_w[0], rg_conv_b[0]
    zero_state = jnp.zeros((nb, n_h, 1, hd), F32)

    proj_c, _ = _modulated_projection(ctx, mod_ctx, w_in, rg_b_in[0], BF16)
    _, hcf = _rglru_scan(proj_c, conv_w, conv_b, wg[0], bg[0], rg_lambda[0, 0], zero_state, reverse=False)
    _, hcb = _rglru_scan(proj_c, conv_w, conv_b, wg[1], bg[1], rg_lambda[0, 1], zero_state, reverse=True)

    expert_stacks = (moe_w_gate, moe_w_up, moe_w_down)
    proj_x, experts0 = _modulated_projection(x, mod_lat[0], w_in, rg_b_in[0], BF16, expert_stacks, 0)
    hxf, _ = _rglru_scan(proj_x, conv_w, conv_b, wg[0], bg[0], rg_lambda[0, 0], hcf, reverse=False)
    readout = (hxf, rg_w_out[0].astype(BF16), rg_b_out[0], x, mod_lat[0], ln1_g[0], ln1_b[0])
    x, _ = _rglru_scan(proj_x, conv_w, conv_b, wg[1], bg[1], rg_lambda[0, 1], hcb, reverse=True,
                       readout=readout, alpha=alpha)
    x = _moe_block(x, mod_lat[0], moe_w_router[0], moe_b_router[0],
                   *experts0,
                   sh_w_gate[0], sh_w_up[0], sh_w_down[0], ln2_g[0], ln2_b[0], alpha)

    z, experts1 = _modulated_glu(x, mod_lat[1], cf_w_pw1[0].astype(BF16), cf_b_pw1[0], expert_stacks, 1)
    x = _conformer_tail(z, cf_w_dw[0], cf_b_dw[0], cf_norm_g[0], cf_norm_b[0], cf_w_pw2[0].astype(BF16),
                        cf_b_pw2[0], x, mod_lat[1], ln1_g[1], ln1_b[1], alpha)
    x = _moe_block(x, mod_lat[1], moe_w_router[1], moe_b_router[1],
                   *experts1,
                   sh_w_gate[1], sh_w_up[1], sh_w_down[1], ln2_g[1], ln2_b[1], alpha)
    return x
```

```python
import functools

import jax
import jax.numpy as jnp
from jax import lax
from jax.experimental import pallas as pl
from jax.experimental.pallas import tpu as pltpu

F32 = jnp.float32
BF16 = jnp.bfloat16

RG_C = 8.0
N_GROUPS = 8
TOPK_GROUPS = 4
TOP_K = 8
ROUTED_SCALE = 2.5
LN_EPS = 1e-5

VMEM_LIMIT_BYTES = 56 * 1024 * 1024
SUBLANES = 8
LANES = 128
BF16_ROWS = 16

TM_PROJ = 512
TM_SCAN = 512
HEADS_PER_STEP = 2
TM_CONV = 256
TM_ROUTE = 512
TM_DISPATCH = 256
TM_COMBINE = 128
MOE_BLOCK = 512


def _tile(default, n):
    t = min(default, n)
    assert n % t == 0, (default, n)
    return t


def _cparams(n_axes):
    return pltpu.CompilerParams(
        dimension_semantics=("arbitrary",) * n_axes,
        vmem_limit_bytes=VMEM_LIMIT_BYTES,
    )


def _sigmoid(v):
    return 0.5 * jnp.tanh(0.5 * v) + 0.5


def _sqrt_nonneg(v):
    return jnp.where(v > 0.0, v * lax.rsqrt(v), 0.0)


def _silu(v):
    return v * _sigmoid(v)


def _gelu_tanh(v):
    c = 0.7978845608028654
    return 0.5 * v * (1.0 + jnp.tanh(c * (v + 0.044715 * (v * v * v))))


def _softplus(v):
    return jnp.maximum(v, 0.0) + jnp.log(1.0 + jnp.exp(-jnp.abs(v)))


def _layer_norm(v, g, b):
    mu = jnp.mean(v, axis=-1, keepdims=True)
    d = v - mu
    var = jnp.mean(d * d, axis=-1, keepdims=True)
    return d * lax.rsqrt(var + LN_EPS) * g + b


def _bdot(a, b):
    return jnp.dot(a, b, preferred_element_type=F32)


def _mod_kernel(c_ref, w_ref, b_ref, o_ref):
    s = _silu(c_ref[...])
    o_ref[0] = _bdot(s.astype(BF16), w_ref[0].astype(BF16)) + b_ref[0]


def _adaln_vectors(c_rows, w_mod, b_mod):
    depth, d, d6 = w_mod.shape
    tn = _tile(1024, d6)
    return pl.pallas_call(
        _mod_kernel,
        grid=(depth, d6 // tn),
        in_specs=[
            pl.BlockSpec((SUBLANES, d), lambda l, n: (0, 0)),
            pl.BlockSpec((1, d, tn), lambda l, n: (l, 0, n)),
            pl.BlockSpec((1, 1, tn), lambda l, n: (l, 0, n)),
        ],
        out_specs=pl.BlockSpec((1, SUBLANES, tn), lambda l, n: (l, 0, n)),
        out_shape=jax.ShapeDtypeStruct((depth, SUBLANES, d6), F32),
        compiler_params=_cparams(2),
        name="adaln_vectors",
    )(c_rows, w_mod, b_mod.reshape(depth, 1, d6))


def _mod_spec(chunk, batch_of):
    def index(*ids):
        return (chunk, batch_of(*ids), 0, 0)
    return index


class _SideCast:
    def __init__(self, stacks, layer, grid):
        self.n = len(stacks)
        self.shapes = [w.shape[1:] for w in stacks]
        n_steps = 1
        for g in grid:
            n_steps *= g

        def step(*ids):
            lin = ids[0]
            for g, v in zip(grid[1:], ids[1:]):
                lin = lin * g + v
            return lin

        self.in_specs, self.out_specs, self.out_shape, self.args = [], [], [], []
        for w in stacks:
            n_l, n_e, a, b = w.shape
            rows = n_e * a
            blk = rows // n_steps
            assert rows % n_steps == 0 and blk % BF16_ROWS == 0, (w.shape, n_steps)
            self.in_specs.append(pl.BlockSpec((1, blk, b), lambda *ids: (layer, step(*ids), 0)))
            self.out_specs.append(pl.BlockSpec((blk, b), lambda *ids: (step(*ids), 0)))
            self.out_shape.append(jax.ShapeDtypeStruct((rows, b), BF16))
            self.args.append(w.reshape(n_l, rows, b))

    @staticmethod
    def run(src_refs, dst_refs):
        for src, dst in zip(src_refs, dst_refs):
            dst[...] = src[0].astype(dst.dtype)

    def unpack(self, outs):
        return [o.reshape(shape) for o, shape in zip(outs, self.shapes)]


def _proj_kernel(x_ref, sh_ref, sc_ref, w_ref, b_ref, *rest, n_side):
    o_ref = rest[n_side]
    h = x_ref[0] * (1.0 + sc_ref[0, 0]) + sh_ref[0, 0]
    o_ref[0] = (_bdot(h.astype(BF16), w_ref[...]) + b_ref[...]).astype(o_ref.dtype)
    _SideCast.run(rest[:n_side], rest[n_side + 1:])


def _modulated_projection(x, mod, w_bf16, bias, out_dtype, side_stacks=(), side_layer=0):
    nb, s, d = x.shape
    n = w_bf16.shape[1]
    tm = _tile(TM_PROJ, s)
    tn = _tile(2560, n)
    dm = mod.shape[-1]
    grid = (n // tn, nb, s // tm)
    side = _SideCast(side_stacks, side_layer, grid)
    outs = pl.pallas_call(
        functools.partial(_proj_kernel, n_side=side.n),
        grid=grid,
        in_specs=[
            pl.BlockSpec((1, tm, d), lambda j, b, i: (b, i, 0)),
            pl.BlockSpec((1, 1, 1, dm), lambda j, b, i: (0, b, 0, 0)),
            pl.BlockSpec((1, 1, 1, dm), lambda j, b, i: (1, b, 0, 0)),
            pl.BlockSpec((d, tn), lambda j, b, i: (0, j)),
            pl.BlockSpec((1, tn), lambda j, b, i: (0, j)),
        ] + side.in_specs,
        out_specs=[pl.BlockSpec((1, tm, tn), lambda j, b, i: (b, i, j))] + side.out_specs,
        out_shape=[jax.ShapeDtypeStruct((nb, s, n), out_dtype)] + side.out_shape,
        compiler_params=_cparams(3),
        name="modulated_projection",
    )(x, mod, mod, w_bf16, bias.reshape(1, n), *side.args)
    return outs[0], side.unpack(outs[1:])


def _rglru_kernel(*refs, reverse, readout, n_t, tm, alpha):
    if readout:
        (cur_ref, prev_ref, next_ref, perm_ref, cw_ref, cb_ref, wg_ref, bg_ref, lam_ref, h0_ref,
         hf_ref, gate_ref, wout_ref, bout_ref, x_ref, g1_ref, lng_ref, lnb_ref,
         xo_ref, hlast_ref,
         win_ref, a_ref, hl_ref, carry_ref, acc_ref) = refs
    else:
        (cur_ref, prev_ref, next_ref, perm_ref, cw_ref, cb_ref, wg_ref, bg_ref, lam_ref, h0_ref,
         ho_ref, hlast_ref,
         win_ref, a_ref, hl_ref, carry_ref) = refs
    i = pl.program_id(1)
    h = pl.program_id(2)
    n_h = pl.num_programs(2)
    ti = (n_t - 1 - i) if reverse else i
    hd = cur_ref.shape[-1]
    head_dim = wg_ref.shape[1]
    group = hd // head_dim
    chunk = tm // SUBLANES
    sub = lax.broadcasted_iota(jnp.int32, (SUBLANES, hd), 0)

    rnn = _bdot(perm_ref[...], cur_ref[0])
    prev_last = jnp.where(ti == 0, 0.0, prev_ref[0].astype(F32)[BF16_ROWS - 1:, :])
    nxt = jnp.where(ti == n_t - 1, 0.0, next_ref[0].astype(F32)[:2, :])
    win_ref[0:SUBLANES, :] = jnp.where(
        sub == 0, prev_last, pltpu.roll(rnn[tm - SUBLANES:, :], 1, axis=0))
    win_ref[SUBLANES:SUBLANES + tm, :] = rnn
    for extra in range(2):
        lo = SUBLANES + tm + extra * SUBLANES
        win_ref[lo:lo + SUBLANES, :] = jnp.where(
            sub == SUBLANES - 1, nxt[extra:extra + 1, :],
            pltpu.roll(rnn[extra * SUBLANES:(extra + 1) * SUBLANES, :], SUBLANES - 1, axis=0))
    n_taps = cw_ref.shape[0]
    xb = cb_ref[...] + cw_ref[0:1, :] * win_ref[pl.ds(0, tm), :]
    for k in range(1, n_taps):
        xb = xb + cw_ref[k:k + 1, :] * win_ref[pl.ds(k * SUBLANES, tm), :]

    xb16 = xb.astype(BF16)
    pre = [_bdot(xb16[:, q * head_dim:(q + 1) * head_dim], wg_ref[q]) + bg_ref[q] for q in range(group)]
    r = _sigmoid(jnp.concatenate([p[:, :head_dim] for p in pre], axis=1))
    gi = _sigmoid(jnp.concatenate([p[:, head_dim:] for p in pre], axis=1))
    log_a = (-RG_C) * r * _softplus(-lam_ref[...])
    a = jnp.exp(log_a)
    a_ref[...] = a
    hl_ref[...] = _sqrt_nonneg(1.0 - a * a) * gi * xb

    if readout:
        gate_act = _gelu_tanh(_bdot(perm_ref[...], gate_ref[0]))

    def scan_step(jj, carry):
        a_cum, h_loc = carry
        j = (chunk - 1 - jj) if reverse else jj
        rows = pl.ds(pl.multiple_of(j * SUBLANES, SUBLANES), SUBLANES)
        aj = a_ref[rows, :]
        h_loc = aj * h_loc + hl_ref[rows, :]
        a_cum = a_cum * aj
        hl_ref[rows, :] = h_loc
        a_ref[rows, :] = a_cum
        return a_cum, h_loc

    a_end, h_end = lax.fori_loop(
        0, chunk, scan_step, (jnp.ones((SUBLANES, hd), F32), jnp.zeros((SUBLANES, hd), F32)),
        unroll=SUBLANES)

    @pl.when(i == 0)
    def _():
        for q in range(group):
            carry_ref[h * group + q] = h0_ref[0, h * group + q]

    state = jnp.concatenate([carry_ref[h * group + q] for q in range(group)], axis=1)
    entry = jnp.zeros((SUBLANES, hd), F32)
    for s in (range(SUBLANES - 1, -1, -1) if reverse else range(SUBLANES)):
        entry = jnp.where(sub == s, state, entry)
        state = a_end[s:s + 1, :] * state + h_end[s:s + 1, :]
    for q in range(group):
        carry_ref[h * group + q] = state[:, q * head_dim:(q + 1) * head_dim]
        hlast_ref[0, h * group + q] = state[:, q * head_dim:(q + 1) * head_dim]

    h_full = hl_ref[...] + a_ref[...] * jnp.concatenate([entry] * chunk, axis=0)
    if not readout:
        ho_ref[0] = h_full.astype(ho_ref.dtype)
    else:
        h_sum = h_full + hf_ref[0].astype(F32)
        contrib = _bdot((h_sum * gate_act).astype(BF16), wout_ref[...])
        n_lc = acc_ref.shape[0]

        @pl.when(h == 0)
        def _():
            for c in range(n_lc):
                acc_ref[c] = contrib[:, c * LANES:(c + 1) * LANES]

        @pl.when(h != 0)
        def _():
            for c in range(n_lc):
                acc_ref[c] += contrib[:, c * LANES:(c + 1) * LANES]

        @pl.when(h == n_h - 1)
        def _():
            y = jnp.concatenate(
                [jnp.concatenate(
                    [acc_ref[c, pl.ds(q * SUBLANES * SUBLANES + s, SUBLANES, stride=SUBLANES), :]
                     for s in range(SUBLANES) for q in range(chunk // SUBLANES)], axis=0)
                 for c in range(n_lc)], axis=1) + bout_ref[...]
            v = alpha * x_ref[0] + g1_ref[0, 0] * y
            xo_ref[0] = _layer_norm(v, lng_ref[...], lnb_ref[...])


def _rglru_scan(proj, conv_w, conv_b, wg, bg, lam, h0, *, reverse, h_dtype=BF16, readout=None, alpha=None):
    nb, s, r2 = proj.shape
    r = r2 // 2
    n_h, hd = wg.shape[0], wg.shape[1]
    tm = _tile(TM_SCAN, s)
    n_t = s // tm
    hb = tm // BF16_ROWS
    n_hb = s // BF16_ROWS

    def tix(i):
        return (n_t - 1 - i) if reverse else i

    rows = jnp.arange(tm, dtype=jnp.int32)
    time_of_row = (rows % SUBLANES) * (tm // SUBLANES) + rows // SUBLANES
    perm = (time_of_row[:, None] == rows[None, :]).astype(BF16)
    perm_spec = pl.BlockSpec((tm, tm), lambda b, i, h: (0, 0))

    group = HEADS_PER_STEP if n_h % HEADS_PER_STEP == 0 else 1
    n_g, gw = n_h // group, group * hd
    in_specs = [
        pl.BlockSpec((1, tm, gw), lambda b, i, h: (b, tix(i), n_g + h)),
        pl.BlockSpec((1, BF16_ROWS, gw), lambda b, i, h: (b, jnp.maximum(tix(i) * hb - 1, 0), n_g + h)),
        pl.BlockSpec((1, BF16_ROWS, gw), lambda b, i, h: (b, jnp.minimum((tix(i) + 1) * hb, n_hb - 1), n_g + h)),
        perm_spec,
        pl.BlockSpec((conv_w.shape[0], gw), lambda b, i, h: (0, h)),
        pl.BlockSpec((1, gw), lambda b, i, h: (0, h)),
        pl.BlockSpec((group, hd, 2 * hd), lambda b, i, h: (h, 0, 0)),
        pl.BlockSpec((group, 1, 2 * hd), lambda b, i, h: (h, 0, 0)),
        pl.BlockSpec((1, gw), lambda b, i, h: (0, h)),
        pl.BlockSpec((1, n_h, 1, hd), lambda b, i, h: (b, 0, 0, 0)),
    ]
    args = [proj, proj, proj, perm, conv_w, conv_b.reshape(1, r), wg, bg, lam.reshape(1, r), h0]
    assert conv_w.shape[0] == 4 and tm // SUBLANES >= 2, "window below covers taps at time offsets -1 .. +2"
    scratch = [
        pltpu.VMEM((tm + 3 * SUBLANES, gw), F32),
        pltpu.VMEM((tm, gw), F32),
        pltpu.VMEM((tm, gw), F32),
        pltpu.VMEM((n_h, 1, hd), F32),
    ]
    hlast_spec = pl.BlockSpec((1, n_h, 1, hd), lambda b, i, h: (b, 0, 0, 0))
    hlast_shape = jax.ShapeDtypeStruct((nb, n_h, 1, hd), F32)
    if readout is None:
        out_specs = [pl.BlockSpec((1, tm, gw), lambda b, i, h: (b, tix(i), h)), hlast_spec]
        out_shape = [jax.ShapeDtypeStruct((nb, s, r), h_dtype), hlast_shape]
    else:
        h_other, w_out, b_out, x, mod, ln_g, ln_b = readout
        d = x.shape[-1]
        in_specs += [
            pl.BlockSpec((1, tm, gw), lambda b, i, h: (b, tix(i), h)),
            pl.BlockSpec((1, tm, gw), lambda b, i, h: (b, tix(i), h)),
            pl.BlockSpec((gw, d), lambda b, i, h: (h, 0)),
            pl.BlockSpec((1, d), lambda b, i, h: (0, 0)),
            pl.BlockSpec((1, tm, d), lambda b, i, h: (b, tix(i), 0)),
            pl.BlockSpec((1, 1, 1, d), lambda b, i, h: (2, b, 0, 0)),
            pl.BlockSpec((1, d), lambda b, i, h: (0, 0)),
            pl.BlockSpec((1, d), lambda b, i, h: (0, 0)),
        ]
        args += [h_other, proj, w_out, b_out.reshape(1, d), x, mod, ln_g.reshape(1, d), ln_b.reshape(1, d)]
        assert (tm // SUBLANES) % SUBLANES == 0, "readout un-interleaves 8 vreg rows at a time"
        scratch += [pltpu.VMEM((d // LANES, tm, LANES), F32)]
        out_specs = [pl.BlockSpec((1, tm, d), lambda b, i, h: (b, tix(i), 0)), hlast_spec]
        out_shape = [jax.ShapeDtypeStruct((nb, s, d), F32), hlast_shape]
    kern = functools.partial(_rglru_kernel, reverse=reverse, readout=readout is not None,
                             n_t=n_t, tm=tm, alpha=alpha)
    return pl.pallas_call(
        kern,
        grid=(nb, n_t, n_g),
        in_specs=in_specs,
        out_specs=out_specs,
        out_shape=out_shape,
        scratch_shapes=scratch,
        compiler_params=_cparams(3),
        name="rglru_bwd_readout" if readout is not None else ("rglru_bwd" if reverse else "rglru_fwd"),
    )(*args)


def _glu_kernel(x_ref, sh_ref, sc_ref, wv_ref, wg_ref, bv_ref, bgate_ref, *rest, n_side):
    o_ref = rest[n_side]
    h = (x_ref[0] * (1.0 + sc_ref[0, 0]) + sh_ref[0, 0]).astype(BF16)
    val = _bdot(h, wv_ref[...]) + bv_ref[...]
    gate = _bdot(h, wg_ref[...]) + bgate_ref[...]
    o_ref[0] = (val * _sigmoid(gate)).astype(o_ref.dtype)
    _SideCast.run(rest[:n_side], rest[n_side + 1:])


def _modulated_glu(x, mod, w_bf16, bias, side_stacks=(), side_layer=0):
    nb, s, d = x.shape
    n = w_bf16.shape[1] // 2
    tm = _tile(TM_PROJ, s)
    tn = _tile(1024, n)
    nj = n // tn
    b2 = bias.reshape(1, 2 * n)
    grid = (nj, nb, s // tm)
    side = _SideCast(side_stacks, side_layer, grid)
    outs = pl.pallas_call(
        functools.partial(_glu_kernel, n_side=side.n),
        grid=grid,
        in_specs=[
            pl.BlockSpec((1, tm, d), lambda j, b, i: (b, i, 0)),
            pl.BlockSpec((1, 1, 1, d), lambda j, b, i: (0, b, 0, 0)),
            pl.BlockSpec((1, 1, 1, d), lambda j, b, i: (1, b, 0, 0)),
            pl.BlockSpec((d, tn), lambda j, b, i: (0, j)),
            pl.BlockSpec((d, tn), lambda j, b, i: (0, nj + j)),
            pl.BlockSpec((1, tn), lambda j, b, i: (0, j)),
            pl.BlockSpec((1, tn), lambda j, b, i: (0, nj + j)),
        ] + side.in_specs,
        out_specs=[pl.BlockSpec((1, tm, tn), lambda j, b, i: (b, i, j))] + side.out_specs,
        out_shape=[jax.ShapeDtypeStruct((nb, s, n), BF16)] + side.out_shape,
        compiler_params=_cparams(3),
        name="modulated_glu",
    )(x, mod, mod, w_bf16, w_bf16, b2, b2, *side.args)
    return outs[0], side.unpack(outs[1:])


def _conformer_tail_kernel(cur_ref, prev_ref, next_ref, wdw_ref, bdw_ref, ng_ref, nb_ref, w2_ref, b2_ref,
                           x_ref, g1_ref, lng_ref, lnb_ref, xo_ref, win_ref, z_ref, shift_ref, *, n_t, tm, alpha,
                           lane_chunk, row_chunk):
    i = pl.program_id(1)
    d = cur_ref.shape[-1]
    halo = BF16_ROWS
    win_ref[0:halo, :] = jnp.where(i == 0, 0.0, prev_ref[0].astype(F32))
    win_ref[halo:halo + tm, :] = cur_ref[0].astype(F32)
    win_ref[halo + tm:, :] = jnp.where(i == n_t - 1, 0.0, next_ref[0].astype(F32))
    n_taps = wdw_ref.shape[0]
    base = halo - (n_taps - 1) // 2

    n_shift = shift_ref.shape[1]
    for c in range(d // lane_chunk):
        lanes = slice(c * lane_chunk, (c + 1) * lane_chunk)
        for s in range(SUBLANES):
            shift_ref[s] = win_ref[pl.ds(s, n_shift), lanes]

        def rows_step(rc, _, lanes=lanes):
            r0 = pl.multiple_of(rc * row_chunk, row_chunk)
            acc = jnp.zeros((row_chunk, lane_chunk), F32) + bdw_ref[:, lanes]
            for k in range(n_taps):
                q, s = divmod(base + k, SUBLANES)
                rows = pl.ds(pl.multiple_of(r0 + q * SUBLANES, SUBLANES), row_chunk)
                acc = acc + wdw_ref[k:k + 1, lanes] * shift_ref[s, rows, :]
            z_ref[pl.ds(r0, row_chunk), lanes] = acc
            return 0

        lax.fori_loop(0, tm // row_chunk, rows_step, 0)

    z = _layer_norm(z_ref[...], ng_ref[...], nb_ref[...])
    y = _bdot(_silu(z).astype(BF16), w2_ref[...]) + b2_ref[...]
    v = alpha * x_ref[0] + g1_ref[0, 0] * y
    xo_ref[0] = _layer_norm(v, lng_ref[...], lnb_ref[...])


def _conformer_tail(z, w_dw, b_dw, n_g, n_b, w2_bf16, b2, x, mod, ln_g, ln_b, alpha):
    nb, s, d = z.shape
    tm = _tile(TM_CONV, s)
    n_t = s // tm
    hb = tm // BF16_ROWS
    n_hb = s // BF16_ROWS
    lane_chunk = min(256, d)
    n_taps = w_dw.shape[0]
    last_tap_row = BF16_ROWS + (n_taps - 1) // 2
    n_shift = tm + last_tap_row // SUBLANES * SUBLANES
    assert n_shift + SUBLANES - 1 <= tm + 2 * BF16_ROWS and (n_taps - 1) // 2 <= BF16_ROWS
    kern = functools.partial(_conformer_tail_kernel, n_t=n_t, tm=tm, alpha=alpha,
                             lane_chunk=lane_chunk, row_chunk=min(32, tm))
    vec = lambda a: a.reshape(1, d)
    vspec = pl.BlockSpec((1, d), lambda b, i: (0, 0))
    return pl.pallas_call(
        kern,
        grid=(nb, n_t),
        in_specs=[
            pl.BlockSpec((1, tm, d), lambda b, i: (b, i, 0)),
            pl.BlockSpec((1, BF16_ROWS, d), lambda b, i: (b, jnp.maximum(i * hb - 1, 0), 0)),
            pl.BlockSpec((1, BF16_ROWS, d), lambda b, i: (b, jnp.minimum((i + 1) * hb, n_hb - 1), 0)),
            pl.BlockSpec((w_dw.shape[0], d), lambda b, i: (0, 0)),
            vspec, vspec, vspec,
            pl.BlockSpec((d, d), lambda b, i: (0, 0)),
            vspec,
            pl.BlockSpec((1, tm, d), lambda b, i: (b, i, 0)),
            pl.BlockSpec((1, 1, 1, d), lambda b, i: (2, b, 0, 0)),
            vspec, vspec,
        ],
        out_specs=pl.BlockSpec((1, tm, d), lambda b, i: (b, i, 0)),
        out_shape=jax.ShapeDtypeStruct((nb, s, d), F32),
        scratch_shapes=[pltpu.VMEM((tm + 2 * BF16_ROWS, d), F32), pltpu.VMEM((tm, d), F32),
                        pltpu.VMEM((SUBLANES, n_shift, lane_chunk), F32)],
        compiler_params=_cparams(2),
        name="conformer_tail",
    )(z, z, z, w_dw, vec(b_dw), vec(n_g), vec(n_b), w2_bf16, vec(b2), x, mod, vec(ln_g), vec(ln_b))


def _router_kernel(x_ref, sh_ref, sc_ref, whi_ref, wlo_ref, bc_ref,
                   idx_ref, wt_ref, rank_ref, cnt_ref, carry_ref, *, n_exp, tm):
    i = pl.program_id(0)

    @pl.when(i == 0)
    def _():
        carry_ref[...] = jnp.zeros_like(carry_ref)

    tok = x_ref[...] * (1.0 + sc_ref[0, 0]) + sh_ref[0, 0]
    t_hi = tok.astype(BF16)
    t_lo = (tok - t_hi.astype(F32)).astype(BF16)
    nt = (((1,), (1,)), ((), ()))
    logits = (lax.dot_general(whi_ref[...], t_hi, nt, preferred_element_type=F32)
              + lax.dot_general(whi_ref[...], t_lo, nt, preferred_element_type=F32)
              + lax.dot_general(wlo_ref[...], t_hi, nt, preferred_element_type=F32))
    scores = _sigmoid(logits)
    biased = scores + bc_ref[...]

    per_group = n_exp // N_GROUPS
    sub = lax.broadcasted_iota(jnp.int32, (per_group, tm), 0)
    neg = -jnp.inf
    grp = [biased[g * per_group:(g + 1) * per_group, :] for g in range(N_GROUPS)]
    sc_g = [scores[g * per_group:(g + 1) * per_group, :] for g in range(N_GROUPS)]

    gscore = []
    for g in range(N_GROUPS):
        m1 = jnp.max(grp[g], axis=0, keepdims=True)
        first = jnp.min(jnp.where(grp[g] == m1, sub, per_group), axis=0, keepdims=True)
        m2 = jnp.max(jnp.where(sub == first, neg, grp[g]), axis=0, keepdims=True)
        gscore.append(m1 + m2)
    masked = []
    for g in range(N_GROUPS):
        beaten = jnp.zeros((1, tm), jnp.int32)
        for o in range(N_GROUPS):
            if o == g:
                continue
            wins = (gscore[o] > gscore[g]) | ((gscore[o] == gscore[g]) & (o < g))
            beaten = beaten + wins.astype(jnp.int32)
        masked.append(jnp.where(beaten < TOPK_GROUPS, grp[g], neg))

    eid = [sub + g * per_group for g in range(N_GROUPS)]
    member = [jnp.zeros((per_group, tm), F32) for _ in range(N_GROUPS)]
    idx_rows, score_rows = [], []
    for _ in range(TOP_K):
        m = masked[0]
        for g in range(1, N_GROUPS):
            m = jnp.maximum(m, masked[g])
        m = jnp.max(m, axis=0, keepdims=True)
        cand = jnp.where(masked[0] == m, eid[0], n_exp)
        for g in range(1, N_GROUPS):
            cand = jnp.minimum(cand, jnp.where(masked[g] == m, eid[g], n_exp))
        first = jnp.min(cand, axis=0, keepdims=True)
        picked = jnp.zeros((per_group, tm), F32)
        for g in range(N_GROUPS):
            sel = eid[g] == first
            picked = picked + jnp.where(sel, sc_g[g], 0.0)
            member[g] = jnp.where(sel, 1.0, member[g])
            masked[g] = jnp.where(sel, neg, masked[g])
        idx_rows.append(first)
        score_rows.append(jnp.sum(picked, axis=0, keepdims=True))

    total = score_rows[0]
    for k in range(1, TOP_K):
        total = total + score_rows[k]

    memb = jnp.concatenate(member, axis=0)
    earlier = jnp.where(lax.broadcasted_iota(jnp.int32, (tm, tm), 0)
                        < lax.broadcasted_iota(jnp.int32, (tm, tm), 1), 1.0, 0.0).astype(BF16)
    pos = _bdot(memb.astype(BF16), earlier) + carry_ref[...]
    pos_g = [pos[g * per_group:(g + 1) * per_group, :] for g in range(N_GROUPS)]
    for k in range(TOP_K):
        acc = jnp.zeros((per_group, tm), F32)
        for g in range(N_GROUPS):
            acc = acc + jnp.where(eid[g] == idx_rows[k], pos_g[g], 0.0)
        rank_ref[k:k + 1, :] = jnp.sum(acc, axis=0, keepdims=True).astype(jnp.int32)
        idx_ref[k:k + 1, :] = idx_rows[k]
        wt_ref[k:k + 1, :] = score_rows[k] / total * ROUTED_SCALE
    carry_ref[...] += jnp.sum(memb, axis=1, keepdims=True)
    cnt_ref[...] = carry_ref[...].astype(jnp.int32)


def _route(x, mod, w_router, b_corr):
    nb, s, d = x.shape
    t = nb * s
    n_exp = w_router.shape[1]
    tm = _tile(TM_ROUTE, s)
    per_b = s // tm
    w_t = w_router.T
    w_hi = w_t.astype(BF16)
    w_lo = (w_t - w_hi.astype(F32)).astype(BF16)
    kern = functools.partial(_router_kernel, n_exp=n_exp, tm=tm)
    out_spec = pl.BlockSpec((TOP_K, tm), lambda i: (0, i))
    return pl.pallas_call(
        kern,
        grid=(t // tm,),
        in_specs=[
            pl.BlockSpec((tm, d), lambda i: (i, 0)),
            pl.BlockSpec((1, 1, 1, d), lambda i: (3, i // per_b, 0, 0)),
            pl.BlockSpec((1, 1, 1, d), lambda i: (4, i // per_b, 0, 0)),
            pl.BlockSpec((n_exp, d), lambda i: (0, 0)),
            pl.BlockSpec((n_exp, d), lambda i: (0, 0)),
            pl.BlockSpec((n_exp, 1), lambda i: (0, 0)),
        ],
        out_specs=[out_spec, out_spec, out_spec, pl.BlockSpec((n_exp, 1), lambda i: (0, 0))],
        out_shape=[
            jax.ShapeDtypeStruct((TOP_K, t), jnp.int32),
            jax.ShapeDtypeStruct((TOP_K, t), F32),
            jax.ShapeDtypeStruct((TOP_K, t), jnp.int32),
            jax.ShapeDtypeStruct((n_exp, 1), jnp.int32),
        ],
        scratch_shapes=[pltpu.VMEM((n_exp, 1), F32)],
        compiler_params=_cparams(1),
        name="moe_router",
    )(x.reshape(t, d), mod, mod, w_hi, w_lo, b_corr.reshape(n_exp, 1).astype(F32))


def _slots_kernel(ps_ref, idx_ref, rank_ref, dest_ref, *, n_exp):
    idx = idx_ref[...]

    def add_start(e, acc):
        return acc + jnp.where(idx == e, ps_ref[e], 0)

    dest_ref[...] = lax.fori_loop(0, n_exp, add_start, rank_ref[...])


def _slots(idx, rank, pad_starts):
    k, t = idx.shape
    tm = _tile(2048, t)
    grid_spec = pltpu.PrefetchScalarGridSpec(
        num_scalar_prefetch=1,
        grid=(t // tm,),
        in_specs=[pl.BlockSpec((k, tm), lambda i, ps: (0, i)), pl.BlockSpec((k, tm), lambda i, ps: (0, i))],
        out_specs=pl.BlockSpec((k, tm), lambda i, ps: (0, i)),
    )
    return pl.pallas_call(
        functools.partial(_slots_kernel, n_exp=pad_starts.shape[0]),
        grid_spec=grid_spec,
        out_shape=jax.ShapeDtypeStruct((k, t), jnp.int32),
        compiler_params=_cparams(1),
        name="moe_slots",
    )(pad_starts, idx, rank)


def _to_slabs(value):
    n_seg = value.shape[1] // LANES
    by_seg = jnp.stack([value[:, s * LANES:(s + 1) * LANES] for s in range(n_seg)], axis=0)
    return jnp.swapaxes(by_seg, 0, 1)


def _from_slabs(slabs):
    by_seg = jnp.swapaxes(slabs, 0, 1)
    return jnp.concatenate([by_seg[s] for s in range(slabs.shape[1])], axis=1)


def _dispatch_kernel(pend_ref, padded_ref, x_ref, sh_ref, sc_ref, dest_ref, wsg_ref, wsu_ref, wsd_ref,
                     xs_ref, shared_ref, tok_ref, zero_ref, sem, zsem, *, n_exp, tm, bm):
    i = pl.program_id(0)

    def zero_copy(e):
        return pltpu.make_async_copy(zero_ref, xs_ref.at[pl.ds(pend_ref[e] - bm, bm)], zsem)

    @pl.when(i == 0)
    def _():
        zero_ref[...] = jnp.zeros_like(zero_ref)

        def start(e, _):
            @pl.when(padded_ref[e] > 0)
            def _():
                zero_copy(e).start()
            return 0

        def wait(e, _):
            @pl.when(padded_ref[e] > 0)
            def _():
                zero_copy(e).wait()
            return 0

        def tail_copy(blk):
            return pltpu.make_async_copy(zero_ref, xs_ref.at[pl.ds(blk * bm, bm)], zsem)

        def start_tail(blk, _):
            tail_copy(blk).start()
            return 0

        def wait_tail(blk, _):
            tail_copy(blk).wait()
            return 0

        first_unused = pend_ref[n_exp - 1] // bm
        n_blk = xs_ref.shape[0] // bm
        lax.fori_loop(0, n_exp, start, 0)
        lax.fori_loop(first_unused, n_blk, start_tail, 0)
        lax.fori_loop(0, n_exp, wait, 0)
        lax.fori_loop(first_unused, n_blk, wait_tail, 0)

    n_steps = pl.num_programs(0)
    slot = i % 2

    def drain(buf):
        for k in range(TOP_K):
            pltpu.make_async_copy(tok_ref.at[buf], xs_ref.at[pl.ds(0, tm)], sem.at[buf]).wait()

    @pl.when(i >= 2)
    def _():
        drain(slot)

    tok = x_ref[...] * (1.0 + sc_ref[0, 0]) + sh_ref[0, 0]
    tok_ref[slot] = _to_slabs(tok)
    for r in range(tm):
        for k in range(TOP_K):
            pltpu.make_async_copy(tok_ref.at[slot, r], xs_ref.at[dest_ref[k, r]],
                                  sem.at[slot]).start(priority=k % 2)
    tok = tok.astype(BF16)
    hid = (_silu(_bdot(tok, wsg_ref[...])) * _bdot(tok, wsu_ref[...])).astype(BF16)
    shared_ref[...] = _bdot(hid, wsd_ref[...])

    @pl.when(i == n_steps - 1)
    def _():
        @pl.when(i >= 1)
        def _():
            drain(1 - slot)
        drain(slot)


def _dispatch(x, mod, dest, pad_ends, padded, n_slots, bm, ws_gate, ws_up, ws_down):
    nb, s, d = x.shape
    f = ws_gate.shape[-1]
    t = nb * s
    tm = _tile(TM_DISPATCH, s)
    per_b = s // tm
    n_exp = pad_ends.shape[0]
    n_seg = d // LANES
    assert n_seg % SUBLANES == 0, "token slabs must be whole (8, 128) tiles"
    kern = functools.partial(_dispatch_kernel, n_exp=n_exp, tm=tm, bm=bm)
    grid_spec = pltpu.PrefetchScalarGridSpec(
        num_scalar_prefetch=2,
        grid=(t // tm,),
        in_specs=[
            pl.BlockSpec((tm, d), lambda i, pe, pd: (i, 0)),
            pl.BlockSpec((1, 1, 1, d), lambda i, pe, pd: (3, i // per_b, 0, 0)),
            pl.BlockSpec((1, 1, 1, d), lambda i, pe, pd: (4, i // per_b, 0, 0)),
            pl.BlockSpec((TOP_K, tm), lambda i, pe, pd: (0, i), memory_space=pltpu.SMEM),
            pl.BlockSpec((d, f), lambda i, pe, pd: (0, 0)),
            pl.BlockSpec((d, f), lambda i, pe, pd: (0, 0)),
            pl.BlockSpec((f, d), lambda i, pe, pd: (0, 0)),
        ],
        out_specs=[pl.BlockSpec(memory_space=pl.ANY), pl.BlockSpec((tm, d), lambda i, pe, pd: (i, 0))],
        scratch_shapes=[
            pltpu.VMEM((2, tm, n_seg, LANES), F32),
            pltpu.VMEM((bm, n_seg, LANES), F32),
            pltpu.SemaphoreType.DMA((2,)),
            pltpu.SemaphoreType.DMA(()),
        ],
    )
    return pl.pallas_call(
        kern,
        grid_spec=grid_spec,
        out_shape=[jax.ShapeDtypeStruct((n_slots, n_seg, LANES), F32), jax.ShapeDtypeStruct((t, d), F32)],
        compiler_params=_cparams(1),
        name="moe_dispatch",
    )(pad_ends, padded, x.reshape(t, d), mod, mod, dest, ws_gate, ws_up, ws_down)


def _experts_kernel(be_ref, nu_ref, xs_ref, wg_ref, wu_ref, wd_ref, ys_ref):
    b = pl.program_id(0)

    @pl.when(b < nu_ref[0])
    def _():
        x = _from_slabs(xs_ref[...]).astype(BF16)
        g = _bdot(x, wg_ref[0])
        u = _bdot(x, wu_ref[0])
        ys_ref[...] = _to_slabs(_bdot((_silu(g) * u).astype(BF16), wd_ref[0]))

    @pl.when(b >= nu_ref[0])
    def _():
        ys_ref[...] = jnp.zeros_like(ys_ref)


def _grouped_experts(xs, blk_e, n_used, w_gate, w_up, w_down, bm):
    d, f = w_gate.shape[1:]
    n_seg = d // LANES
    n_blk = xs.shape[0] // bm

    def blk(b, be, nu):
        return jnp.minimum(b, nu[0] - 1)

    grid_spec = pltpu.PrefetchScalarGridSpec(
        num_scalar_prefetch=2,
        grid=(n_blk,),
        in_specs=[
            pl.BlockSpec((bm, n_seg, LANES), lambda b, be, nu: (blk(b, be, nu), 0, 0)),
            pl.BlockSpec((1, d, f), lambda b, be, nu: (be[blk(b, be, nu)], 0, 0)),
            pl.BlockSpec((1, d, f), lambda b, be, nu: (be[blk(b, be, nu)], 0, 0)),
            pl.BlockSpec((1, f, d), lambda b, be, nu: (be[blk(b, be, nu)], 0, 0)),
        ],
        out_specs=pl.BlockSpec((bm, n_seg, LANES), lambda b, be, nu: (b, 0, 0)),
    )
    return pl.pallas_call(
        _experts_kernel,
        grid_spec=grid_spec,
        out_shape=jax.ShapeDtypeStruct(xs.shape, F32),
        compiler_params=_cparams(1),
        name="moe_experts",
    )(blk_e, n_used, xs, w_gate, w_up, w_down)


def _combine_kernel(x_ref, gt_ref, dest_ref, dest_next_ref, wt_ref, ys_ref, shared_ref,
                    lng_ref, lnb_ref, xo_ref, rows_ref, sem, *, tm, alpha):
    i = pl.program_id(0)
    n_steps = pl.num_programs(0)
    slot = i % 2

    def gather(d_ref, buf):
        def issue(r, _):
            for k in range(TOP_K):
                pltpu.make_async_copy(ys_ref.at[d_ref[k, r]], rows_ref.at[buf, k, r],
                                      sem.at[buf]).start(priority=k % 2)
            return 0
        lax.fori_loop(0, tm, issue, 0)

    @pl.when(i == 0)
    def _():
        gather(dest_ref, 0)

    for r in range(tm):
        for k in range(TOP_K):
            pltpu.make_async_copy(ys_ref.at[dest_next_ref[k, r]], rows_ref.at[1 - slot, k, r],
                                  sem.at[1 - slot]).start(priority=k % 2)

    x = x_ref[...]
    f = shared_ref[...]

    for k in range(TOP_K):
        pltpu.make_async_copy(ys_ref.at[pl.ds(0, tm)], rows_ref.at[slot, k], sem.at[slot]).wait()
    n_seg = rows_ref.shape[3]
    wt = wt_ref[...]
    routed = jnp.broadcast_to(wt[:, 0:1, :], (tm, n_seg, LANES)) * rows_ref[slot, 0]
    for k in range(1, TOP_K):
        routed = routed + jnp.broadcast_to(wt[:, k:k + 1, :], (tm, n_seg, LANES)) * rows_ref[slot, k]
    f = f + _from_slabs(routed)
    v = alpha * x + gt_ref[0, 0] * f
    xo_ref[...] = _layer_norm(v, lng_ref[...], lnb_ref[...])

    @pl.when(i == n_steps - 1)
    def _():
        for k in range(TOP_K):
            pltpu.make_async_copy(ys_ref.at[pl.ds(0, tm)], rows_ref.at[1 - slot, k], sem.at[1 - slot]).wait()


def _combine(x, mod, dest, wt_splat, ys, shared, ln_g, ln_b, alpha):
    nb, s, d = x.shape
    t = nb * s
    tm = _tile(TM_COMBINE, s)
    per_b = s // tm
    n_seg = d // LANES
    kern = functools.partial(_combine_kernel, tm=tm, alpha=alpha)
    vspec = pl.BlockSpec((1, d), lambda i: (0, 0))
    n_steps = t // tm
    out = pl.pallas_call(
        kern,
        grid=(n_steps,),
        in_specs=[
            pl.BlockSpec((tm, d), lambda i: (i, 0)),
            pl.BlockSpec((1, 1, 1, d), lambda i: (5, i // per_b, 0, 0)),
            pl.BlockSpec((TOP_K, tm), lambda i: (0, i), memory_space=pltpu.SMEM),
            pl.BlockSpec((TOP_K, tm), lambda i: (0, jnp.minimum(i + 1, n_steps - 1)), memory_space=pltpu.SMEM),
            pl.BlockSpec((tm, TOP_K, LANES), lambda i: (i, 0, 0)),
            pl.BlockSpec(memory_space=pl.ANY),
            pl.BlockSpec((tm, d), lambda i: (i, 0)),
            vspec, vspec,
        ],
        out_specs=pl.BlockSpec((tm, d), lambda i: (i, 0)),
        out_shape=jax.ShapeDtypeStruct((t, d), F32),
        scratch_shapes=[pltpu.VMEM((2, TOP_K, tm, n_seg, LANES), F32), pltpu.SemaphoreType.DMA((2,))],
        compiler_params=_cparams(1),
        name="moe_combine",
    )(x.reshape(t, d), mod, dest, dest, wt_splat, ys, shared, ln_g.reshape(1, d), ln_b.reshape(1, d))
    return out.reshape(nb, s, d)


def _moe_block(x, mod, w_router, b_router, w_gate, w_up, w_down, ws_gate, ws_up, ws_down, ln_g, ln_b, alpha):
    nb, s, d = x.shape
    t = nb * s
    n_exp = w_router.shape[1]
    bm = _tile(MOE_BLOCK, t * TOP_K)
    idx, wt, rank, counts = _route(x, mod, w_router, b_router)

    counts = counts.reshape(n_exp)
    padded = (counts + bm - 1) // bm * bm
    pad_ends = jnp.cumsum(padded).astype(jnp.int32)
    pad_starts = pad_ends - padded
    dest = _slots(idx, rank, pad_starts.astype(jnp.int32))
    n_blk = t * TOP_K // bm + n_exp
    n_used = (pad_ends[-1:] // bm).astype(jnp.int32)
    blk_start = jnp.arange(n_blk, dtype=jnp.int32) * bm
    blk_e = jnp.sum((pad_ends[None, :] <= blk_start[:, None]).astype(jnp.int32), axis=1)
    blk_e = jnp.minimum(blk_e, n_exp - 1).astype(jnp.int32)

    xs, shared = _dispatch(x, mod, dest, pad_ends, padded.astype(jnp.int32), n_blk * bm, bm,
                           ws_gate.astype(BF16), ws_up.astype(BF16), ws_down.astype(BF16))
    ys = _grouped_experts(xs, blk_e, n_used, w_gate, w_up, w_down, bm)
    wt_splat = jnp.broadcast_to(wt.T[:, :, None], (t, TOP_K, LANES))
    return _combine(x, mod, dest, wt_splat, ys, shared, ln_g, ln_b, alpha)


def kernel(x, c, ctx, c_ctx, w_mod, b_mod, ln1_g, ln1_b, ln2_g, ln2_b, rg_w_in, rg_b_in, rg_conv_w, rg_conv_b, rg_w_gates, rg_b_gates, rg_lambda, rg_w_out, rg_b_out, cf_w_pw1, cf_b_pw1, cf_w_dw, cf_b_dw, cf_norm_g, cf_norm_b, cf_w_pw2, cf_b_pw2, moe_w_router, moe_b_router, moe_w_gate, moe_w_up, moe_w_down, sh_w_gate, sh_w_up, sh_w_down):
    nb, s, d = x.shape
    depth = w_mod.shape[0]
    assert depth == 2 and nb + 1 <= SUBLANES
    alpha = (2 * depth) ** 0.25

    c_rows = jnp.zeros((SUBLANES, d), F32).at[:nb].set(c).at[nb].set(c_ctx)
    mod_all = _adaln_vectors(c_rows, w_mod, b_mod)
    mod_all = mod_all.reshape(depth, SUBLANES, 6, 1, d).transpose(0, 2, 1, 3, 4)
    mod_lat = [mod_all[l, :, :nb] for l in range(depth)]
    mod_ctx = jnp.broadcast_to(mod_all[0, :, nb:nb + 1], (6, nb, 1, d))

    n_h, hd = rg_w_gates.shape[3], rg_w_gates.shape[4]
    r = n_h * hd
    w_in = rg_w_in[0].astype(BF16)
    wg = [jnp.concatenate([rg_w_gates[0, dr, 0], rg_w_gates[0, dr, 1]], axis=-1).astype(BF16) for dr in range(2)]
    bg = [jnp.concatenate([rg_b_gates[0, dr, 0], rg_b_gates[0, dr, 1]], axis=-1).reshape(n_h, 1, 2 * hd)
          for dr in range(2)]
    conv_w, conv_b = rg_conv_w[0], rg_conv_b[0]
    zero_state = jnp.zeros((nb, n_h, 1, hd), F32)

    proj_c, _ = _modulated_projection(ctx, mod_ctx, w_in, rg_b_in[0], BF16)
    _, hcf = _rglru_scan(proj_c, conv_w, conv_b, wg[0], bg[0], rg_lambda[0, 0], zero_state, reverse=False)
    _, hcb = _rglru_scan(proj_c, conv_w, conv_b, wg[1], bg[1], rg_lambda[0, 1], zero_state, reverse=True)

    expert_stacks = (moe_w_gate, moe_w_up, moe_w_down)
    proj_x, experts0 = _modulated_projection(x, mod_lat[0], w_in, rg_b_in[0], BF16, expert_stacks, 0)
    hxf, _ = _rglru_scan(proj_x, conv_w, conv_b, wg[0], bg[0], rg_lambda[0, 0], hcf, reverse=False)
    readout = (hxf, rg_w_out[0].astype(BF16), rg_b_out[0], x, mod_lat[0], ln1_g[0], ln1_b[0])
    x, _ = _rglru_scan(proj_x, conv_w, conv_b, wg[1], bg[1], rg_lambda[0, 1], hcb, reverse=True,
                       readout=readout, alpha=alpha)
    x = _moe_block(x, mod_lat[0], moe_w_router[0], moe_b_router[0],
                   *experts0,
                   sh_w_gate[0], sh_w_up[0], sh_w_down[0], ln2_g[0], ln2_b[0], alpha)

    z, experts1 = _modulated_glu(x, mod_lat[1], cf_w_pw1[0].astype(BF16), cf_b_pw1[0], expert_stacks, 1)
    x = _conformer_tail(z, cf_w_dw[0], cf_b_dw[0], cf_norm_g[0], cf_norm_b[0], cf_w_pw2[0].astype(BF16),
                        cf_b_pw2[0], x, mod_lat[1], ln1_g[1], ln1_b[1], alpha)
    x = _moe_block(x, mod_lat[1], moe_w_router[1], moe_b_router[1],
                   *experts1,
                   sh_w_gate[1], sh_w_up[1], sh_w_down[1], ln2_g[1], ln2_b[1], alpha)
    return x
```

```python
import functools

import jax
import jax.numpy as jnp
from jax import lax
from jax.experimental import pallas as pl
from jax.experimental.pallas import tpu as pltpu

F32 = jnp.float32
BF16 = jnp.bfloat16

RG_C = 8.0
N_GROUPS = 8
TOPK_GROUPS = 4
TOP_K = 8
ROUTED_SCALE = 2.5
LN_EPS = 1e-5

VMEM_LIMIT_BYTES = 56 * 1024 * 1024
SUBLANES = 8
LANES = 128
BF16_ROWS = 16

TM_PROJ = 512
TM_SCAN = 512
HEADS_PER_STEP = 5
TM_CONV = 256
TM_ROUTE = 512
TM_DISPATCH = 256
TM_COMBINE = 128
MOE_BLOCK = 512


def _tile(default, n):
    t = min(default, n)
    assert n % t == 0, (default, n)
    return t


def _cparams(n_axes):
    return pltpu.CompilerParams(
        dimension_semantics=("arbitrary",) * n_axes,
        vmem_limit_bytes=VMEM_LIMIT_BYTES,
    )


def _sigmoid(v):
    return 0.5 * jnp.tanh(0.5 * v) + 0.5


def _sqrt_nonneg(v):
    return jnp.where(v > 0.0, v * lax.rsqrt(v), 0.0)


def _silu(v):
    return v * _sigmoid(v)


def _gelu_tanh(v):
    c = 0.7978845608028654
    return 0.5 * v * (1.0 + jnp.tanh(c * (v + 0.044715 * (v * v * v))))


def _softplus(v):
    return jnp.maximum(v, 0.0) + jnp.log(1.0 + jnp.exp(-jnp.abs(v)))


def _layer_norm(v, g, b):
    mu = jnp.mean(v, axis=-1, keepdims=True)
    d = v - mu
    var = jnp.mean(d * d, axis=-1, keepdims=True)
    return d * lax.rsqrt(var + LN_EPS) * g + b


def _bdot(a, b):
    return jnp.dot(a, b, preferred_element_type=F32)


def _mod_kernel(c_ref, w_ref, b_ref, o_ref):
    s = _silu(c_ref[...])
    o_ref[0] = _bdot(s.astype(BF16), w_ref[0].astype(BF16)) + b_ref[0]


def _adaln_vectors(c_rows, w_mod, b_mod):
    depth, d, d6 = w_mod.shape
    tn = _tile(1024, d6)
    return pl.pallas_call(
        _mod_kernel,
        grid=(depth, d6 // tn),
        in_specs=[
            pl.BlockSpec((SUBLANES, d), lambda l, n: (0, 0)),
            pl.BlockSpec((1, d, tn), lambda l, n: (l, 0, n)),
            pl.BlockSpec((1, 1, tn), lambda l, n: (l, 0, n)),
        ],
        out_specs=pl.BlockSpec((1, SUBLANES, tn), lambda l, n: (l, 0, n)),
        out_shape=jax.ShapeDtypeStruct((depth, SUBLANES, d6), F32),
        compiler_params=_cparams(2),
        name="adaln_vectors",
    )(c_rows, w_mod, b_mod.reshape(depth, 1, d6))


def _mod_spec(chunk, batch_of):
    def index(*ids):
        return (chunk, batch_of(*ids), 0, 0)
    return index


class _SideCast:
    def __init__(self, stacks, layer, grid):
        self.n = len(stacks)
        self.shapes = [w.shape[1:] for w in stacks]
        n_steps = 1
        for g in grid:
            n_steps *= g

        def step(*ids):
            lin = ids[0]
            for g, v in zip(grid[1:], ids[1:]):
                lin = lin * g + v
            return lin

        self.in_specs, self.out_specs, self.out_shape, self.args = [], [], [], []
        for w in stacks:
            n_l, n_e, a, b = w.shape
            rows = n_e * a
            blk = rows // n_steps
            assert rows % n_steps == 0 and blk % BF16_ROWS == 0, (w.shape, n_steps)
            self.in_specs.append(pl.BlockSpec((1, blk, b), lambda *ids: (layer, step(*ids), 0)))
            self.out_specs.append(pl.BlockSpec((blk, b), lambda *ids: (step(*ids), 0)))
            self.out_shape.append(jax.ShapeDtypeStruct((rows, b), BF16))
            self.args.append(w.reshape(n_l, rows, b))

    @staticmethod
    def run(src_refs, dst_refs):
        for src, dst in zip(src_refs, dst_refs):
            dst[...] = src[0].astype(dst.dtype)

    def unpack(self, outs):
        return [o.reshape(shape) for o, shape in zip(outs, self.shapes)]


def _proj_kernel(x_ref, sh_ref, sc_ref, w_ref, b_ref, *rest, n_side):
    o_ref = rest[n_side]
    h = x_ref[0] * (1.0 + sc_ref[0, 0]) + sh_ref[0, 0]
    o_ref[0] = (_bdot(h.astype(BF16), w_ref[...]) + b_ref[...]).astype(o_ref.dtype)
    _SideCast.run(rest[:n_side], rest[n_side + 1:])


def _modulated_projection(x, mod, w_bf16, bias, out_dtype, side_stacks=(), side_layer=0):
    nb, s, d = x.shape
    n = w_bf16.shape[1]
    tm = _tile(TM_PROJ, s)
    tn = _tile(2560, n)
    dm = mod.shape[-1]
    grid = (n // tn, nb, s // tm)
    side = _SideCast(side_stacks, side_layer, grid)
    outs = pl.pallas_call(
        functools.partial(_proj_kernel, n_side=side.n),
        grid=grid,
        in_specs=[
            pl.BlockSpec((1, tm, d), lambda j, b, i: (b, i, 0)),
            pl.BlockSpec((1, 1, 1, dm), lambda j, b, i: (0, b, 0, 0)),
            pl.BlockSpec((1, 1, 1, dm), lambda j, b, i: (1, b, 0, 0)),
            pl.BlockSpec((d, tn), lambda j, b, i: (0, j)),
            pl.BlockSpec((1, tn), lambda j, b, i: (0, j)),
        ] + side.in_specs,
        out_specs=[pl.BlockSpec((1, tm, tn), lambda j, b, i: (b, i, j))] + side.out_specs,
        out_shape=[jax.ShapeDtypeStruct((nb, s, n), out_dtype)] + side.out_shape,
        compiler_params=_cparams(3),
        name="modulated_projection",
    )(x, mod, mod, w_bf16, bias.reshape(1, n), *side.args)
    return outs[0], side.unpack(outs[1:])


def _rglru_kernel(*refs, reverse, readout, n_t, tm, alpha):
    if readout:
        (cur_ref, prev_ref, next_ref, perm_ref, cw_ref, cb_ref, wg_ref, bg_ref, lam_ref, h0_ref,
         hf_ref, gate_ref, wout_ref, bout_ref, x_ref, g1_ref, lng_ref, lnb_ref,
         xo_ref, hlast_ref,
         win_ref, a_ref, hl_ref, carry_ref, acc_ref) = refs
    else:
        (cur_ref, prev_ref, next_ref, perm_ref, cw_ref, cb_ref, wg_ref, bg_ref, lam_ref, h0_ref,
         ho_ref, hlast_ref,
         win_ref, a_ref, hl_ref, carry_ref) = refs
    i = pl.program_id(1)
    h = pl.program_id(2)
    n_h = pl.num_programs(2)
    ti = (n_t - 1 - i) if reverse else i
    hd = cur_ref.shape[-1]
    head_dim = wg_ref.shape[1]
    group = hd // head_dim
    chunk = tm // SUBLANES
    sub = lax.broadcasted_iota(jnp.int32, (SUBLANES, hd), 0)

    rnn = _bdot(perm_ref[...], cur_ref[0])
    prev_last = jnp.where(ti == 0, 0.0, prev_ref[0].astype(F32)[BF16_ROWS - 1:, :])
    nxt = jnp.where(ti == n_t - 1, 0.0, next_ref[0].astype(F32)[:2, :])
    win_ref[0:SUBLANES, :] = jnp.where(
        sub == 0, prev_last, pltpu.roll(rnn[tm - SUBLANES:, :], 1, axis=0))
    win_ref[SUBLANES:SUBLANES + tm, :] = rnn
    for extra in range(2):
        lo = SUBLANES + tm + extra * SUBLANES
        win_ref[lo:lo + SUBLANES, :] = jnp.where(
            sub == SUBLANES - 1, nxt[extra:extra + 1, :],
            pltpu.roll(rnn[extra * SUBLANES:(extra + 1) * SUBLANES, :], SUBLANES - 1, axis=0))
    n_taps = cw_ref.shape[0]
    xb = cb_ref[...] + cw_ref[0:1, :] * win_ref[pl.ds(0, tm), :]
    for k in range(1, n_taps):
        xb = xb + cw_ref[k:k + 1, :] * win_ref[pl.ds(k * SUBLANES, tm), :]

    xb16 = xb.astype(BF16)
    pre = [_bdot(xb16[:, q * head_dim:(q + 1) * head_dim], wg_ref[q]) + bg_ref[q] for q in range(group)]
    r = _sigmoid(jnp.concatenate([p[:, :head_dim] for p in pre], axis=1))
    gi = _sigmoid(jnp.concatenate([p[:, head_dim:] for p in pre], axis=1))
    log_a = (-RG_C) * r * _softplus(-lam_ref[...])
    a = jnp.exp(log_a)
    a_ref[...] = a
    hl_ref[...] = _sqrt_nonneg(1.0 - a * a) * gi * xb

    if readout:
        gate_act = _gelu_tanh(_bdot(perm_ref[...], gate_ref[0]))

    def scan_step(jj, carry):
        a_cum, h_loc = carry
        j = (chunk - 1 - jj) if reverse else jj
        rows = pl.ds(pl.multiple_of(j * SUBLANES, SUBLANES), SUBLANES)
        aj = a_ref[rows, :]
        h_loc = aj * h_loc + hl_ref[rows, :]
        a_cum = a_cum * aj
        hl_ref[rows, :] = h_loc
        a_ref[rows, :] = a_cum
        return a_cum, h_loc

    a_end, h_end = lax.fori_loop(
        0, chunk, scan_step, (jnp.ones((SUBLANES, hd), F32), jnp.zeros((SUBLANES, hd), F32)),
        unroll=SUBLANES)

    @pl.when(i == 0)
    def _():
        for q in range(group):
            carry_ref[h * group + q] = h0_ref[0, h * group + q]

    state = jnp.concatenate([carry_ref[h * group + q] for q in range(group)], axis=1)
    entry = jnp.zeros((SUBLANES, hd), F32)
    for s in (range(SUBLANES - 1, -1, -1) if reverse else range(SUBLANES)):
        entry = jnp.where(sub == s, state, entry)
        state = a_end[s:s + 1, :] * state + h_end[s:s + 1, :]
    for q in range(group):
        carry_ref[h * group + q] = state[:, q * head_dim:(q + 1) * head_dim]
        hlast_ref[0, h * group + q] = state[:, q * head_dim:(q + 1) * head_dim]

    h_full = hl_ref[...] + a_ref[...] * jnp.concatenate([entry] * chunk, axis=0)
    if not readout:
        ho_ref[0] = h_full.astype(ho_ref.dtype)
    else:
        h_sum = h_full + hf_ref[0].astype(F32)
        contrib = _bdot((h_sum * gate_act).astype(BF16), wout_ref[...])
        n_lc = acc_ref.shape[0]

        @pl.when(h == 0)
        def _():
            for c in range(n_lc):
                acc_ref[c] = contrib[:, c * LANES:(c + 1) * LANES]

        @pl.when(h != 0)
        def _():
            for c in range(n_lc):
                acc_ref[c] += contrib[:, c * LANES:(c + 1) * LANES]

        @pl.when(h == n_h - 1)
        def _():
            y = jnp.concatenate(
                [jnp.concatenate(
                    [acc_ref[c, pl.ds(q * SUBLANES * SUBLANES + s, SUBLANES, stride=SUBLANES), :]
                     for s in range(SUBLANES) for q in range(chunk // SUBLANES)], axis=0)
                 for c in range(n_lc)], axis=1) + bout_ref[...]
            v = alpha * x_ref[0] + g1_ref[0, 0] * y
            xo_ref[0] = _layer_norm(v, lng_ref[...], lnb_ref[...])


def _rglru_scan(proj, conv_w, conv_b, wg, bg, lam, h0, *, reverse, h_dtype=BF16, readout=None, alpha=None):
    nb, s, r2 = proj.shape
    r = r2 // 2
    n_h, hd = wg.shape[0], wg.shape[1]
    tm = _tile(TM_SCAN, s)
    n_t = s // tm
    hb = tm // BF16_ROWS
    n_hb = s // BF16_ROWS

    def tix(i):
        return (n_t - 1 - i) if reverse else i

    rows = jnp.arange(tm, dtype=jnp.int32)
    time_of_row = (rows % SUBLANES) * (tm // SUBLANES) + rows // SUBLANES
    perm = (time_of_row[:, None] == rows[None, :]).astype(BF16)
    perm_spec = pl.BlockSpec((tm, tm), lambda b, i, h: (0, 0))

    group = HEADS_PER_STEP if n_h % HEADS_PER_STEP == 0 else 1
    n_g, gw = n_h // group, group * hd
    in_specs = [
        pl.BlockSpec((1, tm, gw), lambda b, i, h: (b, tix(i), n_g + h)),
        pl.BlockSpec((1, BF16_ROWS, gw), lambda b, i, h: (b, jnp.maximum(tix(i) * hb - 1, 0), n_g + h)),
        pl.BlockSpec((1, BF16_ROWS, gw), lambda b, i, h: (b, jnp.minimum((tix(i) + 1) * hb, n_hb - 1), n_g + h)),
        perm_spec,
        pl.BlockSpec((conv_w.shape[0], gw), lambda b, i, h: (0, h)),
        pl.BlockSpec((1, gw), lambda b, i, h: (0, h)),
        pl.BlockSpec((group, hd, 2 * hd), lambda b, i, h: (h, 0, 0)),
        pl.BlockSpec((group, 1, 2 * hd), lambda b, i, h: (h, 0, 0)),
        pl.BlockSpec((1, gw), lambda b, i, h: (0, h)),
        pl.BlockSpec((1, n_h, 1, hd), lambda b, i, h: (b, 0, 0, 0)),
    ]
    args = [proj, proj, proj, perm, conv_w, conv_b.reshape(1, r), wg, bg, lam.reshape(1, r), h0]
    assert conv_w.shape[0] == 4 and tm // SUBLANES >= 2, "window below covers taps at time offsets -1 .. +2"
    scratch = [
        pltpu.VMEM((tm + 3 * SUBLANES, gw), F32),
        pltpu.VMEM((tm, gw), F32),
        pltpu.VMEM((tm, gw), F32),
        pltpu.VMEM((n_h, 1, hd), F32),
    ]
    hlast_spec = pl.BlockSpec((1, n_h, 1, hd), lambda b, i, h: (b, 0, 0, 0))
    hlast_shape = jax.ShapeDtypeStruct((nb, n_h, 1, hd), F32)
    if readout is None:
        out_specs = [pl.BlockSpec((1, tm, gw), lambda b, i, h: (b, tix(i), h)), hlast_spec]
        out_shape = [jax.ShapeDtypeStruct((nb, s, r), h_dtype), hlast_shape]
    else:
        h_other, w_out, b_out, x, mod, ln_g, ln_b = readout
        d = x.shape[-1]
        in_specs += [
            pl.BlockSpec((1, tm, gw), lambda b, i, h: (b, tix(i), h)),
            pl.BlockSpec((1, tm, gw), lambda b, i, h: (b, tix(i), h)),
            pl.BlockSpec((gw, d), lambda b, i, h: (h, 0)),
            pl.BlockSpec((1, d), lambda b, i, h: (0, 0)),
            pl.BlockSpec((1, tm, d), lambda b, i, h: (b, tix(i), 0)),
            pl.BlockSpec((1, 1, 1, d), lambda b, i, h: (2, b, 0, 0)),
            pl.BlockSpec((1, d), lambda b, i, h: (0, 0)),
            pl.BlockSpec((1, d), lambda b, i, h: (0, 0)),
        ]
        args += [h_other, proj, w_out, b_out.reshape(1, d), x, mod, ln_g.reshape(1, d), ln_b.reshape(1, d)]
        assert (tm // SUBLANES) % SUBLANES == 0, "readout un-interleaves 8 vreg rows at a time"
        scratch += [pltpu.VMEM((d // LANES, tm, LANES), F32)]
        out_specs = [pl.BlockSpec((1, tm, d), lambda b, i, h: (b, tix(i), 0)), hlast_spec]
        out_shape = [jax.ShapeDtypeStruct((nb, s, d), F32), hlast_shape]
    kern = functools.partial(_rglru_kernel, reverse=reverse, readout=readout is not None,
                             n_t=n_t, tm=tm, alpha=alpha)
    return pl.pallas_call(
        kern,
        grid=(nb, n_t, n_g),
        in_specs=in_specs,
        out_specs=out_specs,
        out_shape=out_shape,
        scratch_shapes=scratch,
        compiler_params=_cparams(3),
        name="rglru_bwd_readout" if readout is not None else ("rglru_bwd" if reverse else "rglru_fwd"),
    )(*args)


def _glu_kernel(x_ref, sh_ref, sc_ref, wv_ref, wg_ref, bv_ref, bgate_ref, *rest, n_side):
    o_ref = rest[n_side]
    h = (x_ref[0] * (1.0 + sc_ref[0, 0]) + sh_ref[0, 0]).astype(BF16)
    val = _bdot(h, wv_ref[...]) + bv_ref[...]
    gate = _bdot(h, wg_ref[...]) + bgate_ref[...]
    o_ref[0] = (val * _sigmoid(gate)).astype(o_ref.dtype)
    _SideCast.run(rest[:n_side], rest[n_side + 1:])


def _modulated_glu(x, mod, w_bf16, bias, side_stacks=(), side_layer=0):
    nb, s, d = x.shape
    n = w_bf16.shape[1] // 2
    tm = _tile(TM_PROJ, s)
    tn = _tile(1024, n)
    nj = n // tn
    b2 = bias.reshape(1, 2 * n)
    grid = (nj, nb, s // tm)
    side = _SideCast(side_stacks, side_layer, grid)
    outs = pl.pallas_call(
        functools.partial(_glu_kernel, n_side=side.n),
        grid=grid,
        in_specs=[
            pl.BlockSpec((1, tm, d), lambda j, b, i: (b, i, 0)),
            pl.BlockSpec((1, 1, 1, d), lambda j, b, i: (0, b, 0, 0)),
            pl.BlockSpec((1, 1, 1, d), lambda j, b, i: (1, b, 0, 0)),
            pl.BlockSpec((d, tn), lambda j, b, i: (0, j)),
            pl.BlockSpec((d, tn), lambda j, b, i: (0, nj + j)),
            pl.BlockSpec((1, tn), lambda j, b, i: (0, j)),
            pl.BlockSpec((1, tn), lambda j, b, i: (0, nj + j)),
        ] + side.in_specs,
        out_specs=[pl.BlockSpec((1, tm, tn), lambda j, b, i: (b, i, j))] + side.out_specs,
        out_shape=[jax.ShapeDtypeStruct((nb, s, n), BF16)] + side.out_shape,
        compiler_params=_cparams(3),
        name="modulated_glu",
    )(x, mod, mod, w_bf16, w_bf16, b2, b2, *side.args)
    return outs[0], side.unpack(outs[1:])


def _conformer_tail_kernel(cur_ref, prev_ref, next_ref, wdw_ref, bdw_ref, ng_ref, nb_ref, w2_ref, b2_ref,
                           x_ref, g1_ref, lng_ref, lnb_ref, xo_ref, win_ref, z_ref, shift_ref, *, n_t, tm, alpha,
                           lane_chunk, row_chunk):
    i = pl.program_id(1)
    d = cur_ref.shape[-1]
    halo = BF16_ROWS
    win_ref[0:halo, :] = jnp.where(i == 0, 0.0, prev_ref[0].astype(F32))
    win_ref[halo:halo + tm, :] = cur_ref[0].astype(F32)
    win_ref[halo + tm:, :] = jnp.where(i == n_t - 1, 0.0, next_ref[0].astype(F32))
    n_taps = wdw_ref.shape[0]
    base = halo - (n_taps - 1) // 2

    n_shift = shift_ref.shape[1]
    for c in range(d // lane_chunk):
        lanes = slice(c * lane_chunk, (c + 1) * lane_chunk)
        for s in range(SUBLANES):
            shift_ref[s] = win_ref[pl.ds(s, n_shift), lanes]

        def rows_step(rc, _, lanes=lanes):
            r0 = pl.multiple_of(rc * row_chunk, row_chunk)
            acc = jnp.zeros((row_chunk, lane_chunk), F32) + bdw_ref[:, lanes]
            for k in range(n_taps):
                q, s = divmod(base + k, SUBLANES)
                rows = pl.ds(pl.multiple_of(r0 + q * SUBLANES, SUBLANES), row_chunk)
                acc = acc + wdw_ref[k:k + 1, lanes] * shift_ref[s, rows, :]
            z_ref[pl.ds(r0, row_chunk), lanes] = acc
            return 0

        lax.fori_loop(0, tm // row_chunk, rows_step, 0)

    z = _layer_norm(z_ref[...], ng_ref[...], nb_ref[...])
    y = _bdot(_silu(z).astype(BF16), w2_ref[...]) + b2_ref[...]
    v = alpha * x_ref[0] + g1_ref[0, 0] * y
    xo_ref[0] = _layer_norm(v, lng_ref[...], lnb_ref[...])


def _conformer_tail(z, w_dw, b_dw, n_g, n_b, w2_bf16, b2, x, mod, ln_g, ln_b, alpha):
    nb, s, d = z.shape
    tm = _tile(TM_CONV, s)
    n_t = s // tm
    hb = tm // BF16_ROWS
    n_hb = s // BF16_ROWS
    lane_chunk = min(256, d)
    n_taps = w_dw.shape[0]
    last_tap_row = BF16_ROWS + (n_taps - 1) // 2
    n_shift = tm + last_tap_row // SUBLANES * SUBLANES
    assert n_shift + SUBLANES - 1 <= tm + 2 * BF16_ROWS and (n_taps - 1) // 2 <= BF16_ROWS
    kern = functools.partial(_conformer_tail_kernel, n_t=n_t, tm=tm, alpha=alpha,
                             lane_chunk=lane_chunk, row_chunk=min(32, tm))
    vec = lambda a: a.reshape(1, d)
    vspec = pl.BlockSpec((1, d), lambda b, i: (0, 0))
    return pl.pallas_call(
        kern,
        grid=(nb, n_t),
        in_specs=[
            pl.BlockSpec((1, tm, d), lambda b, i: (b, i, 0)),
            pl.BlockSpec((1, BF16_ROWS, d), lambda b, i: (b, jnp.maximum(i * hb - 1, 0), 0)),
            pl.BlockSpec((1, BF16_ROWS, d), lambda b, i: (b, jnp.minimum((i + 1) * hb, n_hb - 1), 0)),
            pl.BlockSpec((w_dw.shape[0], d), lambda b, i: (0, 0)),
            vspec, vspec, vspec,
            pl.BlockSpec((d, d), lambda b, i: (0, 0)),
            vspec,
            pl.BlockSpec((1, tm, d), lambda b, i: (b, i, 0)),
            pl.BlockSpec((1, 1, 1, d), lambda b, i: (2, b, 0, 0)),
            vspec, vspec,
        ],
        out_specs=pl.BlockSpec((1, tm, d), lambda b, i: (b, i, 0)),
        out_shape=jax.ShapeDtypeStruct((nb, s, d), F32),
        scratch_shapes=[pltpu.VMEM((tm + 2 * BF16_ROWS, d), F32), pltpu.VMEM((tm, d), F32),
                        pltpu.VMEM((SUBLANES, n_shift, lane_chunk), F32)],
        compiler_params=_cparams(2),
        name="conformer_tail",
    )(z, z, z, w_dw, vec(b_dw), vec(n_g), vec(n_b), w2_bf16, vec(b2), x, mod, vec(ln_g), vec(ln_b))


def _router_kernel(x_ref, sh_ref, sc_ref, whi_ref, wlo_ref, bc_ref,
                   idx_ref, wt_ref, rank_ref, cnt_ref, carry_ref, *, n_exp, tm):
    i = pl.program_id(0)

    @pl.when(i == 0)
    def _():
        carry_ref[...] = jnp.zeros_like(carry_ref)

    tok = x_ref[...] * (1.0 + sc_ref[0, 0]) + sh_ref[0, 0]
    t_hi = tok.astype(BF16)
    t_lo = (tok - t_hi.astype(F32)).astype(BF16)
    nt = (((1,), (1,)), ((), ()))
    logits = (lax.dot_general(whi_ref[...], t_hi, nt, preferred_element_type=F32)
              + lax.dot_general(whi_ref[...], t_lo, nt, preferred_element_type=F32)
              + lax.dot_general(wlo_ref[...], t_hi, nt, preferred_element_type=F32))
    scores = _sigmoid(logits)
    biased = scores + bc_ref[...]

    per_group = n_exp // N_GROUPS
    sub = lax.broadcasted_iota(jnp.int32, (per_group, tm), 0)
    neg = -jnp.inf
    grp = [biased[g * per_group:(g + 1) * per_group, :] for g in range(N_GROUPS)]
    sc_g = [scores[g * per_group:(g + 1) * per_group, :] for g in range(N_GROUPS)]

    gscore = []
    for g in range(N_GROUPS):
        m1 = jnp.max(grp[g], axis=0, keepdims=True)
        first = jnp.min(jnp.where(grp[g] == m1, sub, per_group), axis=0, keepdims=True)
        m2 = jnp.max(jnp.where(sub == first, neg, grp[g]), axis=0, keepdims=True)
        gscore.append(m1 + m2)
    masked = []
    for g in range(N_GROUPS):
        beaten = jnp.zeros((1, tm), jnp.int32)
        for o in range(N_GROUPS):
            if o == g:
                continue
            wins = (gscore[o] > gscore[g]) | ((gscore[o] == gscore[g]) & (o < g))
            beaten = beaten + wins.astype(jnp.int32)
        masked.append(jnp.where(beaten < TOPK_GROUPS, grp[g], neg))

    eid = [sub + g * per_group for g in range(N_GROUPS)]
    member = [jnp.zeros((per_group, tm), F32) for _ in range(N_GROUPS)]
    idx_rows, score_rows = [], []
    for _ in range(TOP_K):
        m = masked[0]
        for g in range(1, N_GROUPS):
            m = jnp.maximum(m, masked[g])
        m = jnp.max(m, axis=0, keepdims=True)
        cand = jnp.where(masked[0] == m, eid[0], n_exp)
        for g in range(1, N_GROUPS):
            cand = jnp.minimum(cand, jnp.where(masked[g] == m, eid[g], n_exp))
        first = jnp.min(cand, axis=0, keepdims=True)
        picked = jnp.zeros((per_group, tm), F32)
        for g in range(N_GROUPS):
            sel = eid[g] == first
            picked = picked + jnp.where(sel, sc_g[g], 0.0)
            member[g] = jnp.where(sel, 1.0, member[g])
            masked[g] = jnp.where(sel, neg, masked[g])
        idx_rows.append(first)
        score_rows.append(jnp.sum(picked, axis=0, keepdims=True))

    total = score_rows[0]
    for k in range(1, TOP_K):
        total = total + score_rows[k]

    memb = jnp.concatenate(member, axis=0)
    earlier = jnp.where(lax.broadcasted_iota(jnp.int32, (tm, tm), 0)
                        < lax.broadcasted_iota(jnp.int32, (tm, tm), 1), 1.0, 0.0).astype(BF16)
    pos = _bdot(memb.astype(BF16), earlier) + carry_ref[...]
    pos_g = [pos[g * per_group:(g + 1) * per_group, :] for g in range(N_GROUPS)]
    for k in range(TOP_K):
        acc = jnp.zeros((per_group, tm), F32)
        for g in range(N_GROUPS):
            acc = acc + jnp.where(eid[g] == idx_rows[k], pos_g[g], 0.0)
        rank_ref[k:k + 1, :] = jnp.sum(acc, axis=0, keepdims=True).astype(jnp.int32)
        idx_ref[k:k + 1, :] = idx_rows[k]
        wt_ref[k:k + 1, :] = score_rows[k] / total * ROUTED_SCALE
    carry_ref[...] += jnp.sum(memb, axis=1, keepdims=True)
    cnt_ref[...] = carry_ref[...].astype(jnp.int32)


def _route(x, mod, w_router, b_corr):
    nb, s, d = x.shape
    t = nb * s
    n_exp = w_router.shape[1]
    tm = _tile(TM_ROUTE, s)
    per_b = s // tm
    w_t = w_router.T
    w_hi = w_t.astype(BF16)
    w_lo = (w_t - w_hi.astype(F32)).astype(BF16)
    kern = functools.partial(_router_kernel, n_exp=n_exp, tm=tm)
    out_spec = pl.BlockSpec((TOP_K, tm), lambda i: (0, i))
    return pl.pallas_call(
        kern,
        grid=(t // tm,),
        in_specs=[
            pl.BlockSpec((tm, d), lambda i: (i, 0)),
            pl.BlockSpec((1, 1, 1, d), lambda i: (3, i // per_b, 0, 0)),
            pl.BlockSpec((1, 1, 1, d), lambda i: (4, i // per_b, 0, 0)),
            pl.BlockSpec((n_exp, d), lambda i: (0, 0)),
            pl.BlockSpec((n_exp, d), lambda i: (0, 0)),
            pl.BlockSpec((n_exp, 1), lambda i: (0, 0)),
        ],
        out_specs=[out_spec, out_spec, out_spec, pl.BlockSpec((n_exp, 1), lambda i: (0, 0))],
        out_shape=[
            jax.ShapeDtypeStruct((TOP_K, t), jnp.int32),
            jax.ShapeDtypeStruct((TOP_K, t), F32),
            jax.ShapeDtypeStruct((TOP_K, t), jnp.int32),
            jax.ShapeDtypeStruct((n_exp, 1), jnp.int32),
        ],
        scratch_shapes=[pltpu.VMEM((n_exp, 1), F32)],
        compiler_params=_cparams(1),
        name="moe_router",
    )(x.reshape(t, d), mod, mod, w_hi, w_lo, b_corr.reshape(n_exp, 1).astype(F32))


def _slots_kernel(ps_ref, idx_ref, rank_ref, dest_ref, *, n_exp):
    idx = idx_ref[...]

    def add_start(e, acc):
        return acc + jnp.where(idx == e, ps_ref[e], 0)

    dest_ref[...] = lax.fori_loop(0, n_exp, add_start, rank_ref[...])


def _slots(idx, rank, pad_starts):
    k, t = idx.shape
    tm = _tile(2048, t)
    grid_spec = pltpu.PrefetchScalarGridSpec(
        num_scalar_prefetch=1,
        grid=(t // tm,),
        in_specs=[pl.BlockSpec((k, tm), lambda i, ps: (0, i)), pl.BlockSpec((k, tm), lambda i, ps: (0, i))],
        out_specs=pl.BlockSpec((k, tm), lambda i, ps: (0, i)),
    )
    return pl.pallas_call(
        functools.partial(_slots_kernel, n_exp=pad_starts.shape[0]),
        grid_spec=grid_spec,
        out_shape=jax.ShapeDtypeStruct((k, t), jnp.int32),
        compiler_params=_cparams(1),
        name="moe_slots",
    )(pad_starts, idx, rank)


def _to_slabs(value):
    n_seg = value.shape[1] // LANES
    by_seg = jnp.stack([value[:, s * LANES:(s + 1) * LANES] for s in range(n_seg)], axis=0)
    return jnp.swapaxes(by_seg, 0, 1)


def _from_slabs(slabs):
    by_seg = jnp.swapaxes(slabs, 0, 1)
    return jnp.concatenate([by_seg[s] for s in range(slabs.shape[1])], axis=1)


def _dispatch_kernel(pend_ref, padded_ref, x_ref, sh_ref, sc_ref, dest_ref, wsg_ref, wsu_ref, wsd_ref,
                     xs_ref, shared_ref, tok_ref, zero_ref, sem, zsem, *, n_exp, tm, bm):
    i = pl.program_id(0)

    def zero_copy(e):
        return pltpu.make_async_copy(zero_ref, xs_ref.at[pl.ds(pend_ref[e] - bm, bm)], zsem)

    @pl.when(i == 0)
    def _():
        zero_ref[...] = jnp.zeros_like(zero_ref)

        def start(e, _):
            @pl.when(padded_ref[e] > 0)
            def _():
                zero_copy(e).start()
            return 0

        def wait(e, _):
            @pl.when(padded_ref[e] > 0)
            def _():
                zero_copy(e).wait()
            return 0

        def tail_copy(blk):
            return pltpu.make_async_copy(zero_ref, xs_ref.at[pl.ds(blk * bm, bm)], zsem)

        def start_tail(blk, _):
            tail_copy(blk).start()
            return 0

        def wait_tail(blk, _):
            tail_copy(blk).wait()
            return 0

        first_unused = pend_ref[n_exp - 1] // bm
        n_blk = xs_ref.shape[0] // bm
        lax.fori_loop(0, n_exp, start, 0)
        lax.fori_loop(first_unused, n_blk, start_tail, 0)
        lax.fori_loop(0, n_exp, wait, 0)
        lax.fori_loop(first_unused, n_blk, wait_tail, 0)

    n_steps = pl.num_programs(0)
    slot = i % 2

    def drain(buf):
        for k in range(TOP_K):
            pltpu.make_async_copy(tok_ref.at[buf], xs_ref.at[pl.ds(0, tm)], sem.at[buf]).wait()

    @pl.when(i >= 2)
    def _():
        drain(slot)

    tok = x_ref[...] * (1.0 + sc_ref[0, 0]) + sh_ref[0, 0]
    tok_ref[slot] = _to_slabs(tok)
    for r in range(tm):
        for k in range(TOP_K):
            pltpu.make_async_copy(tok_ref.at[slot, r], xs_ref.at[dest_ref[k, r]],
                                  sem.at[slot]).start(priority=k % 2)
    tok = tok.astype(BF16)
    hid = (_silu(_bdot(tok, wsg_ref[...])) * _bdot(tok, wsu_ref[...])).astype(BF16)
    shared_ref[...] = _bdot(hid, wsd_ref[...])

    @pl.when(i == n_steps - 1)
    def _():
        @pl.when(i >= 1)
        def _():
            drain(1 - slot)
        drain(slot)


def _dispatch(x, mod, dest, pad_ends, padded, n_slots, bm, ws_gate, ws_up, ws_down):
    nb, s, d = x.shape
    f = ws_gate.shape[-1]
    t = nb * s
    tm = _tile(TM_DISPATCH, s)
    per_b = s // tm
    n_exp = pad_ends.shape[0]
    n_seg = d // LANES
    assert n_seg % SUBLANES == 0, "token slabs must be whole (8, 128) tiles"
    kern = functools.partial(_dispatch_kernel, n_exp=n_exp, tm=tm, bm=bm)
    grid_spec = pltpu.PrefetchScalarGridSpec(
        num_scalar_prefetch=2,
        grid=(t // tm,),
        in_specs=[
            pl.BlockSpec((tm, d), lambda i, pe, pd: (i, 0)),
            pl.BlockSpec((1, 1, 1, d), lambda i, pe, pd: (3, i // per_b, 0, 0)),
            pl.BlockSpec((1, 1, 1, d), lambda i, pe, pd: (4, i // per_b, 0, 0)),
            pl.BlockSpec((TOP_K, tm), lambda i, pe, pd: (0, i), memory_space=pltpu.SMEM),
            pl.BlockSpec((d, f), lambda i, pe, pd: (0, 0)),
            pl.BlockSpec((d, f), lambda i, pe, pd: (0, 0)),
            pl.BlockSpec((f, d), lambda i, pe, pd: (0, 0)),
        ],
        out_specs=[pl.BlockSpec(memory_space=pl.ANY), pl.BlockSpec((tm, d), lambda i, pe, pd: (i, 0))],
        scratch_shapes=[
            pltpu.VMEM((2, tm, n_seg, LANES), F32),
            pltpu.VMEM((bm, n_seg, LANES), F32),
            pltpu.SemaphoreType.DMA((2,)),
            pltpu.SemaphoreType.DMA(()),
        ],
    )
    return pl.pallas_call(
        kern,
        grid_spec=grid_spec,
        out_shape=[jax.ShapeDtypeStruct((n_slots, n_seg, LANES), F32), jax.ShapeDtypeStruct((t, d), F32)],
        compiler_params=_cparams(1),
        name="moe_dispatch",
    )(pad_ends, padded, x.reshape(t, d), mod, mod, dest, ws_gate, ws_up, ws_down)


def _experts_kernel(be_ref, nu_ref, xs_ref, wg_ref, wu_ref, wd_ref, ys_ref):
    b = pl.program_id(0)

    @pl.when(b < nu_ref[0])
    def _():
        x = _from_slabs(xs_ref[...]).astype(BF16)
        g = _bdot(x, wg_ref[0])
        u = _bdot(x, wu_ref[0])
        ys_ref[...] = _to_slabs(_bdot((_silu(g) * u).astype(BF16), wd_ref[0]))

    @pl.when(b >= nu_ref[0])
    def _():
        ys_ref[...] = jnp.zeros_like(ys_ref)


def _grouped_experts(xs, blk_e, n_used, w_gate, w_up, w_down, bm):
    d, f = w_gate.shape[1:]
    n_seg = d // LANES
    n_blk = xs.shape[0] // bm

    def blk(b, be, nu):
        return jnp.minimum(b, nu[0] - 1)

    grid_spec = pltpu.PrefetchScalarGridSpec(
        num_scalar_prefetch=2,
        grid=(n_blk,),
        in_specs=[
            pl.BlockSpec((bm, n_seg, LANES), lambda b, be, nu: (blk(b, be, nu), 0, 0)),
            pl.BlockSpec((1, d, f), lambda b, be, nu: (be[blk(b, be, nu)], 0, 0)),
            pl.BlockSpec((1, d, f), lambda b, be, nu: (be[blk(b, be, nu)], 0, 0)),
            pl.BlockSpec((1, f, d), lambda b, be, nu: (be[blk(b, be, nu)], 0, 0)),
        ],
        out_specs=pl.BlockSpec((bm, n_seg, LANES), lambda b, be, nu: (b, 0, 0)),
    )
    return pl.pallas_call(
        _experts_kernel,
        grid_spec=grid_spec,
        out_shape=jax.ShapeDtypeStruct(xs.shape, F32),
        compiler_params=_cparams(1),
        name="moe_experts",
    )(blk_e, n_used, xs, w_gate, w_up, w_down)


def _combine_kernel(x_ref, gt_ref, dest_ref, dest_next_ref, wt_ref, ys_ref, shared_ref,
                    lng_ref, lnb_ref, xo_ref, rows_ref, sem, *, tm, alpha):
    i = pl.program_id(0)
    n_steps = pl.num_programs(0)
    slot = i % 2

    def gather(d_ref, buf):
        def issue(r, _):
            for k in range(TOP_K):
                pltpu.make_async_copy(ys_ref.at[d_ref[k, r]], rows_ref.at[buf, k, r],
                                      sem.at[buf]).start(priority=k % 2)
            return 0
        lax.fori_loop(0, tm, issue, 0)

    @pl.when(i == 0)
    def _():
        gather(dest_ref, 0)

    for r in range(tm):
        for k in range(TOP_K):
            pltpu.make_async_copy(ys_ref.at[dest_next_ref[k, r]], rows_ref.at[1 - slot, k, r],
                                  sem.at[1 - slot]).start(priority=k % 2)

    x = x_ref[...]
    f = shared_ref[...]

    for k in range(TOP_K):
        pltpu.make_async_copy(ys_ref.at[pl.ds(0, tm)], rows_ref.at[slot, k], sem.at[slot]).wait()
    n_seg = rows_ref.shape[3]
    wt = wt_ref[...]
    routed = jnp.broadcast_to(wt[:, 0:1, :], (tm, n_seg, LANES)) * rows_ref[slot, 0]
    for k in range(1, TOP_K):
        routed = routed + jnp.broadcast_to(wt[:, k:k + 1, :], (tm, n_seg, LANES)) * rows_ref[slot, k]
    f = f + _from_slabs(routed)
    v = alpha * x + gt_ref[0, 0] * f
    xo_ref[...] = _layer_norm(v, lng_ref[...], lnb_ref[...])

    @pl.when(i == n_steps - 1)
    def _():
        for k in range(TOP_K):
            pltpu.make_async_copy(ys_ref.at[pl.ds(0, tm)], rows_ref.at[1 - slot, k], sem.at[1 - slot]).wait()


def _combine(x, mod, dest, wt_splat, ys, shared, ln_g, ln_b, alpha):
    nb, s, d = x.shape
    t = nb * s
    tm = _tile(TM_COMBINE, s)
    per_b = s // tm
    n_seg = d // LANES
    kern = functools.partial(_combine_kernel, tm=tm, alpha=alpha)
    vspec = pl.BlockSpec((1, d), lambda i: (0, 0))
    n_steps = t // tm
    out = pl.pallas_call(
        kern,
        grid=(n_steps,),
        in_specs=[
            pl.BlockSpec((tm, d), lambda i: (i, 0)),
            pl.BlockSpec((1, 1, 1, d), lambda i: (5, i // per_b, 0, 0)),
            pl.BlockSpec((TOP_K, tm), lambda i: (0, i), memory_space=pltpu.SMEM),
            pl.BlockSpec((TOP_K, tm), lambda i: (0, jnp.minimum(i + 1, n_steps - 1)), memory_space=pltpu.SMEM),
            pl.BlockSpec((tm, TOP_K, LANES), lambda i: (i, 0, 0)),
            pl.BlockSpec(memory_space=pl.ANY),
            pl.BlockSpec((tm, d), lambda i: (i, 0)),
            vspec, vspec,
        ],
        out_specs=pl.BlockSpec((tm, d), lambda i: (i, 0)),
        out_shape=jax.ShapeDtypeStruct((t, d), F32),
        scratch_shapes=[pltpu.VMEM((2, TOP_K, tm, n_seg, LANES), F32), pltpu.SemaphoreType.DMA((2,))],
        compiler_params=_cparams(1),
        name="moe_combine",
    )(x.reshape(t, d), mod, dest, dest, wt_splat, ys, shared, ln_g.reshape(1, d), ln_b.reshape(1, d))
    return out.reshape(nb, s, d)


def _moe_block(x, mod, w_router, b_router, w_gate, w_up, w_down, ws_gate, ws_up, ws_down, ln_g, ln_b, alpha):
    nb, s, d = x.shape
    t = nb * s
    n_exp = w_router.shape[1]
    bm = _tile(MOE_BLOCK, t * TOP_K)
    idx, wt, rank, counts = _route(x, mod, w_router, b_router)

    counts = counts.reshape(n_exp)
    padded = (counts + bm - 1) // bm * bm
    pad_ends = jnp.cumsum(padded).astype(jnp.int32)
    pad_starts = pad_ends - padded
    dest = _slots(idx, rank, pad_starts.astype(jnp.int32))
    n_blk = t * TOP_K // bm + n_exp
    n_used = (pad_ends[-1:] // bm).astype(jnp.int32)
    blk_start = jnp.arange(n_blk, dtype=jnp.int32) * bm
    blk_e = jnp.sum((pad_ends[None, :] <= blk_start[:, None]).astype(jnp.int32), axis=1)
    blk_e = jnp.minimum(blk_e, n_exp - 1).astype(jnp.int32)

    xs, shared = _dispatch(x, mod, dest, pad_ends, padded.astype(jnp.int32), n_blk * bm, bm,
                           ws_gate.astype(BF16), ws_up.astype(BF16), ws_down.astype(BF16))
    ys = _grouped_experts(xs, blk_e, n_used, w_gate, w_up, w_down, bm)
    wt_splat = jnp.broadcast_to(wt.T[:, :, None], (t, TOP_K, LANES))
    return _combine(x, mod, dest, wt_splat, ys, shared, ln_g, ln_b, alpha)


def kernel(x, c, ctx, c_ctx, w_mod, b_mod, ln1_g, ln1_b, ln2_g, ln2_b, rg_w_in, rg_b_in, rg_conv_w, rg_conv_b, rg_w_gates, rg_b_gates, rg_lambda, rg_w_out, rg_b_out, cf_w_pw1, cf_b_pw1, cf_w_dw, cf_b_dw, cf_norm_g, cf_norm_b, cf_w_pw2, cf_b_pw2, moe_w_router, moe_b_router, moe_w_gate, moe_w_up, moe_w_down, sh_w_gate, sh_w_up, sh_w_down):
    nb, s, d = x.shape
    depth = w_mod.shape[0]
    assert depth == 2 and nb + 1 <= SUBLANES
    alpha = (2 * depth) ** 0.25

    c_rows = jnp.zeros((SUBLANES, d), F32).at[:nb].set(c).at[nb].set(c_ctx)
    mod_all = _adaln_vectors(c_rows, w_mod, b_mod)
    mod_all = mod_all.reshape(depth, SUBLANES, 6, 1, d).transpose(0, 2, 1, 3, 4)
    mod_lat = [mod_all[l, :, :nb] for l in range(depth)]
    mod_ctx = jnp.broadcast_to(mod_all[0, :, nb:nb + 1], (6, nb, 1, d))

    n_h, hd = rg_w_gates.shape[3], rg_w_gates.shape[4]
    r = n_h * hd
    w_in = rg_w_in[0].astype(BF16)
    wg = [jnp.concatenate([rg_w_gates[0, dr, 0], rg_w_gates[0, dr, 1]], axis=-1).astype(BF16) for dr in range(2)]
    bg = [jnp.concatenate([rg_b_gates[0, dr, 0], rg_b_gates[0, dr, 1]], axis=-1).reshape(n_h, 1, 2 * hd)
          for dr in range(2)]
    conv_w, conv_b = rg_conv_w[0], rg_conv_b[0]
    zero_state = jnp.zeros((nb, n_h, 1, hd), F32)

    proj_c, _ = _modulated_projection(ctx, mod_ctx, w_in, rg_b_in[0], BF16)
    _, hcf = _rglru_scan(proj_c, conv_w, conv_b, wg[0], bg[0], rg_lambda[0, 0], zero_state, reverse=False)
    _, hcb = _rglru_scan(proj_c, conv_w, conv_b, wg[1], bg[1], rg_lambda[0, 1], zero_state, reverse=True)

    expert_stacks = (moe_w_gate, moe_w_up, moe_w_down)
    proj_x, experts0 = _modulated_projection(x, mod_lat[0], w_in, rg_b_in[0], BF16, expert_stacks, 0)
    hxf, _ = _rglru_scan(proj_x, conv_w, conv_b, wg[0], bg[0], rg_lambda[0, 0], hcf, reverse=False)
    readout = (hxf, rg_w_out[0].astype(BF16), rg_b_out[0], x, mod_lat[0], ln1_g[0], ln1_b[0])
    x, _ = _rglru_scan(proj_x, conv_w, conv_b, wg[1], bg[1], rg_lambda[0, 1], hcb, reverse=True,
                       readout=readout, alpha=alpha)
    x = _moe_block(x, mod_lat[0], moe_w_router[0], moe_b_router[0],
                   *experts0,
                   sh_w_gate[0], sh_w_up[0], sh_w_down[0], ln2_g[0], ln2_b[0], alpha)

    z, experts1 = _modulated_glu(x, mod_lat[1], cf_w_pw1[0].astype(BF16), cf_b_pw1[0], expert_stacks, 1)
    x = _conformer_tail(z, cf_w_dw[0], cf_b_dw[0], cf_norm_g[0], cf_norm_b[0], cf_w_pw2[0].astype(BF16),
                        cf_b_pw2[0], x, mod_lat[1], ln1_g[1], ln1_b[1], alpha)
    x = _moe_block(x, mod_lat[1], moe_w_router[1], moe_b_router[1],
                   *experts1,
                   sh_w_gate[1], sh_w_up[1], sh_w_down[1], ln2_g[1], ln2_b[1], alpha)
    return x
```

```python
import functools

import jax
import jax.numpy as jnp
from jax import lax
from jax.experimental import pallas as pl
from jax.experimental.pallas import tpu as pltpu

F32 = jnp.float32
BF16 = jnp.bfloat16

RG_C = 8.0
N_GROUPS = 8
TOPK_GROUPS = 4
TOP_K = 8
ROUTED_SCALE = 2.5
LN_EPS = 1e-5

VMEM_LIMIT_BYTES = 56 * 1024 * 1024
SUBLANES = 8
LANES = 128
BF16_ROWS = 16

TM_PROJ = 512
TM_SCAN = 512
HEADS_PER_STEP = 5
TM_CONV = 256
CONV_ROW_CHUNK = 128
CONV_LANE_CHUNK = 256
TM_ROUTE = 512
TM_SLOTS = 2048
TM_DISPATCH = 256
TM_COMBINE = 128
MOE_BLOCK = 512
TN_ADALN = 1024
TN_PROJ = 2560
TN_GLU = 1024


def _tile(default, n):
    t = min(default, n)
    assert n % t == 0, (default, n)
    return t


def _cparams(n_axes):
    return pltpu.CompilerParams(
        dimension_semantics=("arbitrary",) * n_axes,
        vmem_limit_bytes=VMEM_LIMIT_BYTES,
    )


def _sigmoid(v):
    return 0.5 * jnp.tanh(0.5 * v) + 0.5


def _sqrt_nonneg(v):
    return jnp.where(v > 0.0, v * lax.rsqrt(v), 0.0)


def _silu(v):
    return v * _sigmoid(v)


def _gelu_tanh(v):
    c = 0.7978845608028654
    return 0.5 * v * (1.0 + jnp.tanh(c * (v + 0.044715 * (v * v * v))))


def _softplus(v):
    return jnp.maximum(v, 0.0) + jnp.log(1.0 + jnp.exp(-jnp.abs(v)))


def _layer_norm(v, g, b):
    mu = jnp.mean(v, axis=-1, keepdims=True)
    d = v - mu
    var = jnp.mean(d * d, axis=-1, keepdims=True)
    return d * lax.rsqrt(var + LN_EPS) * g + b


def _bdot(a, b):
    return jnp.dot(a, b, preferred_element_type=F32)


def _mod_kernel(c_ref, w_ref, b_ref, o_ref):
    s = _silu(c_ref[...])
    o_ref[0] = _bdot(s.astype(BF16), w_ref[0].astype(BF16)) + b_ref[0]


def _adaln_vectors(c_rows, w_mod, b_mod):
    depth, d, d6 = w_mod.shape
    tn = _tile(TN_ADALN, d6)
    return pl.pallas_call(
        _mod_kernel,
        grid=(depth, d6 // tn),
        in_specs=[
            pl.BlockSpec((SUBLANES, d), lambda l, n: (0, 0)),
            pl.BlockSpec((1, d, tn), lambda l, n: (l, 0, n)),
            pl.BlockSpec((1, 1, tn), lambda l, n: (l, 0, n)),
        ],
        out_specs=pl.BlockSpec((1, SUBLANES, tn), lambda l, n: (l, 0, n)),
        out_shape=jax.ShapeDtypeStruct((depth, SUBLANES, d6), F32),
        compiler_params=_cparams(2),
        name="adaln_vectors",
    )(c_rows, w_mod, b_mod.reshape(depth, 1, d6))


class _SideCast:
    def __init__(self, stacks, layer, grid):
        self.n = len(stacks)
        self.shapes = [w.shape[1:] for w in stacks]
        n_steps = 1
        for g in grid:
            n_steps *= g

        def step(*ids):
            lin = ids[0]
            for g, v in zip(grid[1:], ids[1:]):
                lin = lin * g + v
            return lin

        self.in_specs, self.out_specs, self.out_shape, self.args = [], [], [], []
        for w in stacks:
            n_l, n_e, a, b = w.shape
            rows = n_e * a
            blk = rows // n_steps
            assert rows % n_steps == 0 and blk % BF16_ROWS == 0, (w.shape, n_steps)
            self.in_specs.append(pl.BlockSpec((1, blk, b), lambda *ids: (layer, step(*ids), 0)))
            self.out_specs.append(pl.BlockSpec((blk, b), lambda *ids: (step(*ids), 0)))
            self.out_shape.append(jax.ShapeDtypeStruct((rows, b), BF16))
            self.args.append(w.reshape(n_l, rows, b))

    @staticmethod
    def run(src_refs, dst_refs):
        for src, dst in zip(src_refs, dst_refs):
            dst[...] = src[0].astype(dst.dtype)

    def unpack(self, outs):
        return [o.reshape(shape) for o, shape in zip(outs, self.shapes)]


def _proj_kernel(x_ref, sh_ref, sc_ref, w_ref, b_ref, *rest, n_side):
    o_ref = rest[n_side]
    h = x_ref[0] * (1.0 + sc_ref[0, 0]) + sh_ref[0, 0]
    o_ref[0] = (_bdot(h.astype(BF16), w_ref[...]) + b_ref[...]).astype(o_ref.dtype)
    _SideCast.run(rest[:n_side], rest[n_side + 1:])


def _modulated_projection(x, mod, w_bf16, bias, out_dtype, side_stacks=(), side_layer=0):
    nb, s, d = x.shape
    n = w_bf16.shape[1]
    tm = _tile(TM_PROJ, s)
    tn = _tile(TN_PROJ, n)
    dm = mod.shape[-1]
    grid = (n // tn, nb, s // tm)
    side = _SideCast(side_stacks, side_layer, grid)
    outs = pl.pallas_call(
        functools.partial(_proj_kernel, n_side=side.n),
        grid=grid,
        in_specs=[
            pl.BlockSpec((1, tm, d), lambda j, b, i: (b, i, 0)),
            pl.BlockSpec((1, 1, 1, dm), lambda j, b, i: (0, b, 0, 0)),
            pl.BlockSpec((1, 1, 1, dm), lambda j, b, i: (1, b, 0, 0)),
            pl.BlockSpec((d, tn), lambda j, b, i: (0, j)),
            pl.BlockSpec((1, tn), lambda j, b, i: (0, j)),
        ] + side.in_specs,
        out_specs=[pl.BlockSpec((1, tm, tn), lambda j, b, i: (b, i, j))] + side.out_specs,
        out_shape=[jax.ShapeDtypeStruct((nb, s, n), out_dtype)] + side.out_shape,
        compiler_params=_cparams(3),
        name="modulated_projection",
    )(x, mod, mod, w_bf16, bias.reshape(1, n), *side.args)
    return outs[0], side.unpack(outs[1:])


def _rglru_kernel(*refs, reverse, readout, n_t, tm, alpha):
    if readout:
        (cur_ref, prev_ref, next_ref, perm_ref, cw_ref, cb_ref, wg_ref, bg_ref, lam_ref, h0_ref,
         hf_ref, gate_ref, wout_ref, bout_ref, x_ref, g1_ref, lng_ref, lnb_ref,
         xo_ref, hlast_ref,
         win_ref, a_ref, hl_ref, carry_ref, acc_ref) = refs
    else:
        (cur_ref, prev_ref, next_ref, perm_ref, cw_ref, cb_ref, wg_ref, bg_ref, lam_ref, h0_ref,
         ho_ref, hlast_ref,
         win_ref, a_ref, hl_ref, carry_ref) = refs
    i = pl.program_id(1)
    h = pl.program_id(2)
    n_h = pl.num_programs(2)
    ti = (n_t - 1 - i) if reverse else i
    hd = cur_ref.shape[-1]
    head_dim = wg_ref.shape[1]
    group = hd // head_dim
    chunk = tm // SUBLANES
    sub = lax.broadcasted_iota(jnp.int32, (SUBLANES, hd), 0)

    rnn = _bdot(perm_ref[...], cur_ref[0])
    prev_last = jnp.where(ti == 0, 0.0, prev_ref[0].astype(F32)[BF16_ROWS - 1:, :])
    nxt = jnp.where(ti == n_t - 1, 0.0, next_ref[0].astype(F32)[:2, :])
    win_ref[0:SUBLANES, :] = jnp.where(
        sub == 0, prev_last, pltpu.roll(rnn[tm - SUBLANES:, :], 1, axis=0))
    win_ref[SUBLANES:SUBLANES + tm, :] = rnn
    for extra in range(2):
        lo = SUBLANES + tm + extra * SUBLANES
        win_ref[lo:lo + SUBLANES, :] = jnp.where(
            sub == SUBLANES - 1, nxt[extra:extra + 1, :],
            pltpu.roll(rnn[extra * SUBLANES:(extra + 1) * SUBLANES, :], SUBLANES - 1, axis=0))
    n_taps = cw_ref.shape[0]
    xb = cb_ref[...] + cw_ref[0:1, :] * win_ref[pl.ds(0, tm), :]
    for k in range(1, n_taps):
        xb = xb + cw_ref[k:k + 1, :] * win_ref[pl.ds(k * SUBLANES, tm), :]

    xb16 = xb.astype(BF16)
    pre = [_bdot(xb16[:, q * head_dim:(q + 1) * head_dim], wg_ref[q]) + bg_ref[q] for q in range(group)]
    r = _sigmoid(jnp.concatenate([p[:, :head_dim] for p in pre], axis=1))
    gi = _sigmoid(jnp.concatenate([p[:, head_dim:] for p in pre], axis=1))
    log_a = (-RG_C) * r * _softplus(-lam_ref[...])
    a = jnp.exp(log_a)
    a_ref[...] = a
    hl_ref[...] = _sqrt_nonneg(1.0 - a * a) * gi * xb

    if readout:
        gate_act = _gelu_tanh(_bdot(perm_ref[...], gate_ref[0]))

    def scan_step(jj, carry):
        a_cum, h_loc = carry
        j = (chunk - 1 - jj) if reverse else jj
        rows = pl.ds(pl.multiple_of(j * SUBLANES, SUBLANES), SUBLANES)
        aj = a_ref[rows, :]
        h_loc = aj * h_loc + hl_ref[rows, :]
        a_cum = a_cum * aj
        hl_ref[rows, :] = h_loc
        a_ref[rows, :] = a_cum
        return a_cum, h_loc

    a_end, h_end = lax.fori_loop(
        0, chunk, scan_step, (jnp.ones((SUBLANES, hd), F32), jnp.zeros((SUBLANES, hd), F32)),
        unroll=SUBLANES)

    @pl.when(i == 0)
    def _():
        for q in range(group):
            carry_ref[h * group + q] = h0_ref[0, h * group + q]

    state = jnp.concatenate([carry_ref[h * group + q] for q in range(group)], axis=1)
    entry = jnp.zeros((SUBLANES, hd), F32)
    for s in (range(SUBLANES - 1, -1, -1) if reverse else range(SUBLANES)):
        entry = jnp.where(sub == s, state, entry)
        state = a_end[s:s + 1, :] * state + h_end[s:s + 1, :]
    for q in range(group):
        carry_ref[h * group + q] = state[:, q * head_dim:(q + 1) * head_dim]
        hlast_ref[0, h * group + q] = state[:, q * head_dim:(q + 1) * head_dim]

    h_full = hl_ref[...] + a_ref[...] * jnp.concatenate([entry] * chunk, axis=0)
    if not readout:
        ho_ref[0] = h_full.astype(ho_ref.dtype)
    else:
        h_sum = h_full + hf_ref[0].astype(F32)
        contrib = _bdot((h_sum * gate_act).astype(BF16), wout_ref[...])
        n_lc = acc_ref.shape[0]

        @pl.when(h == 0)
        def _():
            for c in range(n_lc):
                acc_ref[c] = contrib[:, c * LANES:(c + 1) * LANES]

        @pl.when(h != 0)
        def _():
            for c in range(n_lc):
                acc_ref[c] += contrib[:, c * LANES:(c + 1) * LANES]

        @pl.when(h == n_h - 1)
        def _():
            y = jnp.concatenate(
                [jnp.concatenate(
                    [acc_ref[c, pl.ds(q * SUBLANES * SUBLANES + s, SUBLANES, stride=SUBLANES), :]
                     for s in range(SUBLANES) for q in range(chunk // SUBLANES)], axis=0)
                 for c in range(n_lc)], axis=1) + bout_ref[...]
            v = alpha * x_ref[0] + g1_ref[0, 0] * y
            xo_ref[0] = _layer_norm(v, lng_ref[...], lnb_ref[...])


def _rglru_scan(proj, conv_w, conv_b, wg, bg, lam, h0, *, reverse, h_dtype=BF16, readout=None, alpha=None):
    nb, s, r2 = proj.shape
    r = r2 // 2
    n_h, hd = wg.shape[0], wg.shape[1]
    tm = _tile(TM_SCAN, s)
    n_t = s // tm
    hb = tm // BF16_ROWS
    n_hb = s // BF16_ROWS

    def tix(i):
        return (n_t - 1 - i) if reverse else i

    rows = jnp.arange(tm, dtype=jnp.int32)
    time_of_row = (rows % SUBLANES) * (tm // SUBLANES) + rows // SUBLANES
    perm = (time_of_row[:, None] == rows[None, :]).astype(BF16)
    perm_spec = pl.BlockSpec((tm, tm), lambda b, i, h: (0, 0))

    group = HEADS_PER_STEP if n_h % HEADS_PER_STEP == 0 else 1
    n_g, gw = n_h // group, group * hd
    in_specs = [
        pl.BlockSpec((1, tm, gw), lambda b, i, h: (b, tix(i), n_g + h)),
        pl.BlockSpec((1, BF16_ROWS, gw), lambda b, i, h: (b, jnp.maximum(tix(i) * hb - 1, 0), n_g + h)),
        pl.BlockSpec((1, BF16_ROWS, gw), lambda b, i, h: (b, jnp.minimum((tix(i) + 1) * hb, n_hb - 1), n_g + h)),
        perm_spec,
        pl.BlockSpec((conv_w.shape[0], gw), lambda b, i, h: (0, h)),
        pl.BlockSpec((1, gw), lambda b, i, h: (0, h)),
        pl.BlockSpec((group, hd, 2 * hd), lambda b, i, h: (h, 0, 0)),
        pl.BlockSpec((group, 1, 2 * hd), lambda b, i, h: (h, 0, 0)),
        pl.BlockSpec((1, gw), lambda b, i, h: (0, h)),
        pl.BlockSpec((1, n_h, 1, hd), lambda b, i, h: (b, 0, 0, 0)),
    ]
    args = [proj, proj, proj, perm, conv_w, conv_b.reshape(1, r), wg, bg, lam.reshape(1, r), h0]
    assert conv_w.shape[0] == 4 and tm // SUBLANES >= 2, "window below covers taps at time offsets -1 .. +2"
    scratch = [
        pltpu.VMEM((tm + 3 * SUBLANES, gw), F32),
        pltpu.VMEM((tm, gw), F32),
        pltpu.VMEM((tm, gw), F32),
        pltpu.VMEM((n_h, 1, hd), F32),
    ]
    hlast_spec = pl.BlockSpec((1, n_h, 1, hd), lambda b, i, h: (b, 0, 0, 0))
    hlast_shape = jax.ShapeDtypeStruct((nb, n_h, 1, hd), F32)
    if readout is None:
        out_specs = [pl.BlockSpec((1, tm, gw), lambda b, i, h: (b, tix(i), h)), hlast_spec]
        out_shape = [jax.ShapeDtypeStruct((nb, s, r), h_dtype), hlast_shape]
    else:
        h_other, w_out, b_out, x, mod, ln_g, ln_b = readout
        d = x.shape[-1]
        in_specs += [
            pl.BlockSpec((1, tm, gw), lambda b, i, h: (b, tix(i), h)),
            pl.BlockSpec((1, tm, gw), lambda b, i, h: (b, tix(i), h)),
            pl.BlockSpec((gw, d), lambda b, i, h: (h, 0)),
            pl.BlockSpec((1, d), lambda b, i, h: (0, 0)),
            pl.BlockSpec((1, tm, d), lambda b, i, h: (b, tix(i), 0)),
            pl.BlockSpec((1, 1, 1, d), lambda b, i, h: (2, b, 0, 0)),
            pl.BlockSpec((1, d), lambda b, i, h: (0, 0)),
            pl.BlockSpec((1, d), lambda b, i, h: (0, 0)),
        ]
        args += [h_other, proj, w_out, b_out.reshape(1, d), x, mod, ln_g.reshape(1, d), ln_b.reshape(1, d)]
        assert (tm // SUBLANES) % SUBLANES == 0, "readout un-interleaves 8 vreg rows at a time"
        scratch += [pltpu.VMEM((d // LANES, tm, LANES), F32)]
        out_specs = [pl.BlockSpec((1, tm, d), lambda b, i, h: (b, tix(i), 0)), hlast_spec]
        out_shape = [jax.ShapeDtypeStruct((nb, s, d), F32), hlast_shape]
    kern = functools.partial(_rglru_kernel, reverse=reverse, readout=readout is not None,
                             n_t=n_t, tm=tm, alpha=alpha)
    return pl.pallas_call(
        kern,
        grid=(nb, n_t, n_g),
        in_specs=in_specs,
        out_specs=out_specs,
        out_shape=out_shape,
        scratch_shapes=scratch,
        compiler_params=_cparams(3),
        name="rglru_bwd_readout" if readout is not None else ("rglru_bwd" if reverse else "rglru_fwd"),
    )(*args)


def _glu_kernel(x_ref, sh_ref, sc_ref, wv_ref, wg_ref, bv_ref, bgate_ref, *rest, n_side):
    o_ref = rest[n_side]
    h = (x_ref[0] * (1.0 + sc_ref[0, 0]) + sh_ref[0, 0]).astype(BF16)
    val = _bdot(h, wv_ref[...]) + bv_ref[...]
    gate = _bdot(h, wg_ref[...]) + bgate_ref[...]
    o_ref[0] = (val * _sigmoid(gate)).astype(o_ref.dtype)
    _SideCast.run(rest[:n_side], rest[n_side + 1:])


def _modulated_glu(x, mod, w_bf16, bias, side_stacks=(), side_layer=0):
    nb, s, d = x.shape
    n = w_bf16.shape[1] // 2
    tm = _tile(TM_PROJ, s)
    tn = _tile(TN_GLU, n)
    nj = n // tn
    b2 = bias.reshape(1, 2 * n)
    grid = (nj, nb, s // tm)
    side = _SideCast(side_stacks, side_layer, grid)
    outs = pl.pallas_call(
        functools.partial(_glu_kernel, n_side=side.n),
        grid=grid,
        in_specs=[
            pl.BlockSpec((1, tm, d), lambda j, b, i: (b, i, 0)),
            pl.BlockSpec((1, 1, 1, d), lambda j, b, i: (0, b, 0, 0)),
            pl.BlockSpec((1, 1, 1, d), lambda j, b, i: (1, b, 0, 0)),
            pl.BlockSpec((d, tn), lambda j, b, i: (0, j)),
            pl.BlockSpec((d, tn), lambda j, b, i: (0, nj + j)),
            pl.BlockSpec((1, tn), lambda j, b, i: (0, j)),
            pl.BlockSpec((1, tn), lambda j, b, i: (0, nj + j)),
        ] + side.in_specs,
        out_specs=[pl.BlockSpec((1, tm, tn), lambda j, b, i: (b, i, j))] + side.out_specs,
        out_shape=[jax.ShapeDtypeStruct((nb, s, n), BF16)] + side.out_shape,
        compiler_params=_cparams(3),
        name="modulated_glu",
    )(x, mod, mod, w_bf16, w_bf16, b2, b2, *side.args)
    return outs[0], side.unpack(outs[1:])


def _conformer_tail_kernel(cur_ref, prev_ref, next_ref, wdw_ref, bdw_ref, ng_ref, nb_ref, w2_ref, b2_ref,
                           x_ref, g1_ref, lng_ref, lnb_ref, xo_ref, win_ref, z_ref, shift_ref, *, n_t, tm, alpha,
                           lane_chunk, row_chunk):
    i = pl.program_id(1)
    d = cur_ref.shape[-1]
    halo = BF16_ROWS
    win_ref[0:halo, :] = jnp.where(i == 0, 0.0, prev_ref[0].astype(F32))
    win_ref[halo:halo + tm, :] = cur_ref[0].astype(F32)
    win_ref[halo + tm:, :] = jnp.where(i == n_t - 1, 0.0, next_ref[0].astype(F32))
    n_taps = wdw_ref.shape[0]
    base = halo - (n_taps - 1) // 2

    n_shift = shift_ref.shape[1]
    for c in range(d // lane_chunk):
        lanes = slice(c * lane_chunk, (c + 1) * lane_chunk)
        for s in range(SUBLANES):
            shift_ref[s] = win_ref[pl.ds(s, n_shift), lanes]

        def rows_step(rc, _, lanes=lanes):
            r0 = pl.multiple_of(rc * row_chunk, row_chunk)
            acc = jnp.zeros((row_chunk, lane_chunk), F32) + bdw_ref[:, lanes]
            for k in range(n_taps):
                q, s = divmod(base + k, SUBLANES)
                rows = pl.ds(pl.multiple_of(r0 + q * SUBLANES, SUBLANES), row_chunk)
                acc = acc + wdw_ref[k:k + 1, lanes] * shift_ref[s, rows, :]
            z_ref[pl.ds(r0, row_chunk), lanes] = acc
            return 0

        lax.fori_loop(0, tm // row_chunk, rows_step, 0)

    z = _layer_norm(z_ref[...], ng_ref[...], nb_ref[...])
    y = _bdot(_silu(z).astype(BF16), w2_ref[...]) + b2_ref[...]
    v = alpha * x_ref[0] + g1_ref[0, 0] * y
    xo_ref[0] = _layer_norm(v, lng_ref[...], lnb_ref[...])


def _conformer_tail(z, w_dw, b_dw, n_g, n_b, w2_bf16, b2, x, mod, ln_g, ln_b, alpha):
    nb, s, d = z.shape
    tm = _tile(TM_CONV, s)
    n_t = s // tm
    hb = tm // BF16_ROWS
    n_hb = s // BF16_ROWS
    lane_chunk = min(CONV_LANE_CHUNK, d)
    n_taps = w_dw.shape[0]
    last_tap_row = BF16_ROWS + (n_taps - 1) // 2
    n_shift = tm + last_tap_row // SUBLANES * SUBLANES
    assert n_shift + SUBLANES - 1 <= tm + 2 * BF16_ROWS and (n_taps - 1) // 2 <= BF16_ROWS
    kern = functools.partial(_conformer_tail_kernel, n_t=n_t, tm=tm, alpha=alpha,
                             lane_chunk=lane_chunk, row_chunk=min(CONV_ROW_CHUNK, tm))
    vec = lambda a: a.reshape(1, d)
    vspec = pl.BlockSpec((1, d), lambda b, i: (0, 0))
    return pl.pallas_call(
        kern,
        grid=(nb, n_t),
        in_specs=[
            pl.BlockSpec((1, tm, d), lambda b, i: (b, i, 0)),
            pl.BlockSpec((1, BF16_ROWS, d), lambda b, i: (b, jnp.maximum(i * hb - 1, 0), 0)),
            pl.BlockSpec((1, BF16_ROWS, d), lambda b, i: (b, jnp.minimum((i + 1) * hb, n_hb - 1), 0)),
            pl.BlockSpec((w_dw.shape[0], d), lambda b, i: (0, 0)),
            vspec, vspec, vspec,
            pl.BlockSpec((d, d), lambda b, i: (0, 0)),
            vspec,
            pl.BlockSpec((1, tm, d), lambda b, i: (b, i, 0)),
            pl.BlockSpec((1, 1, 1, d), lambda b, i: (2, b, 0, 0)),
            vspec, vspec,
        ],
        out_specs=pl.BlockSpec((1, tm, d), lambda b, i: (b, i, 0)),
        out_shape=jax.ShapeDtypeStruct((nb, s, d), F32),
        scratch_shapes=[pltpu.VMEM((tm + 2 * BF16_ROWS, d), F32), pltpu.VMEM((tm, d), F32),
                        pltpu.VMEM((SUBLANES, n_shift, lane_chunk), F32)],
        compiler_params=_cparams(2),
        name="conformer_tail",
    )(z, z, z, w_dw, vec(b_dw), vec(n_g), vec(n_b), w2_bf16, vec(b2), x, mod, vec(ln_g), vec(ln_b))


def _router_kernel(x_ref, sh_ref, sc_ref, whi_ref, wlo_ref, bc_ref,
                   idx_ref, wt_ref, rank_ref, cnt_ref, carry_ref, *, n_exp, tm):
    i = pl.program_id(0)

    @pl.when(i == 0)
    def _():
        carry_ref[...] = jnp.zeros_like(carry_ref)

    tok = x_ref[...] * (1.0 + sc_ref[0, 0]) + sh_ref[0, 0]
    t_hi = tok.astype(BF16)
    t_lo = (tok - t_hi.astype(F32)).astype(BF16)
    nt = (((1,), (1,)), ((), ()))
    logits = (lax.dot_general(whi_ref[...], t_hi, nt, preferred_element_type=F32)
              + lax.dot_general(whi_ref[...], t_lo, nt, preferred_element_type=F32)
              + lax.dot_general(wlo_ref[...], t_hi, nt, preferred_element_type=F32))
    scores = _sigmoid(logits)
    biased = scores + bc_ref[...]

    per_group = n_exp // N_GROUPS
    sub = lax.broadcasted_iota(jnp.int32, (per_group, tm), 0)
    neg = -jnp.inf
    grp = [biased[g * per_group:(g + 1) * per_group, :] for g in range(N_GROUPS)]
    sc_g = [scores[g * per_group:(g + 1) * per_group, :] for g in range(N_GROUPS)]

    gscore = []
    for g in range(N_GROUPS):
        m1 = jnp.max(grp[g], axis=0, keepdims=True)
        first = jnp.min(jnp.where(grp[g] == m1, sub, per_group), axis=0, keepdims=True)
        m2 = jnp.max(jnp.where(sub == first, neg, grp[g]), axis=0, keepdims=True)
        gscore.append(m1 + m2)
    masked = []
    for g in range(N_GROUPS):
        beaten = jnp.zeros((1, tm), jnp.int32)
        for o in range(N_GROUPS):
            if o == g:
                continue
            wins = (gscore[o] > gscore[g]) | ((gscore[o] == gscore[g]) & (o < g))
            beaten = beaten + wins.astype(jnp.int32)
        masked.append(jnp.where(beaten < TOPK_GROUPS, grp[g], neg))

    eid = [sub + g * per_group for g in range(N_GROUPS)]
    member = [jnp.zeros((per_group, tm), F32) for _ in range(N_GROUPS)]
    idx_rows, score_rows = [], []
    for _ in range(TOP_K):
        m = masked[0]
        for g in range(1, N_GROUPS):
            m = jnp.maximum(m, masked[g])
        m = jnp.max(m, axis=0, keepdims=True)
        cand = jnp.where(masked[0] == m, eid[0], n_exp)
        for g in range(1, N_GROUPS):
            cand = jnp.minimum(cand, jnp.where(masked[g] == m, eid[g], n_exp))
        first = jnp.min(cand, axis=0, keepdims=True)
        picked = jnp.zeros((per_group, tm), F32)
        for g in range(N_GROUPS):
            sel = eid[g] == first
            picked = picked + jnp.where(sel, sc_g[g], 0.0)
            member[g] = jnp.where(sel, 1.0, member[g])
            masked[g] = jnp.where(sel, neg, masked[g])
        idx_rows.append(first)
        score_rows.append(jnp.sum(picked, axis=0, keepdims=True))

    total = score_rows[0]
    for k in range(1, TOP_K):
        total = total + score_rows[k]

    memb = jnp.concatenate(member, axis=0)
    earlier = jnp.where(lax.broadcasted_iota(jnp.int32, (tm, tm), 0)
                        < lax.broadcasted_iota(jnp.int32, (tm, tm), 1), 1.0, 0.0).astype(BF16)
    pos = _bdot(memb.astype(BF16), earlier) + carry_ref[...]
    pos_g = [pos[g * per_group:(g + 1) * per_group, :] for g in range(N_GROUPS)]
    for k in range(TOP_K):
        acc = jnp.zeros((per_group, tm), F32)
        for g in range(N_GROUPS):
            acc = acc + jnp.where(eid[g] == idx_rows[k], pos_g[g], 0.0)
        rank_ref[k:k + 1, :] = jnp.sum(acc, axis=0, keepdims=True).astype(jnp.int32)
        idx_ref[k:k + 1, :] = idx_rows[k]
        wt_ref[k:k + 1, :] = score_rows[k] / total * ROUTED_SCALE
    carry_ref[...] += jnp.sum(memb, axis=1, keepdims=True)
    cnt_ref[...] = carry_ref[...].astype(jnp.int32)


def _route(x, mod, w_router, b_corr):
    nb, s, d = x.shape
    t = nb * s
    n_exp = w_router.shape[1]
    tm = _tile(TM_ROUTE, s)
    per_b = s // tm
    w_t = w_router.T
    w_hi = w_t.astype(BF16)
    w_lo = (w_t - w_hi.astype(F32)).astype(BF16)
    kern = functools.partial(_router_kernel, n_exp=n_exp, tm=tm)
    out_spec = pl.BlockSpec((TOP_K, tm), lambda i: (0, i))
    return pl.pallas_call(
        kern,
        grid=(t // tm,),
        in_specs=[
            pl.BlockSpec((tm, d), lambda i: (i, 0)),
            pl.BlockSpec((1, 1, 1, d), lambda i: (3, i // per_b, 0, 0)),
            pl.BlockSpec((1, 1, 1, d), lambda i: (4, i // per_b, 0, 0)),
            pl.BlockSpec((n_exp, d), lambda i: (0, 0)),
            pl.BlockSpec((n_exp, d), lambda i: (0, 0)),
            pl.BlockSpec((n_exp, 1), lambda i: (0, 0)),
        ],
        out_specs=[out_spec, out_spec, out_spec, pl.BlockSpec((n_exp, 1), lambda i: (0, 0))],
        out_shape=[
            jax.ShapeDtypeStruct((TOP_K, t), jnp.int32),
            jax.ShapeDtypeStruct((TOP_K, t), F32),
            jax.ShapeDtypeStruct((TOP_K, t), jnp.int32),
            jax.ShapeDtypeStruct((n_exp, 1), jnp.int32),
        ],
        scratch_shapes=[pltpu.VMEM((n_exp, 1), F32)],
        compiler_params=_cparams(1),
        name="moe_router",
    )(x.reshape(t, d), mod, mod, w_hi, w_lo, b_corr.reshape(n_exp, 1).astype(F32))


def _slots_kernel(ps_ref, idx_ref, rank_ref, dest_ref, *, n_exp):
    idx = idx_ref[...]

    def add_start(e, acc):
        return acc + jnp.where(idx == e, ps_ref[e], 0)

    dest_ref[...] = lax.fori_loop(0, n_exp, add_start, rank_ref[...])


def _slots(idx, rank, pad_starts):
    k, t = idx.shape
    tm = _tile(TM_SLOTS, t)
    grid_spec = pltpu.PrefetchScalarGridSpec(
        num_scalar_prefetch=1,
        grid=(t // tm,),
        in_specs=[pl.BlockSpec((k, tm), lambda i, ps: (0, i)), pl.BlockSpec((k, tm), lambda i, ps: (0, i))],
        out_specs=pl.BlockSpec((k, tm), lambda i, ps: (0, i)),
    )
    return pl.pallas_call(
        functools.partial(_slots_kernel, n_exp=pad_starts.shape[0]),
        grid_spec=grid_spec,
        out_shape=jax.ShapeDtypeStruct((k, t), jnp.int32),
        compiler_params=_cparams(1),
        name="moe_slots",
    )(pad_starts, idx, rank)


def _to_slabs(value):
    n_seg = value.shape[1] // LANES
    by_seg = jnp.stack([value[:, s * LANES:(s + 1) * LANES] for s in range(n_seg)], axis=0)
    return jnp.swapaxes(by_seg, 0, 1)


def _from_slabs(slabs):
    by_seg = jnp.swapaxes(slabs, 0, 1)
    return jnp.concatenate([by_seg[s] for s in range(slabs.shape[1])], axis=1)


def _dispatch_kernel(pend_ref, padded_ref, x_ref, sh_ref, sc_ref, dest_ref, wsg_ref, wsu_ref, wsd_ref,
                     xs_ref, shared_ref, tok_ref, zero_ref, sem, zsem, *, n_exp, tm, bm):
    i = pl.program_id(0)

    def zero_copy(e):
        return pltpu.make_async_copy(zero_ref, xs_ref.at[pl.ds(pend_ref[e] - bm, bm)], zsem)

    @pl.when(i == 0)
    def _():
        zero_ref[...] = jnp.zeros_like(zero_ref)

        def start(e, _):
            @pl.when(padded_ref[e] > 0)
            def _():
                zero_copy(e).start()
            return 0

        def wait(e, _):
            @pl.when(padded_ref[e] > 0)
            def _():
                zero_copy(e).wait()
            return 0

        def tail_copy(blk):
            return pltpu.make_async_copy(zero_ref, xs_ref.at[pl.ds(blk * bm, bm)], zsem)

        def start_tail(blk, _):
            tail_copy(blk).start()
            return 0

        def wait_tail(blk, _):
            tail_copy(blk).wait()
            return 0

        first_unused = pend_ref[n_exp - 1] // bm
        n_blk = xs_ref.shape[0] // bm
        lax.fori_loop(0, n_exp, start, 0)
        lax.fori_loop(first_unused, n_blk, start_tail, 0)
        lax.fori_loop(0, n_exp, wait, 0)
        lax.fori_loop(first_unused, n_blk, wait_tail, 0)

    n_steps = pl.num_programs(0)
    slot = i % 2

    def drain(buf):
        for k in range(TOP_K):
            pltpu.make_async_copy(tok_ref.at[buf], xs_ref.at[pl.ds(0, tm)], sem.at[buf]).wait()

    @pl.when(i >= 2)
    def _():
        drain(slot)

    tok = x_ref[...] * (1.0 + sc_ref[0, 0]) + sh_ref[0, 0]
    tok_ref[slot] = _to_slabs(tok)
    for r in range(tm):
        for k in range(TOP_K):
            pltpu.make_async_copy(tok_ref.at[slot, r], xs_ref.at[dest_ref[k, r]],
                                  sem.at[slot]).start(priority=k % 2)
    tok = tok.astype(BF16)
    hid = (_silu(_bdot(tok, wsg_ref[...])) * _bdot(tok, wsu_ref[...])).astype(BF16)
    shared_ref[...] = _bdot(hid, wsd_ref[...])

    @pl.when(i == n_steps - 1)
    def _():
        @pl.when(i >= 1)
        def _():
            drain(1 - slot)
        drain(slot)


def _dispatch(x, mod, dest, pad_ends, padded, n_slots, bm, ws_gate, ws_up, ws_down):
    nb, s, d = x.shape
    f = ws_gate.shape[-1]
    t = nb * s
    tm = _tile(TM_DISPATCH, s)
    per_b = s // tm
    n_exp = pad_ends.shape[0]
    n_seg = d // LANES
    assert n_seg % SUBLANES == 0, "token slabs must be whole (8, 128) tiles"
    kern = functools.partial(_dispatch_kernel, n_exp=n_exp, tm=tm, bm=bm)
    grid_spec = pltpu.PrefetchScalarGridSpec(
        num_scalar_prefetch=2,
        grid=(t // tm,),
        in_specs=[
            pl.BlockSpec((tm, d), lambda i, pe, pd: (i, 0)),
            pl.BlockSpec((1, 1, 1, d), lambda i, pe, pd: (3, i // per_b, 0, 0)),
            pl.BlockSpec((1, 1, 1, d), lambda i, pe, pd: (4, i // per_b, 0, 0)),
            pl.BlockSpec((TOP_K, tm), lambda i, pe, pd: (0, i), memory_space=pltpu.SMEM),
            pl.BlockSpec((d, f), lambda i, pe, pd: (0, 0)),
            pl.BlockSpec((d, f), lambda i, pe, pd: (0, 0)),
            pl.BlockSpec((f, d), lambda i, pe, pd: (0, 0)),
        ],
        out_specs=[pl.BlockSpec(memory_space=pl.ANY), pl.BlockSpec((tm, d), lambda i, pe, pd: (i, 0))],
        scratch_shapes=[
            pltpu.VMEM((2, tm, n_seg, LANES), F32),
            pltpu.VMEM((bm, n_seg, LANES), F32),
            pltpu.SemaphoreType.DMA((2,)),
            pltpu.SemaphoreType.DMA(()),
        ],
    )
    return pl.pallas_call(
        kern,
        grid_spec=grid_spec,
        out_shape=[jax.ShapeDtypeStruct((n_slots, n_seg, LANES), F32), jax.ShapeDtypeStruct((t, d), F32)],
        compiler_params=_cparams(1),
        name="moe_dispatch",
    )(pad_ends, padded, x.reshape(t, d), mod, mod, dest, ws_gate, ws_up, ws_down)


def _experts_kernel(be_ref, nu_ref, xs_ref, wg_ref, wu_ref, wd_ref, ys_ref):
    b = pl.program_id(0)

    @pl.when(b < nu_ref[0])
    def _():
        x = _from_slabs(xs_ref[...]).astype(BF16)
        g = _bdot(x, wg_ref[0])
        u = _bdot(x, wu_ref[0])
        ys_ref[...] = _to_slabs(_bdot((_silu(g) * u).astype(BF16), wd_ref[0]))

    @pl.when(b >= nu_ref[0])
    def _():
        ys_ref[...] = jnp.zeros_like(ys_ref)


def _grouped_experts(xs, blk_e, n_used, w_gate, w_up, w_down, bm):
    d, f = w_gate.shape[1:]
    n_seg = d // LANES
    n_blk = xs.shape[0] // bm

    def blk(b, be, nu):
        return jnp.minimum(b, nu[0] - 1)

    grid_spec = pltpu.PrefetchScalarGridSpec(
        num_scalar_prefetch=2,
        grid=(n_blk,),
        in_specs=[
            pl.BlockSpec((bm, n_seg, LANES), lambda b, be, nu: (blk(b, be, nu), 0, 0)),
            pl.BlockSpec((1, d, f), lambda b, be, nu: (be[blk(b, be, nu)], 0, 0)),
            pl.BlockSpec((1, d, f), lambda b, be, nu: (be[blk(b, be, nu)], 0, 0)),
            pl.BlockSpec((1, f, d), lambda b, be, nu: (be[blk(b, be, nu)], 0, 0)),
        ],
        out_specs=pl.BlockSpec((bm, n_seg, LANES), lambda b, be, nu: (b, 0, 0)),
    )
    return pl.pallas_call(
        _experts_kernel,
        grid_spec=grid_spec,
        out_shape=jax.ShapeDtypeStruct(xs.shape, F32),
        compiler_params=_cparams(1),
        name="moe_experts",
    )(blk_e, n_used, xs, w_gate, w_up, w_down)


def _combine_kernel(x_ref, gt_ref, dest_ref, dest_next_ref, wt_ref, ys_ref, shared_ref,
                    lng_ref, lnb_ref, xo_ref, rows_ref, sem, *, tm, alpha):
    i = pl.program_id(0)
    n_steps = pl.num_programs(0)
    slot = i % 2

    def gather(d_ref, buf):
        def issue(r, _):
            for k in range(TOP_K):
                pltpu.make_async_copy(ys_ref.at[d_ref[k, r]], rows_ref.at[buf, k, r],
                                      sem.at[buf]).start(priority=k % 2)
            return 0
        lax.fori_loop(0, tm, issue, 0)

    @pl.when(i == 0)
    def _():
        gather(dest_ref, 0)

    for r in range(tm):
        for k in range(TOP_K):
            pltpu.make_async_copy(ys_ref.at[dest_next_ref[k, r]], rows_ref.at[1 - slot, k, r],
                                  sem.at[1 - slot]).start(priority=k % 2)

    x = x_ref[...]
    f = shared_ref[...]

    for k in range(TOP_K):
        pltpu.make_async_copy(ys_ref.at[pl.ds(0, tm)], rows_ref.at[slot, k], sem.at[slot]).wait()
    n_seg = rows_ref.shape[3]
    wt = wt_ref[...]
    routed = jnp.broadcast_to(wt[:, 0:1, :], (tm, n_seg, LANES)) * rows_ref[slot, 0]
    for k in range(1, TOP_K):
        routed = routed + jnp.broadcast_to(wt[:, k:k + 1, :], (tm, n_seg, LANES)) * rows_ref[slot, k]
    f = f + _from_slabs(routed)
    v = alpha * x + gt_ref[0, 0] * f
    xo_ref[...] = _layer_norm(v, lng_ref[...], lnb_ref[...])

    @pl.when(i == n_steps - 1)
    def _():
        for k in range(TOP_K):
            pltpu.make_async_copy(ys_ref.at[pl.ds(0, tm)], rows_ref.at[1 - slot, k], sem.at[1 - slot]).wait()


def _combine(x, mod, dest, wt_splat, ys, shared, ln_g, ln_b, alpha):
    nb, s, d = x.shape
    t = nb * s
    tm = _tile(TM_COMBINE, s)
    per_b = s // tm
    n_seg = d // LANES
    kern = functools.partial(_combine_kernel, tm=tm, alpha=alpha)
    vspec = pl.BlockSpec((1, d), lambda i: (0, 0))
    n_steps = t // tm
    out = pl.pallas_call(
        kern,
        grid=(n_steps,),
        in_specs=[
            pl.BlockSpec((tm, d), lambda i: (i, 0)),
            pl.BlockSpec((1, 1, 1, d), lambda i: (5, i // per_b, 0, 0)),
            pl.BlockSpec((TOP_K, tm), lambda i: (0, i), memory_space=pltpu.SMEM),
            pl.BlockSpec((TOP_K, tm), lambda i: (0, jnp.minimum(i + 1, n_steps - 1)), memory_space=pltpu.SMEM),
            pl.BlockSpec((tm, TOP_K, LANES), lambda i: (i, 0, 0)),
            pl.BlockSpec(memory_space=pl.ANY),
            pl.BlockSpec((tm, d), lambda i: (i, 0)),
            vspec, vspec,
        ],
        out_specs=pl.BlockSpec((tm, d), lambda i: (i, 0)),
        out_shape=jax.ShapeDtypeStruct((t, d), F32),
        scratch_shapes=[pltpu.VMEM((2, TOP_K, tm, n_seg, LANES), F32), pltpu.SemaphoreType.DMA((2,))],
        compiler_params=_cparams(1),
        name="moe_combine",
    )(x.reshape(t, d), mod, dest, dest, wt_splat, ys, shared, ln_g.reshape(1, d), ln_b.reshape(1, d))
    return out.reshape(nb, s, d)


def _moe_block(x, mod, w_router, b_router, w_gate, w_up, w_down, ws_gate, ws_up, ws_down, ln_g, ln_b, alpha):
    nb, s, d = x.shape
    t = nb * s
    n_exp = w_router.shape[1]
    bm = _tile(MOE_BLOCK, t * TOP_K)
    idx, wt, rank, counts = _route(x, mod, w_router, b_router)

    counts = counts.reshape(n_exp)
    padded = (counts + bm - 1) // bm * bm
    pad_ends = jnp.cumsum(padded).astype(jnp.int32)
    pad_starts = pad_ends - padded
    dest = _slots(idx, rank, pad_starts.astype(jnp.int32))
    n_blk = t * TOP_K // bm + n_exp
    n_used = (pad_ends[-1:] // bm).astype(jnp.int32)
    blk_start = jnp.arange(n_blk, dtype=jnp.int32) * bm
    blk_e = jnp.sum((pad_ends[None, :] <= blk_start[:, None]).astype(jnp.int32), axis=1)
    blk_e = jnp.minimum(blk_e, n_exp - 1).astype(jnp.int32)

    xs, shared = _dispatch(x, mod, dest, pad_ends, padded.astype(jnp.int32), n_blk * bm, bm,
                           ws_gate.astype(BF16), ws_up.astype(BF16), ws_down.astype(BF16))
    ys = _grouped_experts(xs, blk_e, n_used, w_gate, w_up, w_down, bm)
    wt_splat = jnp.broadcast_to(wt.T[:, :, None], (t, TOP_K, LANES))
    return _combine(x, mod, dest, wt_splat, ys, shared, ln_g, ln_b, alpha)


def kernel(x, c, ctx, c_ctx, w_mod, b_mod, ln1_g, ln1_b, ln2_g, ln2_b, rg_w_in, rg_b_in, rg_conv_w, rg_conv_b, rg_w_gates, rg_b_gates, rg_lambda, rg_w_out, rg_b_out, cf_w_pw1, cf_b_pw1, cf_w_dw, cf_b_dw, cf_norm_g, cf_norm_b, cf_w_pw2, cf_b_pw2, moe_w_router, moe_b_router, moe_w_gate, moe_w_up, moe_w_down, sh_w_gate, sh_w_up, sh_w_down):
    nb, s, d = x.shape
    depth = w_mod.shape[0]
    assert depth == 2 and nb + 1 <= SUBLANES
    alpha = (2 * depth) ** 0.25

    c_rows = jnp.zeros((SUBLANES, d), F32).at[:nb].set(c).at[nb].set(c_ctx)
    mod_all = _adaln_vectors(c_rows, w_mod, b_mod)
    mod_all = mod_all.reshape(depth, SUBLANES, 6, 1, d).transpose(0, 2, 1, 3, 4)
    mod_lat = [mod_all[l, :, :nb] for l in range(depth)]
    mod_ctx = jnp.broadcast_to(mod_all[0, :, nb:nb + 1], (6, nb, 1, d))

    n_h, hd = rg_w_gates.shape[3], rg_w_gates.shape[4]
    r = n_h * hd
    w_in = rg_w_in[0].astype(BF16)
    wg = [jnp.concatenate([rg_w_gates[0, dr, 0], rg_w_gates[0, dr, 1]], axis=-1).astype(BF16) for dr in range(2)]
    bg = [jnp.concatenate([rg_b_gates[0, dr, 0], rg_b_gates[0, dr, 1]], axis=-1).reshape(n_h, 1, 2 * hd)
          for dr in range(2)]
    conv_w, conv_b = rg_conv_w[0], rg_conv_b[0]
    zero_state = jnp.zeros((nb, n_h, 1, hd), F32)

    proj_c, _ = _modulated_projection(ctx, mod_ctx, w_in, rg_b_in[0], BF16)
    _, hcf = _rglru_scan(proj_c, conv_w, conv_b, wg[0], bg[0], rg_lambda[0, 0], zero_state, reverse=False)
    _, hcb = _rglru_scan(proj_c, conv_w, conv_b, wg[1], bg[1], rg_lambda[0, 1], zero_state, reverse=True)

    expert_stacks = (moe_w_gate, moe_w_up, moe_w_down)
    proj_x, experts0 = _modulated_projection(x, mod_lat[0], w_in, rg_b_in[0], BF16, expert_stacks, 0)
    hxf, _ = _rglru_scan(proj_x, conv_w, conv_b, wg[0], bg[0], rg_lambda[0, 0], hcf, reverse=False)
    readout = (hxf, rg_w_out[0].astype(BF16), rg_b_out[0], x, mod_lat[0], ln1_g[0], ln1_b[0])
    x, _ = _rglru_scan(proj_x, conv_w, conv_b, wg[1], bg[1], rg_lambda[0, 1], hcb, reverse=True,
                       readout=readout, alpha=alpha)
    x = _moe_block(x, mod_lat[0], moe_w_router[0], moe_b_router[0],
                   *experts0,
                   sh_w_gate[0], sh_w_up[0], sh_w_down[0], ln2_g[0], ln2_b[0], alpha)

    z, experts1 = _modulated_glu(x, mod_lat[1], cf_w_pw1[0].astype(BF16), cf_b_pw1[0], expert_stacks, 1)
    x = _conformer_tail(z, cf_w_dw[0], cf_b_dw[0], cf_norm_g[0], cf_norm_b[0], cf_w_pw2[0].astype(BF16),
                        cf_b_pw2[0], x, mod_lat[1], ln1_g[1], ln1_b[1], alpha)
    x = _moe_block(x, mod_lat[1], moe_w_router[1], moe_b_router[1],
                   *experts1,
                   sh_w_gate[1], sh_w_up[1], sh_w_down[1], ln2_g[1], ln2_b[1], alpha)
    return x
```
